```python
import math
import jax, jax.numpy as jnp
from jax import lax
import numpy as np

D_MODEL = 1024
BATCH = 4
SEQ = 4096
DEPTH = 4
DEC_BATCH = 32
DEC_SEQ = 1
PAST_LEN = 8192
PAGE_SIZE = 128

N_HEADS = 8
HEAD_DIM = 64
KV_HEADS = 2
GROUP = N_HEADS // KV_HEADS
NSA_WIDTH = N_HEADS * HEAD_DIM
KV_WIDTH = KV_HEADS * HEAD_DIM
CMP_BLOCK = 32
SEL_BLOCK = 64
SEL_RATIO = SEL_BLOCK // CMP_BLOCK
TOP_K = 16
WINDOW = 512
Q_BLOCK = 128
N_NSA_BRANCH = 3
POOL_WIDTH = 512
POOL_GROUPS = 4
POOL_GROUP_WIDTH = POOL_WIDTH // POOL_GROUPS
POOL_WINDOWS = (2, 4, 8, 16)
POOL_BUF = 15
MEM_TOKENS = 256
MEM_HEADS = 4
MEM_HEAD_DIM = 128
MEM_WIDTH = MEM_HEADS * MEM_HEAD_DIM
N_BRANCH = 3
D_FF = ((8 * D_MODEL // 3 + 255) // 256) * 256
PROJ_WIDTH = NSA_WIDTH + 6 * KV_WIDTH + N_NSA_BRANCH * N_HEADS + POOL_WIDTH + MEM_WIDTH + N_BRANCH * D_MODEL
EPS = 1e-6
NEG = -1e30
BIG = 1e9

kernel_name = "nsa_pool_memory_hybrid_step"


def rmsnorm(x, g):
    xf = x.astype(jnp.float32)
    y = xf * lax.rsqrt(jnp.mean(xf * xf, axis=-1, keepdims=True) + EPS)
    return (y * g.astype(jnp.float32)).astype(x.dtype)


def masked_softmax(s, mask):
    s = jnp.where(mask, s.astype(jnp.float32), NEG)
    m = jnp.max(s, axis=-1, keepdims=True)
    p = jnp.exp(s - m) * mask
    return p / jnp.maximum(jnp.sum(p, axis=-1, keepdims=True), 1e-30)


def _split_points():
    sizes = (NSA_WIDTH,) + (KV_WIDTH,) * 6 + (N_NSA_BRANCH * N_HEADS, POOL_WIDTH, MEM_WIDTH, N_BRANCH * D_MODEL)
    pts, acc = [], 0
    for s in sizes[:-1]:
        acc += s
        pts.append(acc)
    return pts


def project_inputs(x, l, w):
    b, t = x.shape[:2]
    z = rmsnorm(x, w['attn_norm_g'][l]) @ w['w_in'][l]
    q, kc, vc, ks, vs, kw, vw, nsa_g, pool_u, mem_q, merge_g = jnp.split(z, _split_points(), axis=-1)
    kv = lambda a: a.reshape(b, t, KV_HEADS, HEAD_DIM)
    q = rmsnorm(q.reshape(b, t, KV_HEADS, GROUP, HEAD_DIM), w['nsa_q_g'][l])
    ks = rmsnorm(kv(ks), w['nsa_ks_g'][l])
    kw = rmsnorm(kv(kw), w['nsa_kw_g'][l])
    mem_q = rmsnorm(mem_q.reshape(b, t, MEM_HEADS, MEM_HEAD_DIM), w['mem_q_g'][l])
    nsa_kv = jnp.stack([kv(kc), kv(vc), ks, kv(vs)], axis=2)
    win_kv = jnp.stack([kw, kv(vw)], axis=2)
    return q, nsa_kv, win_kv, nsa_g, pool_u, mem_q, merge_g


def compress_blocks(k_raw, pe, wc):
    b, l = k_raw.shape[:2]
    nc = l // CMP_BLOCK
    kb = k_raw[:, :nc * CMP_BLOCK].reshape(b, nc, CMP_BLOCK, KV_HEADS, HEAD_DIM)
    pooled = jnp.mean(kb + pe[None, None, :, None, :], axis=2)
    return jnp.einsum('bckd,de->bcke', pooled, wc)


def nsa_cmp_sel(q, q_pos, kc_raw, vc_raw, ks, vs, l, w):
    b, t = q.shape[:2]
    L = ks.shape[1]
    scale = HEAD_DIM ** -0.5
    kc = rmsnorm(compress_blocks(kc_raw, w['cmp_pe_k'][l], w['cmp_wk'][l]), w['nsa_kc_g'][l])
    vc = compress_blocks(vc_raw, w['cmp_pe_v'][l], w['cmp_wv'][l])
    nc = kc.shape[1]
    c_end = jnp.arange(nc) * CMP_BLOCK + (CMP_BLOCK - 1)
    c_mask = c_end[None, :] <= q_pos[:, None]
    s = jnp.einsum('btkgd,bckd->bkgtc', q, kc) * scale
    p_cmp = masked_softmax(s, c_mask)
    o_cmp = jnp.einsum('bkgtc,bckd->btkgd', p_cmp.astype(vc.dtype), vc)
    ns = (L + SEL_BLOCK - 1) // SEL_BLOCK
    imp = jnp.sum(p_cmp, axis=2)
    imp = jnp.pad(imp, ((0, 0), (0, 0), (0, 0), (0, ns * SEL_RATIO - nc)))
    imp = imp.reshape(b, KV_HEADS, t, ns, SEL_RATIO).sum(-1)
    blk = jnp.arange(ns)[None, :]
    cur = q_pos[:, None] // SEL_BLOCK
    valid = blk * SEL_BLOCK <= q_pos[:, None]
    forced = (blk == 0) | (blk == cur) | (blk == cur - 1)
    score = jnp.where(valid, jnp.where(forced, BIG, imp), -BIG)
    k_sel = min(TOP_K, ns)
    _, idx = lax.top_k(score, k_sel)
    pad = ns * SEL_BLOCK - L
    to_blocks = lambda a: jnp.pad(a, ((0, 0), (0, pad), (0, 0), (0, 0))).reshape(b, ns, SEL_BLOCK, KV_HEADS, HEAD_DIM).transpose(0, 3, 1, 2, 4)
    ks_b = to_blocks(ks)
    vs_b = to_blocks(vs)
    tb = Q_BLOCK if t % Q_BLOCK == 0 else t
    nb = t // tb
    q_blocks = q.reshape(b, nb, tb, KV_HEADS, GROUP, HEAD_DIM).transpose(1, 0, 2, 3, 4, 5)
    idx_blocks = idx.reshape(b, KV_HEADS, nb, tb, k_sel).transpose(2, 0, 1, 3, 4)
    pos_blocks = q_pos.reshape(nb, tb)
    bi = jnp.arange(b)[:, None, None, None]
    hi = jnp.arange(KV_HEADS)[None, :, None, None]
    sel_offsets = jnp.arange(SEL_BLOCK)

    def sel_block(args):
        qb, ib, pb = args
        kg = ks_b[bi, hi, ib]
        vg = vs_b[bi, hi, ib]
        kpos = ib[..., None] * SEL_BLOCK + sel_offsets
        mask = (kpos <= pb[None, None, :, None, None]).reshape(b, KV_HEADS, 1, tb, k_sel * SEL_BLOCK)
        sc = jnp.einsum('btkgd,bktjsd->bkgtjs', qb, kg).reshape(b, KV_HEADS, GROUP, tb, k_sel * SEL_BLOCK) * scale
        pr = masked_softmax(sc, mask).reshape(b, KV_HEADS, GROUP, tb, k_sel, SEL_BLOCK)
        return jnp.einsum('bkgtjs,bktjsd->btkgd', pr.astype(vg.dtype), vg)

    o_sel = lax.map(sel_block, (q_blocks, idx_blocks, pos_blocks))
    o_sel = o_sel.transpose(1, 0, 2, 3, 4, 5).reshape(b, t, KV_HEADS, GROUP, HEAD_DIM)
    return o_cmp, o_sel


def gqa_window(q, k, v, q_pos, k_pos):
    mask = (k_pos[:, None, :] <= q_pos[:, :, None]) & (k_pos[:, None, :] > q_pos[:, :, None] - WINDOW)
    s = jnp.einsum('bntkgd,bnskd->bnkgts', q, k) * HEAD_DIM ** -0.5
    p = masked_softmax(s, mask[None, :, None, None])
    return jnp.einsum('bnkgts,bnskd->bntkgd', p.astype(v.dtype), v)


def window_prompt(q, kw, vw):
    b, t = q.shape[:2]
    nb = t // Q_BLOCK
    nw = WINDOW // Q_BLOCK
    band = lambda a: jnp.concatenate(
        [jnp.pad(a, ((0, 0), (WINDOW, 0), (0, 0), (0, 0))).reshape(b, nb + nw, Q_BLOCK, KV_HEADS, HEAD_DIM)[:, i:i + nb]
         for i in range(nw + 1)], axis=2)
    q_pos = jnp.arange(t).reshape(nb, Q_BLOCK)
    k_pos = jnp.arange(nb)[:, None] * Q_BLOCK - WINDOW + jnp.arange((nw + 1) * Q_BLOCK)[None, :]
    o = gqa_window(q.reshape(b, nb, Q_BLOCK, KV_HEADS, GROUP, HEAD_DIM), band(kw), band(vw), q_pos, k_pos)
    return o.reshape(b, t, KV_HEADS, GROUP, HEAD_DIM)


def pool_mix(u_ext, n_prev, pos, w_pool, pool_scale):
    b = u_ext.shape[0]
    t = pos.shape[0]
    uf = u_ext.astype(jnp.float32)
    cs = jnp.pad(jnp.cumsum(uf, axis=1), ((0, 0), (1, 0), (0, 0)))
    end = cs[:, n_prev + 1:]
    outs = []
    for gi, win in enumerate(POOL_WINDOWS):
        lo, hi = gi * POOL_GROUP_WIDTH, (gi + 1) * POOL_GROUP_WIDTH
        start = cs[:, n_prev + 1 - win:n_prev + 1 - win + t, lo:hi]
        cnt = jnp.minimum(pos + 1, win).astype(jnp.float32)[None, :, None]
        outs.append((end[..., lo:hi] - start) / cnt)
    d = (jnp.concatenate(outs, axis=-1) - uf[:, n_prev:]).astype(u_ext.dtype)
    d = d.reshape(b, t, POOL_GROUPS, POOL_GROUP_WIDTH)
    y = jnp.einsum('btgc,gce->btge', d, w_pool).reshape(b, t, POOL_WIDTH)
    return y * pool_scale


def memory_kv(mem, l, w):
    b, m = mem.shape[:2]
    kv = (rmsnorm(mem, w['mem_norm_g'][l]) @ w['w_mem_kv'][l]).reshape(b, m, 2, MEM_HEADS, MEM_HEAD_DIM)
    return jnp.stack([rmsnorm(kv[:, :, 0], w['mem_k_g'][l]), kv[:, :, 1]], axis=2)


def mem_attend(q, mk, mv):
    s = jnp.einsum('bthd,bmhd->bhtm', q, mk).astype(jnp.float32) * MEM_HEAD_DIM ** -0.5
    p = jax.nn.softmax(s, axis=-1)
    return jnp.einsum('bhtm,bmhd->bthd', p.astype(mv.dtype), mv)


def merge_and_ffn(x, l, w, o_cmp, o_sel, o_win, nsa_g, pool_y, mem_o, merge_g):
    b, t = x.shape[:2]
    g = jax.nn.sigmoid(nsa_g.astype(jnp.float32)).reshape(b, t, KV_HEADS, GROUP, N_NSA_BRANCH).astype(x.dtype)
    o_nsa = (g[..., 0, None] * o_cmp + g[..., 1, None] * o_sel + g[..., 2, None] * o_win).reshape(b, t, NSA_WIDTH)
    mg = jax.nn.sigmoid(merge_g.astype(jnp.float32)).reshape(b, t, N_BRANCH, D_MODEL).astype(x.dtype)
    h = (mg[:, :, 0] * (o_nsa @ w['w_up_nsa'][l])
         + mg[:, :, 1] * (pool_y @ w['w_up_pool'][l])
         + mg[:, :, 2] * (mem_o.reshape(b, t, MEM_WIDTH) @ w['w_up_mem'][l]))
    x = x + h @ w['w_out'][l]
    gate, up = jnp.split(rmsnorm(x, w['ffn_norm_g'][l]) @ w['w_gate_up'][l], 2, axis=-1)
    return x + (jax.nn.silu(gate) * up) @ w['w_down'][l]


def prompt_layer(x, mem, l, w):
    t = x.shape[1]
    pos = jnp.arange(t)
    q, nsa_kv, win_kv, nsa_g, pool_u, mem_q, merge_g = project_inputs(x, l, w)
    o_cmp, o_sel = nsa_cmp_sel(q, pos, nsa_kv[:, :, 0], nsa_kv[:, :, 1], nsa_kv[:, :, 2], nsa_kv[:, :, 3], l, w)
    o_win = window_prompt(q, win_kv[:, :, 0], win_kv[:, :, 1])
    u_ext = jnp.pad(pool_u, ((0, 0), (POOL_BUF, 0), (0, 0)))
    pool_y = pool_mix(u_ext, POOL_BUF, pos, w['w_pool'][l], w['pool_scale'][l])
    mkv = memory_kv(mem, l, w)
    mem_o = mem_attend(mem_q, mkv[:, :, 0], mkv[:, :, 1])
    y = merge_and_ffn(x, l, w, o_cmp, o_sel, o_win, nsa_g, pool_y, mem_o, merge_g)
    wb = min(WINDOW, t)
    return y, nsa_kv, win_kv[:, t - wb:], pool_u[:, t - POOL_BUF:], mkv


def sample_layer(x, l, w, nsa_pool_l, page_table, win_buf, pool_buf, mkv):
    b, t = x.shape[:2]
    pos = PAST_LEN + jnp.arange(t)
    q, nsa_kv, win_kv, nsa_g, pool_u, mem_q, merge_g = project_inputs(x, l, w)
    past = nsa_pool_l[page_table].reshape(b, -1, 4, KV_HEADS, HEAD_DIM)
    full = jnp.concatenate([past, nsa_kv], axis=1)
    o_cmp, o_sel = nsa_cmp_sel(q, pos, full[:, :, 0], full[:, :, 1], full[:, :, 2], full[:, :, 3], l, w)
    wkv = jnp.concatenate([win_buf, win_kv], axis=1)
    wb = win_buf.shape[1]
    k_pos = PAST_LEN - wb + jnp.arange(wb + t)
    o_win = gqa_window(q[:, None], wkv[:, None, :, 0], wkv[:, None, :, 1], pos[None], k_pos[None])[:, 0]
    u_ext = jnp.concatenate([pool_buf, pool_u], axis=1)
    pool_y = pool_mix(u_ext, POOL_BUF, pos, w['w_pool'][l], w['pool_scale'][l])
    mem_o = mem_attend(mem_q, mkv[:, :, 0], mkv[:, :, 1])
    y = merge_and_ffn(x, l, w, o_cmp, o_sel, o_win, nsa_g, pool_y, mem_o, merge_g)
    return y, nsa_kv, wkv[:, t:], u_ext[:, t:]


def setup_inputs(seed: int = 0) -> dict:
    key = jax.random.key(seed)
    ks = jax.random.split(key, 40)
    f32 = jnp.float32
    nrm = lambda k, shape, sc: jax.random.normal(k, shape, f32) * sc
    gain = lambda k, shape: 1.0 + 0.05 * jax.random.normal(k, shape, f32)
    n_pages = PAST_LEN // PAGE_SIZE
    n_used = DEC_BATCH * n_pages
    n_phys = n_used + n_used // 4
    win_buf = min(WINDOW, PAST_LEN)
    page_table = jax.random.permutation(ks[0], n_phys)[:n_used].reshape(DEC_BATCH, n_pages).astype(jnp.int32)
    return {
        'x_prompt': nrm(ks[1], (BATCH, SEQ, D_MODEL), 1.0),
        'x_sample': nrm(ks[2], (DEC_BATCH, DEC_SEQ, D_MODEL), 1.0),
        'mem_prompt': nrm(ks[3], (BATCH, MEM_TOKENS, D_MODEL), 1.0),
        'cache_nsa_kv': nrm(ks[4], (DEPTH, n_phys, PAGE_SIZE, 4, KV_HEADS, HEAD_DIM), 1.0),
        'cache_win_kv': nrm(ks[5], (DEPTH, DEC_BATCH, win_buf, 2, KV_HEADS, HEAD_DIM), 1.0),
        'state_pool': nrm(ks[6], (DEPTH, DEC_BATCH, POOL_BUF, POOL_WIDTH), 1.0),
        'cache_mem_kv': nrm(ks[7], (DEPTH, DEC_BATCH, MEM_TOKENS, 2, MEM_HEADS, MEM_HEAD_DIM), 1.0),
        'page_table': page_table,
        'attn_norm_g': gain(ks[8], (DEPTH, D_MODEL)),
        'w_in': nrm(ks[9], (DEPTH, D_MODEL, PROJ_WIDTH), D_MODEL ** -0.5),
        'nsa_q_g': gain(ks[10], (DEPTH, HEAD_DIM)),
        'nsa_kc_g': gain(ks[11], (DEPTH, HEAD_DIM)),
        'nsa_ks_g': gain(ks[12], (DEPTH, HEAD_DIM)),
        'nsa_kw_g': gain(ks[13], (DEPTH, HEAD_DIM)),
        'cmp_pe_k': nrm(ks[14], (DEPTH, CMP_BLOCK, HEAD_DIM), 0.1),
        'cmp_pe_v': nrm(ks[15], (DEPTH, CMP_BLOCK, HEAD_DIM), 0.1),
        'cmp_wk': nrm(ks[16], (DEPTH, HEAD_DIM, HEAD_DIM), HEAD_DIM ** -0.5),
        'cmp_wv': nrm(ks[17], (DEPTH, HEAD_DIM, HEAD_DIM), HEAD_DIM ** -0.5),
        'w_pool': nrm(ks[18], (DEPTH, POOL_GROUPS, POOL_GROUP_WIDTH, POOL_GROUP_WIDTH), POOL_GROUP_WIDTH ** -0.5),
        'pool_scale': gain(ks[19], (DEPTH, POOL_WIDTH)),
        'mem_norm_g': gain(ks[20], (DEPTH, D_MODEL)),
        'w_mem_kv': nrm(ks[21], (DEPTH, D_MODEL, 2 * MEM_WIDTH), D_MODEL ** -0.5),
        'mem_q_g': gain(ks[22], (DEPTH, MEM_HEAD_DIM)),
        'mem_k_g': gain(ks[23], (DEPTH, MEM_HEAD_DIM)),
        'w_up_nsa': nrm(ks[24], (DEPTH, NSA_WIDTH, D_MODEL), NSA_WIDTH ** -0.5),
        'w_up_pool': nrm(ks[25], (DEPTH, POOL_WIDTH, D_MODEL), POOL_WIDTH ** -0.5),
        'w_up_mem': nrm(ks[26], (DEPTH, MEM_WIDTH, D_MODEL), MEM_WIDTH ** -0.5),
        'w_out': nrm(ks[27], (DEPTH, D_MODEL, D_MODEL), D_MODEL ** -0.5),
        'ffn_norm_g': gain(ks[28], (DEPTH, D_MODEL)),
        'w_gate_up': nrm(ks[29], (DEPTH, D_MODEL, 2 * D_FF), D_MODEL ** -0.5),
        'w_down': nrm(ks[30], (DEPTH, D_FF, D_MODEL), D_FF ** -0.5),
    }


def reference(x_prompt, x_sample, mem_prompt, cache_nsa_kv, cache_win_kv, state_pool, cache_mem_kv, page_table,
              attn_norm_g, w_in, nsa_q_g, nsa_kc_g, nsa_ks_g, nsa_kw_g, cmp_pe_k, cmp_pe_v, cmp_wk, cmp_wv,
              w_pool, pool_scale, mem_norm_g, w_mem_kv, mem_q_g, mem_k_g, w_up_nsa, w_up_pool, w_up_mem, w_out,
              ffn_norm_g, w_gate_up, w_down):
    w = {'attn_norm_g': attn_norm_g, 'w_in': w_in, 'nsa_q_g': nsa_q_g, 'nsa_kc_g': nsa_kc_g,
         'nsa_ks_g': nsa_ks_g, 'nsa_kw_g': nsa_kw_g, 'cmp_pe_k': cmp_pe_k, 'cmp_pe_v': cmp_pe_v,
         'cmp_wk': cmp_wk, 'cmp_wv': cmp_wv, 'w_pool': w_pool, 'pool_scale': pool_scale,
         'mem_norm_g': mem_norm_g, 'w_mem_kv': w_mem_kv, 'mem_q_g': mem_q_g, 'mem_k_g': mem_k_g,
         'w_up_nsa': w_up_nsa, 'w_up_pool': w_up_pool, 'w_up_mem': w_up_mem, 'w_out': w_out,
         'ffn_norm_g': ffn_norm_g, 'w_gate_up': w_gate_up, 'w_down': w_down}
    xp, xs = x_prompt, x_sample
    nsa_p, nsa_s, win_p, win_s, pool_p, pool_s, mem_p = [], [], [], [], [], [], []
    for l in range(DEPTH):
        xp, a, bw, c, d = prompt_layer(xp, mem_prompt, l, w)
        nsa_p.append(a); win_p.append(bw); pool_p.append(c); mem_p.append(d)
        xs, a, bw, c = sample_layer(xs, l, w, cache_nsa_kv[l], page_table, cache_win_kv[l], state_pool[l], cache_mem_kv[l])
        nsa_s.append(a); win_s.append(bw); pool_s.append(c)
    return (xp, xs, jnp.stack(nsa_p), jnp.stack(nsa_s), jnp.stack(win_p), jnp.stack(win_s),
            jnp.stack(pool_p), jnp.stack(pool_s), jnp.stack(mem_p))
```

```python
import functools

import jax
import jax.numpy as jnp
from jax import lax
from jax.experimental import pallas as pl
from jax.experimental.pallas import tpu as pltpu

F32 = jnp.float32
BF16 = jnp.bfloat16

D_MODEL = 1024
PAGE_SIZE = 128
N_HEADS = 8
HEAD_DIM = 64
KV_HEADS = 2
GROUP = N_HEADS // KV_HEADS
NSA_WIDTH = N_HEADS * HEAD_DIM
KV_WIDTH = KV_HEADS * HEAD_DIM
CMP_BLOCK = 32
SEL_BLOCK = 64
TOP_K = 16
WINDOW = 512
N_NSA_BRANCH = 3
POOL_WIDTH = 512
POOL_GROUPS = 4
POOL_GROUP_WIDTH = POOL_WIDTH // POOL_GROUPS
POOL_WINDOWS = (2, 4, 8, 16)
POOL_BUF = 15
MEM_HEADS = 4
MEM_HEAD_DIM = 128
MEM_WIDTH = MEM_HEADS * MEM_HEAD_DIM
N_BRANCH = 3
D_FF = ((8 * D_MODEL // 3 + 255) // 256) * 256
EPS = 1e-6
NEG = -1e30
BIG = 1e9

V7X_LANES = 128
V7X_SUBLANES = 8
V7X_VMEM_LIMIT_BYTES = 56 * 1024 * 1024

C_Q = 0
C_NSAKV = C_Q + NSA_WIDTH
C_WINKV = C_NSAKV + 4 * KV_WIDTH
C_POOL = C_WINKV + 2 * KV_WIDTH
C_MEMQ = C_POOL + POOL_WIDTH
C_MERGE = C_MEMQ + MEM_WIDTH
C_NSAG = C_MERGE + N_BRANCH * D_MODEL
NSAG_PAD = V7X_LANES
PROJ_PAD = C_NSAG + NSAG_PAD

TM_PROJ = 256
TQ = 128
KC_SEL = 512
FF_CHUNK = D_FF // 2
PAGES_PER_STEP_CMP = 32
PAGES_PER_STEP_SEL = 16


def _dot(a, b):
    return jnp.dot(a, b, preferred_element_type=F32)


def _dot_nt(a, b):
    return lax.dot_general(a, b, (((1,), (1,)), ((), ())), preferred_element_type=F32)


def _dot_f32(a, b):
    return jnp.dot(a, b, preferred_element_type=F32, precision=lax.Precision.HIGHEST)


def _split_bf16(a, n):
    parts, r = [], a
    for _ in range(n):
        p = r.astype(BF16)
        parts.append(p)
        r = r - p.astype(F32)
    return parts


def _dot_split_lhs(a, b, n):
    return sum(_dot(p, b) for p in _split_bf16(a, n))


def _dot_split_rhs(a, b, n):
    return sum(_dot(a, p) for p in _split_bf16(b, n))


def _iota(shape, dim):
    return lax.broadcasted_iota(jnp.int32, shape, dim)


def _rms_rows(x, g):
    return x * lax.rsqrt(jnp.mean(x * x, axis=-1, keepdims=True) + EPS) * g


def _seg_rmsnorm(v, ones_bd, gain, seg):
    ssq = _dot_split_lhs(v * v, ones_bd, 2)
    return v * lax.rsqrt(ssq * (1.0 / seg) + EPS) * gain


def _const_spec(shape):
    return pl.BlockSpec(shape, lambda *_: (0,) * len(shape), pipeline_mode=pl.Buffered(1))


def _params(*sem):
    return pltpu.CompilerParams(dimension_semantics=sem, vmem_limit_bytes=V7X_VMEM_LIMIT_BYTES)


def _inproj_kernel(x_ref, g_ref, w_ref, qg_ref, ksg_ref, kwg_ref, mqg_ref, ones64_ref, ones128_ref,
                   q_ref, nsakv_ref, winkv_ref, kvbf_ref, pool_ref, memq_ref, mg_ref, ng_ref):
    hb = _rms_rows(x_ref[...], g_ref[...]).astype(BF16)

    def proj(lo, width):
        return _dot(hb, w_ref[:, lo:lo + width])

    ones_kv = ones64_ref[0:KV_WIDTH, 0:KV_WIDTH]
    q = _seg_rmsnorm(proj(C_Q, NSA_WIDTH), ones64_ref[...], qg_ref[...], HEAD_DIM)
    q_ref[...] = (q * HEAD_DIM ** -0.5).astype(BF16)
    kcvc = proj(C_NSAKV, 2 * KV_WIDTH)
    ks = _seg_rmsnorm(proj(C_NSAKV + 2 * KV_WIDTH, KV_WIDTH), ones_kv, ksg_ref[...], HEAD_DIM)
    vs = proj(C_NSAKV + 3 * KV_WIDTH, KV_WIDTH)
    kw = _seg_rmsnorm(proj(C_WINKV, KV_WIDTH), ones_kv, kwg_ref[...], HEAD_DIM)
    vw = proj(C_WINKV + KV_WIDTH, KV_WIDTH)
    nsakv_ref[:, 0:2 * KV_WIDTH] = kcvc
    nsakv_ref[:, 2 * KV_WIDTH:3 * KV_WIDTH] = ks
    nsakv_ref[:, 3 * KV_WIDTH:4 * KV_WIDTH] = vs
    winkv_ref[:, 0:KV_WIDTH] = kw
    winkv_ref[:, KV_WIDTH:2 * KV_WIDTH] = vw
    kvbf_ref[:, 0:KV_WIDTH] = ks.astype(BF16)
    kvbf_ref[:, KV_WIDTH:2 * KV_WIDTH] = vs.astype(BF16)
    kvbf_ref[:, 2 * KV_WIDTH:3 * KV_WIDTH] = kw.astype(BF16)
    kvbf_ref[:, 3 * KV_WIDTH:4 * KV_WIDTH] = vw.astype(BF16)
    pool_ref[...] = proj(C_POOL, POOL_WIDTH)
    mq = _seg_rmsnorm(proj(C_MEMQ, MEM_WIDTH), ones128_ref[...], mqg_ref[...], MEM_HEAD_DIM)
    memq_ref[...] = mq.astype(BF16)
    for j in range(N_BRANCH):
        mg_ref[:, j * D_MODEL:(j + 1) * D_MODEL] = jax.nn.sigmoid(proj(C_MERGE + j * D_MODEL, D_MODEL))
    ng_ref[...] = jax.nn.sigmoid(proj(C_NSAG, NSAG_PAD))


def _inproj(x, lw, consts):
    n = x.shape[0]
    tm = min(TM_PROJ, n)
    row = lambda w: pl.BlockSpec((tm, w), lambda i: (i, 0))
    widths = (NSA_WIDTH, 4 * KV_WIDTH, 2 * KV_WIDTH, 4 * KV_WIDTH, POOL_WIDTH, MEM_WIDTH, N_BRANCH * D_MODEL, NSAG_PAD)
    dtypes = (BF16, F32, F32, BF16, F32, BF16, F32, F32)
    return pl.pallas_call(
        _inproj_kernel,
        grid=(n // tm,),
        in_specs=[row(D_MODEL), _const_spec((1, D_MODEL)), _const_spec((D_MODEL, PROJ_PAD)),
                  _const_spec((1, NSA_WIDTH)), _const_spec((1, KV_WIDTH)), _const_spec((1, KV_WIDTH)),
                  _const_spec((1, MEM_WIDTH)), _const_spec((NSA_WIDTH, NSA_WIDTH)), _const_spec((MEM_WIDTH, MEM_WIDTH))],
        out_specs=[row(w) for w in widths],
        out_shape=[jax.ShapeDtypeStruct((n, w), d) for w, d in zip(widths, dtypes)],
        compiler_params=_params("parallel"),
        name="inproj",
    )(x, lw["attn_g"], lw["w_in"], lw["q_g"], lw["ks_g"], lw["kw_g"], lw["mq_g"], consts["ones64"], consts["ones128"])


def _memkv_kernel(m_ref, g_ref, w_ref, kg_ref, ones128_ref, kv_ref, kvbf_ref):
    hb = _rms_rows(m_ref[...], g_ref[...]).astype(BF16)
    k = _seg_rmsnorm(_dot(hb, w_ref[:, 0:MEM_WIDTH]), ones128_ref[...], kg_ref[...], MEM_HEAD_DIM)
    v = _dot(hb, w_ref[:, MEM_WIDTH:2 * MEM_WIDTH])
    kv_ref[:, 0:MEM_WIDTH] = k
    kv_ref[:, MEM_WIDTH:2 * MEM_WIDTH] = v
    kvbf_ref[:, 0:MEM_WIDTH] = k.astype(BF16)
    kvbf_ref[:, MEM_WIDTH:2 * MEM_WIDTH] = v.astype(BF16)


def _memkv(mem, lw, consts):
    n = mem.shape[0]
    tm = min(TM_PROJ, n)
    row = lambda w: pl.BlockSpec((tm, w), lambda i: (i, 0))
    return pl.pallas_call(
        _memkv_kernel,
        grid=(n // tm,),
        in_specs=[row(D_MODEL), _const_spec((1, D_MODEL)), _const_spec((D_MODEL, 2 * MEM_WIDTH)),
                  _const_spec((1, MEM_WIDTH)), _const_spec((MEM_WIDTH, MEM_WIDTH))],
        out_specs=[row(2 * MEM_WIDTH), row(2 * MEM_WIDTH)],
        out_shape=[jax.ShapeDtypeStruct((n, 2 * MEM_WIDTH), F32), jax.ShapeDtypeStruct((n, 2 * MEM_WIDTH), BF16)],
        compiler_params=_params("parallel"),
        name="memkv",
    )(mem, lw["mem_g"], lw["w_mem_kv"], lw["mk_g"], consts["ones128"])


def _compress_prompt_kernel(kv_ref, pe_ref, wk_ref, wv_ref, kcg_ref, ones64_ref, out_ref):
    t = kv_ref.shape[0]
    nc = t // CMP_BLOCK
    pooled = jnp.sum(kv_ref[...].reshape(nc, CMP_BLOCK, 2 * KV_WIDTH), axis=1) * (1.0 / CMP_BLOCK)
    pooled = pooled + jnp.sum(pe_ref[...], axis=0, keepdims=True) * (1.0 / CMP_BLOCK)
    k = _dot_f32(pooled[:, 0:KV_WIDTH], wk_ref[...])
    v = _dot_f32(pooled[:, KV_WIDTH:2 * KV_WIDTH], wv_ref[...])
    out_ref[:, 0:KV_WIDTH] = _seg_rmsnorm(k, ones64_ref[...], kcg_ref[...], HEAD_DIM)
    out_ref[:, KV_WIDTH:2 * KV_WIDTH] = v


def _compress_prompt(nsakv, lw, consts):
    b, t, _ = nsakv.shape
    nc = t // CMP_BLOCK
    return pl.pallas_call(
        _compress_prompt_kernel,
        grid=(b,),
        in_specs=[pl.BlockSpec((None, t, 2 * KV_WIDTH), lambda i: (i, 0, 0)),
                  _const_spec((CMP_BLOCK, 2 * KV_WIDTH)), _const_spec((KV_WIDTH, KV_WIDTH)),
                  _const_spec((KV_WIDTH, KV_WIDTH)), _const_spec((1, KV_WIDTH)), _const_spec((KV_WIDTH, KV_WIDTH))],
        out_specs=pl.BlockSpec((None, nc, 2 * KV_WIDTH), lambda i: (i, 0, 0)),
        out_shape=jax.ShapeDtypeStruct((b, nc, 2 * KV_WIDTH), F32),
        compiler_params=_params("parallel"),
        name="compress_prompt",
    )(nsakv, lw["cmp_pe"], lw["cmp_wk"], lw["cmp_wv"], lw["kc_g"], consts["ones64_kv"])


def _masked_softmax_rows(s, mask):
    maskf = jnp.where(mask, 1.0, 0.0)
    s = jnp.where(mask[None], s, NEG)
    m = jnp.max(s, axis=-1, keepdims=True)
    p = jnp.exp(s - m) * maskf[None]
    den = jnp.maximum(jnp.sum(p, axis=-1, keepdims=True), 1e-30)
    return p * (1.0 / den)


def _stack_heads(q, k):
    return jnp.concatenate([q[:, (k * GROUP + g) * HEAD_DIM:(k * GROUP + g + 1) * HEAD_DIM] for g in range(GROUP)], axis=0)


def _unstack_heads(o, rows):
    return jnp.concatenate([o[g * rows:(g + 1) * rows] for g in range(GROUP)], axis=1)


def _topk_select_t(score_t, blk, n_sel):
    ns = score_t.shape[0]
    cnt = jnp.zeros(score_t.shape, F32)
    for j in range(ns):
        r = score_t[j:j + 1, :]
        ge = jnp.where(r >= score_t, 1.0, 0.0)
        gt = jnp.where(r > score_t, 1.0, 0.0)
        cnt = cnt + jnp.where(blk > j, ge, gt)
    return jnp.where(cnt < n_sel, 1.0, 0.0)


def _prompt_attn_kernel(q_ref, kcvc_ref, kv_ref, poolc_ref, poolp_ref, memq_ref, mkv_ref, ng_ref,
                        e_ref, pairt_ref, gexp_ref, wpool_ref, pscale_ref,
                        onsa_ref, pooly_ref, memo_ref,
                        m_scr, l_scr, acc_scr, pool_scr):
    tq = q_ref.shape[0]
    nc = kcvc_ref.shape[0]
    ns = pairt_ref.shape[0]
    kc = e_ref.shape[2]
    n_sel = min(TOP_K, ns)
    qi = pl.program_id(1)
    t0 = qi * tq
    q = q_ref[...]

    eye_tq = jnp.where(_iota((tq, tq), 0) == _iota((tq, tq), 1), 1.0, 0.0).astype(BF16)
    blk_t = _iota((ns, tq), 0)
    tpos_t = t0 + _iota((ns, tq), 1)
    cur_t = tpos_t // SEL_BLOCK
    valid_t = blk_t * SEL_BLOCK <= tpos_t
    forced_t = (blk_t == 0) | (blk_t == cur_t) | (blk_t == cur_t - 1)
    cmask = (_iota((tq, nc), 1) * CMP_BLOCK + (CMP_BLOCK - 1)) <= (t0 + _iota((tq, nc), 0))
    n_chunks = (t0 + tq - 1) // kc + 1

    o_cmp, o_sel, o_win = [], [], []
    for k in range(KV_HEADS):
        qk = _stack_heads(q, k)
        kc_k = kcvc_ref[:, k * HEAD_DIM:(k + 1) * HEAD_DIM].astype(BF16)
        vc_k = kcvc_ref[:, KV_WIDTH + k * HEAD_DIM:KV_WIDTH + (k + 1) * HEAD_DIM].astype(BF16)
        p = _masked_softmax_rows(_dot_nt(qk, kc_k).reshape(GROUP, tq, nc), cmask)
        o_cmp.append(_unstack_heads(_dot(p.reshape(GROUP * tq, nc).astype(BF16), vc_k), tq))
        imp_t = (p[0] + p[1] + p[2] + p[3]).T
        imp2_t = _dot_split_rhs(pairt_ref[...], imp_t, 3)
        score_t = jnp.where(valid_t, jnp.where(forced_t, BIG, imp2_t), -BIG)
        sel_t = _topk_select_t(score_t, blk_t, n_sel)
        sel = _dot_nt(eye_tq, sel_t.astype(BF16)).astype(BF16)
        m_scr[...] = jnp.full(m_scr.shape, NEG, F32)
        l_scr[...] = jnp.zeros(l_scr.shape, F32)
        acc_scr[...] = jnp.zeros(acc_scr.shape, F32)

        def sel_chunk(c, carry):
            start = pl.multiple_of(c * kc, kc)
            k_c = kv_ref[pl.ds(start, kc), k * HEAD_DIM:(k + 1) * HEAD_DIM]
            v_c = kv_ref[pl.ds(start, kc), KV_WIDTH + k * HEAD_DIM:KV_WIDTH + (k + 1) * HEAD_DIM]
            s = _dot_nt(qk, k_c).reshape(GROUP, tq, kc)
            selx = _dot(sel, e_ref[c])
            allowed = (selx > 0.5) & ((start + _iota((tq, kc), 1)) <= (t0 + _iota((tq, kc), 0)))
            s = (s + jnp.where(allowed, 0.0, NEG)[None]).reshape(GROUP * tq, kc)
            m_old = m_scr[...]
            m_new = jnp.maximum(m_old, jnp.max(s, axis=-1, keepdims=True))
            alpha = jnp.exp(m_old - m_new)
            pc = jnp.exp(s - m_new)
            l_scr[...] = alpha * l_scr[...] + jnp.sum(pc, axis=-1, keepdims=True)
            acc_scr[...] = alpha * acc_scr[...] + _dot(pc.astype(BF16), v_c)
            m_scr[...] = m_new
            return carry

        lax.fori_loop(0, n_chunks, sel_chunk, 0)
        o_sel.append(_unstack_heads(acc_scr[...] * (1.0 / l_scr[...]), tq))
        nwb = WINDOW // tq + 1
        s_blocks, v_blocks, masks = [], [], []
        for i in range(nwb):
            kb = qi - (nwb - 1) + i
            start = pl.multiple_of(jnp.maximum(kb, 0) * tq, tq)
            inside = kb >= 0
            k_b = kv_ref[pl.ds(start, tq), 2 * KV_WIDTH + k * HEAD_DIM:2 * KV_WIDTH + (k + 1) * HEAD_DIM]
            v_b = kv_ref[pl.ds(start, tq), 3 * KV_WIDTH + k * HEAD_DIM:3 * KV_WIDTH + (k + 1) * HEAD_DIM]
            s_blocks.append(jnp.where(inside, _dot_nt(qk, k_b), 0.0))
            v_blocks.append(jnp.where(inside, v_b, jnp.zeros_like(v_b)))
            kpos = kb * tq + _iota((tq, tq), 1)
            tpos = t0 + _iota((tq, tq), 0)
            masks.append((kpos <= tpos) & (kpos > tpos - WINDOW))
        s_w = jnp.concatenate(s_blocks, axis=1).reshape(GROUP, tq, nwb * tq)
        p_w = _masked_softmax_rows(s_w, jnp.concatenate(masks, axis=1)).reshape(GROUP * tq, nwb * tq).astype(BF16)
        ow = _dot(p_w[:, 0:tq], v_blocks[0])
        for i in range(1, nwb):
            ow = ow + _dot(p_w[:, i * tq:(i + 1) * tq], v_blocks[i])
        o_win.append(_unstack_heads(ow, tq))

    gates = _dot_split_lhs(ng_ref[...], gexp_ref[...], 2)
    o_nsa = (gates[:, 0:NSA_WIDTH] * jnp.concatenate(o_cmp, axis=1)
             + gates[:, NSA_WIDTH:2 * NSA_WIDTH] * jnp.concatenate(o_sel, axis=1)
             + gates[:, 2 * NSA_WIDTH:3 * NSA_WIDTH] * jnp.concatenate(o_win, axis=1))
    onsa_ref[...] = o_nsa.astype(onsa_ref.dtype)

    mq = memq_ref[...]
    outs = []
    for h in range(MEM_HEADS):
        lo = h * MEM_HEAD_DIM
        s = _dot_nt(mq[:, lo:lo + MEM_HEAD_DIM], mkv_ref[:, lo:lo + MEM_HEAD_DIM]) * MEM_HEAD_DIM ** -0.5
        pm = jnp.exp(s - jnp.max(s, axis=-1, keepdims=True))
        pm = pm * (1.0 / jnp.sum(pm, axis=-1, keepdims=True))
        outs.append(_dot(pm.astype(BF16), mkv_ref[:, MEM_WIDTH + lo:MEM_WIDTH + lo + MEM_HEAD_DIM]))
    memo_ref[...] = jnp.concatenate(outs, axis=1).astype(memo_ref.dtype)

    cur = poolc_ref[...]
    pool_scr[0:16, :] = jnp.where(qi > 0, poolp_ref[...], 0.0)
    pool_scr[16:16 + tq, :] = cur
    tpos1 = t0 + _iota((tq, POOL_GROUP_WIDTH), 0) + 1
    ys = []
    for gi, win in enumerate(POOL_WINDOWS):
        lo = gi * POOL_GROUP_WIDTH
        acc = pool_scr[16:16 + tq, lo:lo + POOL_GROUP_WIDTH]
        for sft in range(1, win):
            acc = acc + pool_scr[16 - sft:16 - sft + tq, lo:lo + POOL_GROUP_WIDTH]
        cnt = jnp.minimum(tpos1, win).astype(F32)
        d = acc / cnt - cur[:, lo:lo + POOL_GROUP_WIDTH]
        ys.append(_dot(d.astype(BF16), wpool_ref[gi]))
    pooly_ref[...] = (jnp.concatenate(ys, axis=1) * pscale_ref[...]).astype(pooly_ref.dtype)


def _prompt_attn(q, kcvc, kvbf, pool_u, memq, mkvbf, ng, lw, consts):
    b, t, _ = q.shape
    nc = t // CMP_BLOCK
    ns = t // SEL_BLOCK
    tq = TQ
    kc = min(KC_SEL, t)
    m = mkvbf.shape[1]
    tok = lambda w: pl.BlockSpec((None, tq, w), lambda i, j: (i, j, 0))
    per_b = lambda r, w: pl.BlockSpec((None, r, w), lambda i, j: (i, 0, 0))
    prev_rows = pl.BlockSpec((None, 16, POOL_WIDTH), lambda i, j: (i, jnp.maximum(j * (tq // 16) - 1, 0), 0))
    out = jax.ShapeDtypeStruct((b, t, NSA_WIDTH), BF16)
    return pl.pallas_call(
        _prompt_attn_kernel,
        grid=(b, t // tq),
        in_specs=[tok(NSA_WIDTH), per_b(nc, 2 * KV_WIDTH), per_b(t, 4 * KV_WIDTH), tok(POOL_WIDTH), prev_rows,
                  tok(MEM_WIDTH), per_b(m, 2 * MEM_WIDTH), tok(NSAG_PAD),
                  _const_spec((t // kc, ns, kc)), _const_spec((ns, nc)), _const_spec((NSAG_PAD, 3 * NSA_WIDTH)),
                  _const_spec((POOL_GROUPS, POOL_GROUP_WIDTH, POOL_GROUP_WIDTH)), _const_spec((1, POOL_WIDTH))],
        out_specs=[tok(NSA_WIDTH), tok(POOL_WIDTH), tok(MEM_WIDTH)],
        out_shape=[out, out, out],
        scratch_shapes=[pltpu.VMEM((GROUP * tq, 1), F32), pltpu.VMEM((GROUP * tq, 1), F32),
                        pltpu.VMEM((GROUP * tq, HEAD_DIM), F32), pltpu.VMEM((16 + tq, POOL_WIDTH), F32)],
        compiler_params=_params("parallel", "arbitrary"),
        name="prompt_attn",
    )(q, kcvc, kvbf, pool_u, pool_u, memq, mkvbf, ng,
      consts["sel_expand"], consts["pair_t"], consts["gate_expand"], lw["w_pool"], lw["pool_scale"])


def _merge_ffn_kernel(x_ref, onsa_ref, pooly_ref, memo_ref, mg_ref, wn_ref, wp_ref, wm_ref, wo_ref, fg_ref,
                      wgu_ref, wd_ref, y_ref):
    h = (mg_ref[:, 0:D_MODEL] * _dot(onsa_ref[...].astype(BF16), wn_ref[...])
         + mg_ref[:, D_MODEL:2 * D_MODEL] * _dot(pooly_ref[...].astype(BF16), wp_ref[...])
         + mg_ref[:, 2 * D_MODEL:3 * D_MODEL] * _dot(memo_ref[...].astype(BF16), wm_ref[...]))
    x1 = x_ref[...] + _dot(h.astype(BF16), wo_ref[...])
    hn = _rms_rows(x1, fg_ref[...]).astype(BF16)
    acc = x1
    for j in range(D_FF // FF_CHUNK):
        lo = j * FF_CHUNK
        gate = _dot(hn, wgu_ref[:, lo:lo + FF_CHUNK])
        up = _dot(hn, wgu_ref[:, D_FF + lo:D_FF + lo + FF_CHUNK])
        act = gate * jax.nn.sigmoid(gate) * up
        acc = acc + _dot(act.astype(BF16), wd_ref[lo:lo + FF_CHUNK, :])
    y_ref[...] = acc


def _merge_ffn(x, onsa, pooly, memo, mg, lw):
    n = x.shape[0]
    tm = min(TM_PROJ, n)
    row = lambda w: pl.BlockSpec((tm, w), lambda i: (i, 0))
    return pl.pallas_call(
        _merge_ffn_kernel,
        grid=(n // tm,),
        in_specs=[row(D_MODEL), row(NSA_WIDTH), row(POOL_WIDTH), row(MEM_WIDTH), row(N_BRANCH * D_MODEL),
                  _const_spec((NSA_WIDTH, D_MODEL)), _const_spec((POOL_WIDTH, D_MODEL)), _const_spec((MEM_WIDTH, D_MODEL)),
                  _const_spec((D_MODEL, D_MODEL)), _const_spec((1, D_MODEL)),
                  _const_spec((D_MODEL, 2 * D_FF)), _const_spec((D_FF, D_MODEL))],
        out_specs=row(D_MODEL),
        out_shape=jax.ShapeDtypeStruct((n, D_MODEL), F32),
        compiler_params=_params("parallel"),
        name="merge_ffn",
    )(x, onsa, pooly, memo, mg, lw["w_up_nsa"], lw["w_up_pool"], lw["w_up_mem"], lw["w_out"], lw["ffn_g"],
      lw["w_gate_up"], lw["w_down"])


def _page_specs(npg, half, layer_of, batch_of, step_of):
    def spec(i):
        def index_map(*args):
            pt = args[-1]
            grid = args[:-1]
            return (layer_of(grid), pt[batch_of(grid), step_of(grid) * npg + i], half, 0)
        return pl.BlockSpec((None, None, 2 * KV_WIDTH, PAGE_SIZE), index_map)
    return [spec(i) for i in range(npg)]


def _compress_pages_kernel(pt_ref, *refs):
    del pt_ref
    npg = len(refs) - 7
    pages = refs[:npg]
    pe_ref, wkt_ref, wvt_ref, kcg_ref, ones64_ref, poolm_ref, out_ref = refs[npg:]
    x = jnp.concatenate([pg[...] for pg in pages], axis=1)
    pooled = _dot_split_lhs(x, poolm_ref[...], 2)
    pooled = pooled + jnp.sum(pe_ref[...], axis=1, keepdims=True) * (1.0 / CMP_BLOCK)
    k = _dot_f32(wkt_ref[...], pooled[0:KV_WIDTH])
    v = _dot_f32(wvt_ref[...], pooled[KV_WIDTH:2 * KV_WIDTH])
    ssq = _dot_split_rhs(ones64_ref[...], k * k, 2)
    out_ref[0:KV_WIDTH, :] = k * lax.rsqrt(ssq * (1.0 / HEAD_DIM) + EPS) * kcg_ref[...]
    out_ref[KV_WIDTH:2 * KV_WIDTH, :] = v


def _compress_pages(cache_t, page_table, w):
    depth = cache_t.shape[0]
    bd, n_pages = page_table.shape
    npg = min(PAGES_PER_STEP_CMP, n_pages)
    per_page = PAGE_SIZE // CMP_BLOCK
    per_step = npg * per_page
    ncs = n_pages * per_page
    lconst = lambda r, c: pl.BlockSpec((None, r, c), lambda l, b, s, pt: (l, 0, 0))
    const = lambda r, c: pl.BlockSpec((r, c), lambda l, b, s, pt: (0, 0))
    grid_spec = pltpu.PrefetchScalarGridSpec(
        num_scalar_prefetch=1,
        grid=(depth, bd, n_pages // npg),
        in_specs=_page_specs(npg, 0, lambda g: g[0], lambda g: g[1], lambda g: g[2])
        + [lconst(2 * KV_WIDTH, CMP_BLOCK), lconst(KV_WIDTH, KV_WIDTH), lconst(KV_WIDTH, KV_WIDTH), lconst(KV_WIDTH, 1),
           const(KV_WIDTH, KV_WIDTH), const(npg * PAGE_SIZE, per_step)],
        out_specs=pl.BlockSpec((None, None, 2 * KV_WIDTH, per_step), lambda l, b, s, pt: (l, b, 0, s)),
    )
    rows = jnp.arange(npg * PAGE_SIZE)[:, None] // CMP_BLOCK
    poolm = jnp.where(rows == jnp.arange(per_step)[None, :], 1.0 / CMP_BLOCK, 0.0).astype(BF16)
    return pl.pallas_call(
        _compress_pages_kernel,
        grid_spec=grid_spec,
        out_shape=jax.ShapeDtypeStruct((depth, bd, 2 * KV_WIDTH, ncs), F32),
        compiler_params=_params("parallel", "parallel", "arbitrary"),
        name="compress_pages",
    )(page_table, *([cache_t] * npg), w["cmp_pe_t"], w["cmp_wk_t"], w["cmp_wv_t"], w["kc_g_col"], w["ones64_kv"], poolm)


def _rows8(row):
    return jnp.broadcast_to(row, (V7X_SUBLANES, row.shape[1]))


def _sample_attn_kernel(pt_ref, *refs, past_len):
    del pt_ref
    npg = len(refs) - 24
    pages = refs[:npg]
    (q_ref, nsakv_ref, winkv_ref, ng_ref, poolu_ref, memq_ref, kcvc_ref, wincache_ref, poolstate_ref, memcache_ref,
     e_ref, pair_ref, fold_ref, unfold_ref, wpool_ref, pscale_ref,
     onsa_ref, pooly_ref, memo_ref,
     qbd_scr, bias_scr, m_scr, l_scr, acc_scr) = refs[npg:]
    step = pl.program_id(1)
    n_steps = pl.num_programs(1)
    ncs = kcvc_ref.shape[1]
    nsp = pair_ref.shape[1]
    keys_per_step = npg * PAGE_SIZE
    pos = past_len
    row8 = _iota((8, KV_WIDTH), 0)
    lane8 = _iota((8, KV_WIDTH), 1)
    head_lanes = (lane8 // HEAD_DIM) == (row8 // GROUP)

    @pl.when(step == 0)
    def _():
        q8 = jnp.where((_iota((8, NSA_WIDTH), 1) // HEAD_DIM) == _iota((8, NSA_WIDTH), 0),
                       _rows8(q_ref[...].astype(F32)), 0.0)
        qbd = _dot(q8.astype(BF16), fold_ref[...])
        qbd_scr[...] = qbd
        s = _dot(qbd.astype(BF16), kcvc_ref[0:KV_WIDTH, :].astype(BF16))
        cmask = (_iota((8, ncs), 1) * CMP_BLOCK + (CMP_BLOCK - 1)) <= pos
        p = _masked_softmax_rows(s[None], cmask)[0]
        ocmp = _dot_nt(p.astype(BF16), kcvc_ref[KV_WIDTH:2 * KV_WIDTH, :].astype(BF16))
        acc_scr[0] = jnp.where(head_lanes, ocmp, 0.0)
        eye = _iota((nsp, nsp), 0) == _iota((nsp, nsp), 1)
        ii = _iota((nsp, nsp), 1)
        jj = _iota((nsp, nsp), 0)
        cur = pos // SEL_BLOCK
        sel_rows = []
        for k in range(KV_HEADS):
            imp = jnp.sum(p[k * GROUP:(k + 1) * GROUP], axis=0, keepdims=True)
            imp2 = _dot_split_lhs(_rows8(imp), pair_ref[...], 3)[0:1]
            blk = _iota((1, nsp), 1)
            forced = (blk == 0) | (blk == cur) | (blk == cur - 1)
            valid = blk * SEL_BLOCK <= pos
            score = jnp.where(valid, jnp.where(forced, BIG, imp2), -BIG)
            score_i = jnp.broadcast_to(score, (nsp, nsp))
            score_j = jnp.sum(jnp.where(eye, score_i, 0.0), axis=1, keepdims=True)
            beats = jnp.where((score_j > score_i) | ((score_j == score_i) & (jj < ii)), 1.0, 0.0)
            cnt = jnp.sum(beats, axis=0, keepdims=True)
            cnt = cnt + jnp.where(score < BIG, 1.0, 0.0)
            sel_rows.append(jnp.where(cnt < TOP_K, 1.0, 0.0))
        sel8 = jnp.where(_iota((8, nsp), 0) < GROUP, _rows8(sel_rows[0]), _rows8(sel_rows[1])).astype(BF16)
        for st in range(bias_scr.shape[0]):
            selx = _dot(sel8, e_ref[:, st * keys_per_step:(st + 1) * keys_per_step])
            bias_scr[st] = jnp.where(selx > 0.5, 0.0, NEG)
        m_scr[...] = jnp.full(m_scr.shape, NEG, F32)
        l_scr[...] = jnp.zeros(l_scr.shape, F32)
        acc_scr[1] = jnp.zeros((8, KV_WIDTH), F32)

    qbd = qbd_scr[...]
    qbd_bf = qbd.astype(BF16)
    k_t = jnp.concatenate([pg[0:KV_WIDTH, :] for pg in pages], axis=1).astype(BF16)
    v_t = jnp.concatenate([pg[KV_WIDTH:2 * KV_WIDTH, :] for pg in pages], axis=1).astype(BF16)
    s_all = _dot(qbd_bf, k_t) + bias_scr[step]
    m_old = m_scr[...]
    m_new = jnp.maximum(m_old, jnp.max(s_all, axis=-1, keepdims=True))
    p_all = jnp.exp(s_all - m_new) * jnp.where(s_all > 0.5 * NEG, 1.0, 0.0)
    alpha = jnp.exp(m_old - m_new)
    l_scr[...] = alpha * l_scr[...] + jnp.sum(p_all, axis=-1, keepdims=True)
    acc_scr[1] = alpha * acc_scr[1] + _dot_nt(p_all.astype(BF16), v_t)
    m_scr[...] = m_new

    @pl.when(step == n_steps - 1)
    def _():
        ks_new = _rows8(nsakv_ref[:, 2 * KV_WIDTH:3 * KV_WIDTH])
        vs_new = _rows8(nsakv_ref[:, 3 * KV_WIDTH:4 * KV_WIDTH])
        s_new = jnp.sum(qbd * ks_new, axis=-1, keepdims=True)
        m_old = m_scr[...]
        m_fin = jnp.maximum(m_old, s_new)
        alpha = jnp.exp(m_old - m_fin)
        p_new = jnp.exp(s_new - m_fin)
        l_fin = alpha * l_scr[...] + p_new
        osel = (alpha * acc_scr[1] + p_new * vs_new) * (1.0 / l_fin)
        wb = wincache_ref.shape[1]
        kw_new = _rows8(winkv_ref[:, 0:KV_WIDTH])
        vw_new = _rows8(winkv_ref[:, KV_WIDTH:2 * KV_WIDTH])
        s_w = _dot(qbd_bf, wincache_ref[0:KV_WIDTH, :].astype(BF16))
        kpos = pos - wb + _iota((8, wb), 1)
        wmask = (kpos <= pos) & (kpos > pos - WINDOW)
        s_wn = jnp.sum(qbd * kw_new, axis=-1, keepdims=True)
        s_w = jnp.where(wmask, s_w, NEG)
        m_w = jnp.maximum(jnp.max(s_w, axis=-1, keepdims=True), s_wn)
        p_w = jnp.exp(s_w - m_w) * jnp.where(wmask, 1.0, 0.0)
        p_wn = jnp.exp(s_wn - m_w)
        l_w = jnp.sum(p_w, axis=-1, keepdims=True) + p_wn
        owin = (_dot_nt(p_w.astype(BF16), wincache_ref[KV_WIDTH:2 * KV_WIDTH, :].astype(BF16)) + p_wn * vw_new) * (1.0 / l_w)
        ng8 = _rows8(ng_ref[...])
        glane = _iota((8, NSAG_PAD), 1)
        grow = _iota((8, NSAG_PAD), 0)
        gate = lambda br: jnp.sum(jnp.where(glane == N_NSA_BRANCH * grow + br, ng8, 0.0), axis=-1, keepdims=True)
        o8 = jnp.where(head_lanes, gate(0) * acc_scr[0] + gate(1) * osel + gate(2) * owin, 0.0)
        o512 = _dot_split_lhs(o8, unfold_ref[...], 3)
        own = (_iota((8, NSA_WIDTH), 1) // HEAD_DIM) == _iota((8, NSA_WIDTH), 0)
        onsa_ref[...] = jnp.sum(jnp.where(own, o512, 0.0), axis=0, keepdims=True)
        mtok = memcache_ref.shape[0] // (2 * MEM_HEADS)
        outs = []
        for h in range(MEM_HEADS):
            k_h = memcache_ref[pl.ds(h, mtok, stride=2 * MEM_HEADS), :].astype(BF16)
            v_h = memcache_ref[pl.ds(MEM_HEADS + h, mtok, stride=2 * MEM_HEADS), :].astype(BF16)
            q_h = _rows8(memq_ref[:, h * MEM_HEAD_DIM:(h + 1) * MEM_HEAD_DIM].astype(F32)).astype(BF16)
            s_m = _dot_nt(q_h, k_h) * MEM_HEAD_DIM ** -0.5
            p_m = jnp.exp(s_m - jnp.max(s_m, axis=-1, keepdims=True))
            p_m = p_m * (1.0 / jnp.sum(p_m, axis=-1, keepdims=True))
            outs.append(_dot(p_m.astype(BF16), v_h)[0:1])
        memo_ref[...] = jnp.concatenate(outs, axis=1)
        u_new = poolu_ref[...]
        state = poolstate_ref[...]
        srow = _iota((POOL_BUF, POOL_GROUP_WIDTH), 0)
        ys = []
        for gi, win in enumerate(POOL_WINDOWS):
            lo = gi * POOL_GROUP_WIDTH
            u_g = u_new[:, lo:lo + POOL_GROUP_WIDTH]
            tail = jnp.sum(jnp.where(srow >= POOL_BUF - (win - 1), state[:, lo:lo + POOL_GROUP_WIDTH], 0.0),
                           axis=0, keepdims=True)
            d = (tail + u_g) / float(min(pos + 1, win)) - u_g
            ys.append(_dot(_rows8(d).astype(BF16), wpool_ref[gi])[0:1])
        pooly_ref[...] = jnp.concatenate(ys, axis=1) * pscale_ref[...]


def _sample_attn(l, q, nsakv, winkv, ng, pool_u, memq, kcvc_s, cache_t, wincache_t, poolstate, memcache, page_table,
                 lw, consts):
    bd, n_pages = page_table.shape
    npg = min(PAGES_PER_STEP_SEL, n_pages)
    n_steps = n_pages // npg
    past_len = n_pages * PAGE_SIZE
    ncs = past_len // CMP_BLOCK
    nsp = past_len // SEL_BLOCK
    wb = wincache_t.shape[3]
    mrows = memcache.shape[2]
    row = lambda w: pl.BlockSpec((None, 1, w), lambda b, s, pt: (b, 0, 0))
    const = lambda shape: pl.BlockSpec(shape, lambda b, s, pt: (0,) * len(shape))
    per_b = lambda r, c: pl.BlockSpec((None, None, r, c), lambda b, s, pt: (l, b, 0, 0))
    in_specs = (
        _page_specs(npg, 1, lambda g: l, lambda g: g[0], lambda g: g[1])
        + [row(NSA_WIDTH), row(4 * KV_WIDTH), row(2 * KV_WIDTH), row(NSAG_PAD), row(POOL_WIDTH), row(MEM_WIDTH),
           per_b(2 * KV_WIDTH, ncs), per_b(2 * KV_WIDTH, wb), per_b(POOL_BUF, POOL_WIDTH), per_b(mrows, MEM_HEAD_DIM),
           const((nsp, past_len)), const((ncs, nsp)), const((NSA_WIDTH, KV_WIDTH)), const((KV_WIDTH, NSA_WIDTH)),
           const((POOL_GROUPS, POOL_GROUP_WIDTH, POOL_GROUP_WIDTH)), const((1, POOL_WIDTH))])
    grid_spec = pltpu.PrefetchScalarGridSpec(
        num_scalar_prefetch=1,
        grid=(bd, n_steps),
        in_specs=in_specs,
        out_specs=[row(NSA_WIDTH), row(POOL_WIDTH), row(MEM_WIDTH)],
        scratch_shapes=[pltpu.VMEM((8, KV_WIDTH), F32), pltpu.VMEM((n_steps, 8, npg * PAGE_SIZE), F32),
                        pltpu.VMEM((8, 1), F32), pltpu.VMEM((8, 1), F32), pltpu.VMEM((2, 8, KV_WIDTH), F32)],
    )
    out = jax.ShapeDtypeStruct((bd, 1, NSA_WIDTH), F32)
    return pl.pallas_call(
        functools.partial(_sample_attn_kernel, past_len=past_len),
        grid_spec=grid_spec,
        out_shape=[out, out, out],
        compiler_params=_params("parallel", "arbitrary"),
        name="sample_attn",
    )(page_table, *([cache_t] * npg), q, nsakv, winkv, ng, pool_u, memq, kcvc_s, wincache_t, poolstate, memcache,
      consts["sel_expand_s"], consts["pair_s"], consts["fold"], consts["unfold"], lw["w_pool"], lw["pool_scale"])


def _block_diag_ones(n, seg):
    return (jnp.arange(n)[:, None] // seg == jnp.arange(n)[None, :] // seg).astype(BF16)


def _constants(t, past_len):
    nc, ns = t // CMP_BLOCK, t // SEL_BLOCK
    kc = min(KC_SEL, t)
    key_blk = jnp.arange(t) // SEL_BLOCK
    sel_expand = (jnp.arange(ns)[:, None] == key_blk[None, :]).astype(BF16)
    sel_expand = sel_expand.reshape(ns, t // kc, kc).transpose(1, 0, 2)
    ratio = SEL_BLOCK // CMP_BLOCK
    pair_t = (jnp.arange(ns)[:, None] == jnp.arange(nc)[None, :] // ratio).astype(BF16)
    gr = jnp.arange(NSAG_PAD)[:, None]
    gc = jnp.arange(3 * NSA_WIDTH)[None, :]
    gate_expand = (gr == ((gc % NSA_WIDTH) // HEAD_DIM) * N_NSA_BRANCH + gc // NSA_WIDTH).astype(BF16)
    ncs, nsp = past_len // CMP_BLOCK, past_len // SEL_BLOCK
    sel_expand_s = (jnp.arange(nsp)[:, None] == (jnp.arange(past_len) // SEL_BLOCK)[None, :]).astype(BF16)
    pair_s = (jnp.arange(ncs)[:, None] // ratio == jnp.arange(nsp)[None, :]).astype(BF16)
    c = jnp.arange(NSA_WIDTH)
    fold_col = (c // (GROUP * HEAD_DIM)) * HEAD_DIM + c % HEAD_DIM
    fold = (fold_col[:, None] == jnp.arange(KV_WIDTH)[None, :]).astype(BF16)
    return {
        "ones64": _block_diag_ones(NSA_WIDTH, HEAD_DIM), "ones128": _block_diag_ones(MEM_WIDTH, MEM_HEAD_DIM),
        "ones64_kv": _block_diag_ones(KV_WIDTH, HEAD_DIM),
        "sel_expand": sel_expand, "pair_t": pair_t, "gate_expand": gate_expand,
        "sel_expand_s": sel_expand_s, "pair_s": pair_s, "fold": fold, "unfold": fold.T,
    }


def _tile_row(g, reps):
    return jnp.tile(g, reps)[None, :].astype(F32)


def kernel(x_prompt, x_sample, mem_prompt, cache_nsa_kv, cache_win_kv, state_pool, cache_mem_kv, page_table,
           attn_norm_g, w_in, nsa_q_g, nsa_kc_g, nsa_ks_g, nsa_kw_g, cmp_pe_k, cmp_pe_v, cmp_wk, cmp_wv,
           w_pool, pool_scale, mem_norm_g, w_mem_kv, mem_q_g, mem_k_g, w_up_nsa, w_up_pool, w_up_mem, w_out,
           ffn_norm_g, w_gate_up, w_down):
    depth = w_in.shape[0]
    b, t, _ = x_prompt.shape
    bd = x_sample.shape[0]
    n_pages = page_table.shape[1]
    past_len = n_pages * PAGE_SIZE
    mtok = mem_prompt.shape[1]
    wb = cache_win_kv.shape[2]
    assert x_sample.shape[1] == 1 and t % KC_SEL == 0 and wb == WINDOW and past_len >= WINDOW
    consts = _constants(t, past_len)

    nsag_lo = NSA_WIDTH + 6 * KV_WIDTH
    nsag_hi = nsag_lo + N_NSA_BRANCH * N_HEADS
    w_in_r = jnp.concatenate(
        [w_in[:, :, :nsag_lo], w_in[:, :, nsag_hi:], w_in[:, :, nsag_lo:nsag_hi],
         jnp.zeros((depth, D_MODEL, NSAG_PAD - N_NSA_BRANCH * N_HEADS), w_in.dtype)], axis=2).astype(BF16)
    eye_kv = jnp.eye(KV_HEADS, dtype=F32)
    layers = []
    for l in range(depth):
        layers.append({
            "attn_g": attn_norm_g[l][None, :], "w_in": w_in_r[l],
            "q_g": _tile_row(nsa_q_g[l], N_HEADS), "ks_g": _tile_row(nsa_ks_g[l], KV_HEADS),
            "kw_g": _tile_row(nsa_kw_g[l], KV_HEADS), "mq_g": _tile_row(mem_q_g[l], MEM_HEADS),
            "kc_g": _tile_row(nsa_kc_g[l], KV_HEADS),
            "cmp_pe": jnp.concatenate([jnp.tile(cmp_pe_k[l], (1, KV_HEADS)), jnp.tile(cmp_pe_v[l], (1, KV_HEADS))], axis=1),
            "cmp_wk": jnp.kron(eye_kv, cmp_wk[l]), "cmp_wv": jnp.kron(eye_kv, cmp_wv[l]),
            "w_pool": w_pool[l].astype(BF16), "pool_scale": pool_scale[l][None, :],
            "mem_g": mem_norm_g[l][None, :], "w_mem_kv": w_mem_kv[l].astype(BF16), "mk_g": _tile_row(mem_k_g[l], MEM_HEADS),
            "w_up_nsa": w_up_nsa[l].astype(BF16), "w_up_pool": w_up_pool[l].astype(BF16),
            "w_up_mem": w_up_mem[l].astype(BF16), "w_out": w_out[l].astype(BF16),
            "ffn_g": ffn_norm_g[l][None, :], "w_gate_up": w_gate_up[l].astype(BF16), "w_down": w_down[l].astype(BF16),
        })
    stack = lambda key, f: jnp.stack([f(lw[key]) for lw in layers])
    cmp_w = {"cmp_pe_t": stack("cmp_pe", jnp.transpose), "cmp_wk_t": stack("cmp_wk", jnp.transpose),
             "cmp_wv_t": stack("cmp_wv", jnp.transpose), "kc_g_col": stack("kc_g", jnp.transpose),
             "ones64_kv": consts["ones64_kv"]}

    cache_t = cache_nsa_kv.transpose(0, 1, 3, 4, 5, 2).reshape(depth, cache_nsa_kv.shape[1], 4 * KV_WIDTH, PAGE_SIZE)
    wincache_t = cache_win_kv.transpose(0, 1, 3, 4, 5, 2).reshape(depth, bd, 2 * KV_WIDTH, wb)
    memcache = cache_mem_kv.reshape(depth, bd, cache_mem_kv.shape[2] * 2 * MEM_HEADS, MEM_HEAD_DIM)
    kcvc_s = _compress_pages(cache_t, page_table, cmp_w)

    xp = x_prompt.reshape(b * t, D_MODEL)
    xs = x_sample.reshape(bd, D_MODEL)
    mem = mem_prompt.reshape(b * mtok, D_MODEL)
    nsa_p, nsa_s, win_p, win_s, pool_p, pool_s, mem_p = [], [], [], [], [], [], []
    for l, lw in enumerate(layers):
        q, nsakv, winkv, kvbf, pool_u, memq, mg, ng = _inproj(xp, lw, consts)
        mkv, mkvbf = _memkv(mem, lw, consts)
        nsakv3 = nsakv.reshape(b, t, 4 * KV_WIDTH)
        kcvc = _compress_prompt(nsakv3, lw, consts)
        pool3 = pool_u.reshape(b, t, POOL_WIDTH)
        onsa, pooly, memo = _prompt_attn(
            q.reshape(b, t, NSA_WIDTH), kcvc, kvbf.reshape(b, t, 4 * KV_WIDTH), pool3, memq.reshape(b, t, MEM_WIDTH),
            mkvbf.reshape(b, mtok, 2 * MEM_WIDTH), ng.reshape(b, t, NSAG_PAD), lw, consts)
        xp = _merge_ffn(xp, onsa.reshape(b * t, NSA_WIDTH), pooly.reshape(b * t, POOL_WIDTH),
                        memo.reshape(b * t, MEM_WIDTH), mg, lw)
        nsa_p.append(nsakv3.reshape(b, t, 4, KV_HEADS, HEAD_DIM))
        win_p.append(winkv.reshape(b, t, 2, KV_HEADS, HEAD_DIM)[:, t - min(WINDOW, t):])
        pool_p.append(pool3[:, t - POOL_BUF:])
        mem_p.append(mkv.reshape(b, mtok, 2, MEM_HEADS, MEM_HEAD_DIM))
        q, nsakv, winkv, _, pool_u, memq, mg, ng = _inproj(xs, lw, consts)
        r3 = lambda a: a.reshape(bd, 1, a.shape[-1])
        onsa, pooly, memo = _sample_attn(l, r3(q), r3(nsakv), r3(winkv), r3(ng), r3(pool_u), r3(memq), kcvc_s, cache_t,
                                         wincache_t, state_pool, memcache, page_table, lw, consts)
        xs = _merge_ffn(xs, onsa.reshape(bd, NSA_WIDTH), pooly.reshape(bd, POOL_WIDTH), memo.reshape(bd, MEM_WIDTH), mg, lw)
        nsa_s.append(nsakv.reshape(bd, 1, 4, KV_HEADS, HEAD_DIM))
        win_s.append(jnp.concatenate([cache_win_kv[l][:, 1:], winkv.reshape(bd, 1, 2, KV_HEADS, HEAD_DIM)], axis=1))
        pool_s.append(jnp.concatenate([state_pool[l][:, 1:], pool_u.reshape(bd, 1, POOL_WIDTH)], axis=1))
    return (xp.reshape(b, t, D_MODEL), xs.reshape(bd, 1, D_MODEL), jnp.stack(nsa_p), jnp.stack(nsa_s),
            jnp.stack(win_p), jnp.stack(win_s), jnp.stack(pool_p), jnp.stack(pool_s), jnp.stack(mem_p))
```

```python
import functools

import jax
import jax.numpy as jnp
from jax import lax
from jax.experimental import pallas as pl
from jax.experimental.pallas import tpu as pltpu

F32 = jnp.float32
BF16 = jnp.bfloat16

D_MODEL = 1024
PAGE_SIZE = 128
N_HEADS = 8
HEAD_DIM = 64
KV_HEADS = 2
GROUP = N_HEADS // KV_HEADS
NSA_WIDTH = N_HEADS * HEAD_DIM
KV_WIDTH = KV_HEADS * HEAD_DIM
CMP_BLOCK = 32
SEL_BLOCK = 64
TOP_K = 16
WINDOW = 512
N_NSA_BRANCH = 3
POOL_WIDTH = 512
POOL_GROUPS = 4
POOL_GROUP_WIDTH = POOL_WIDTH // POOL_GROUPS
POOL_WINDOWS = (2, 4, 8, 16)
POOL_BUF = 15
MEM_HEADS = 4
MEM_HEAD_DIM = 128
MEM_WIDTH = MEM_HEADS * MEM_HEAD_DIM
N_BRANCH = 3
D_FF = ((8 * D_MODEL // 3 + 255) // 256) * 256
EPS = 1e-6
LOG2_E = 1.4426950408889634
NEG = -1e30
BIG = 1e9

V7X_LANES = 128
V7X_SUBLANES = 8
V7X_VMEM_LIMIT_BYTES = 56 * 1024 * 1024

C_Q = 0
C_NSAKV = C_Q + NSA_WIDTH
C_WINKV = C_NSAKV + 4 * KV_WIDTH
C_POOL = C_WINKV + 2 * KV_WIDTH
C_MEMQ = C_POOL + POOL_WIDTH
C_MERGE = C_MEMQ + MEM_WIDTH
C_NSAG = C_MERGE + N_BRANCH * D_MODEL
NSAG_PAD = V7X_LANES
PROJ_PAD = C_NSAG + NSAG_PAD

TM_PROJ = 256
TQ = 128
SEL_CHUNKS_PER_STEP = 2
FF_CHUNK = D_FF // 2
PAGES_PER_STEP_CMP = 32
PAGES_PER_STEP_SEL = 16


def _dot(a, b):
    return jnp.dot(a, b, preferred_element_type=F32)


def _dot_nt(a, b):
    return lax.dot_general(a, b, (((1,), (1,)), ((), ())), preferred_element_type=F32)


def _dot_f32(a, b):
    return jnp.dot(a, b, preferred_element_type=F32, precision=lax.Precision.HIGHEST)


def _split_bf16(a, n):
    parts, r = [], a
    for _ in range(n):
        p = r.astype(BF16)
        parts.append(p)
        r = r - p.astype(F32)
    return parts


def _dot_split_lhs(a, b, n):
    return sum(_dot(p, b) for p in _split_bf16(a, n))


def _dot_split_rhs(a, b, n):
    return sum(_dot(a, p) for p in _split_bf16(b, n))


def _iota(shape, dim):
    return lax.broadcasted_iota(jnp.int32, shape, dim)


def _rms_rows(x, g):
    return x * lax.rsqrt(jnp.mean(x * x, axis=-1, keepdims=True) + EPS) * g


def _seg_rmsnorm(v, ones_bd, gain, seg):
    ssq = _dot_split_lhs(v * v, ones_bd, 2)
    return v * lax.rsqrt(ssq * (1.0 / seg) + EPS) * gain


def _const_spec(shape):
    return pl.BlockSpec(shape, lambda *_: (0,) * len(shape), pipeline_mode=pl.Buffered(1))


def _params(*sem):
    return pltpu.CompilerParams(dimension_semantics=sem, vmem_limit_bytes=V7X_VMEM_LIMIT_BYTES)


def _inproj_kernel(x_ref, g_ref, w_ref, wvt_ref, qg_ref, ksg_ref, kwg_ref, mqg_ref, ones64_ref, ones128_ref,
                   q_ref, nsakv_ref, winkv_ref, kext_ref, kwbf_ref, vt_ref, pool_ref, memq_ref, mg_ref, ng_ref,
                   *, seq_len):
    tm = x_ref.shape[0]
    hb = _rms_rows(x_ref[...], g_ref[...]).astype(BF16)

    def proj(lo, width):
        return _dot(hb, w_ref[:, lo:lo + width])

    ones_kv = ones64_ref[0:KV_WIDTH, 0:KV_WIDTH]
    q = _seg_rmsnorm(proj(C_Q, NSA_WIDTH), ones64_ref[...], qg_ref[...], HEAD_DIM)
    q_ref[...] = (q * (HEAD_DIM ** -0.5 * LOG2_E)).astype(BF16)
    kcvc = proj(C_NSAKV, 2 * KV_WIDTH)
    ks = _seg_rmsnorm(proj(C_NSAKV + 2 * KV_WIDTH, KV_WIDTH), ones_kv, ksg_ref[...], HEAD_DIM)
    vs = proj(C_NSAKV + 3 * KV_WIDTH, KV_WIDTH)
    kw = _seg_rmsnorm(proj(C_WINKV, KV_WIDTH), ones_kv, kwg_ref[...], HEAD_DIM)
    vw = proj(C_WINKV + KV_WIDTH, KV_WIDTH)
    nsakv_ref[:, 0:2 * KV_WIDTH] = kcvc
    nsakv_ref[:, 2 * KV_WIDTH:3 * KV_WIDTH] = ks
    nsakv_ref[:, 3 * KV_WIDTH:4 * KV_WIDTH] = vs
    winkv_ref[:, 0:KV_WIDTH] = kw
    winkv_ref[:, KV_WIDTH:2 * KV_WIDTH] = vw
    t_seq = (pl.program_id(0) * tm + _iota((tm, HEAD_DIM), 0)) % seq_len
    onehot = jnp.where(t_seq // SEL_BLOCK == _iota((tm, HEAD_DIM), 1), 1.0, 0.0).astype(BF16)
    for k in range(KV_HEADS):
        kext_ref[:, 2 * k * HEAD_DIM:(2 * k + 1) * HEAD_DIM] = ks[:, k * HEAD_DIM:(k + 1) * HEAD_DIM].astype(BF16)
        kext_ref[:, (2 * k + 1) * HEAD_DIM:(2 * k + 2) * HEAD_DIM] = onehot
    kwbf_ref[...] = kw.astype(BF16)
    vt_ref[...] = _dot_nt(wvt_ref[...], hb).astype(BF16)
    pool_ref[...] = proj(C_POOL, POOL_WIDTH)
    mq = _seg_rmsnorm(proj(C_MEMQ, MEM_WIDTH), ones128_ref[...], mqg_ref[...], MEM_HEAD_DIM)
    memq_ref[...] = mq.astype(BF16)
    for j in range(N_BRANCH):
        mg_ref[:, j * D_MODEL:(j + 1) * D_MODEL] = jax.nn.sigmoid(proj(C_MERGE + j * D_MODEL, D_MODEL))
    ng_ref[...] = jax.nn.sigmoid(proj(C_NSAG, NSAG_PAD))


def _inproj(x, lw, consts, seq_len):
    n = x.shape[0]
    tm = min(TM_PROJ, n)
    row = lambda w: pl.BlockSpec((tm, w), lambda i: (i, 0))
    widths = (NSA_WIDTH, 4 * KV_WIDTH, 2 * KV_WIDTH, 2 * KV_WIDTH, KV_WIDTH, None, POOL_WIDTH, MEM_WIDTH,
              N_BRANCH * D_MODEL, NSAG_PAD)
    dtypes = (BF16, F32, F32, BF16, BF16, BF16, F32, BF16, F32, F32)
    vt_spec = pl.BlockSpec((None, 2 * KV_WIDTH, tm), lambda i: (i, 0, 0))
    vt_shape = jax.ShapeDtypeStruct((n // tm, 2 * KV_WIDTH, tm), BF16)
    return pl.pallas_call(
        functools.partial(_inproj_kernel, seq_len=seq_len),
        grid=(n // tm,),
        in_specs=[row(D_MODEL), _const_spec((1, D_MODEL)), _const_spec((D_MODEL, PROJ_PAD)),
                  _const_spec((2 * KV_WIDTH, D_MODEL)),
                  _const_spec((1, NSA_WIDTH)), _const_spec((1, KV_WIDTH)), _const_spec((1, KV_WIDTH)),
                  _const_spec((1, MEM_WIDTH)), _const_spec((NSA_WIDTH, NSA_WIDTH)), _const_spec((MEM_WIDTH, MEM_WIDTH))],
        out_specs=[vt_spec if w is None else row(w) for w in widths],
        out_shape=[vt_shape if w is None else jax.ShapeDtypeStruct((n, w), d) for w, d in zip(widths, dtypes)],
        compiler_params=_params("parallel"),
        name="inproj",
    )(x, lw["attn_g"], lw["w_in"], lw["w_v_t"], lw["q_g"], lw["ks_g"], lw["kw_g"], lw["mq_g"],
      consts["ones64"], consts["ones128"])


def _memkv_kernel(m_ref, g_ref, w_ref, wvt_ref, kg_ref, ones128_ref, kv_ref, kbf_ref, vt_ref):
    hb = _rms_rows(m_ref[...], g_ref[...]).astype(BF16)
    k = _seg_rmsnorm(_dot(hb, w_ref[:, 0:MEM_WIDTH]), ones128_ref[...], kg_ref[...], MEM_HEAD_DIM)
    kv_ref[:, 0:MEM_WIDTH] = k
    kv_ref[:, MEM_WIDTH:2 * MEM_WIDTH] = _dot(hb, w_ref[:, MEM_WIDTH:2 * MEM_WIDTH])
    kbf_ref[...] = k.astype(BF16)
    vt_ref[...] = _dot_nt(wvt_ref[...], hb).astype(BF16)


def _memkv(mem, lw, consts):
    b, mtok, _ = mem.shape
    blk = lambda r, c: pl.BlockSpec((None, r, c), lambda i: (i, 0, 0))
    return pl.pallas_call(
        _memkv_kernel,
        grid=(b,),
        in_specs=[blk(mtok, D_MODEL), _const_spec((1, D_MODEL)), _const_spec((D_MODEL, 2 * MEM_WIDTH)),
                  _const_spec((MEM_WIDTH, D_MODEL)), _const_spec((1, MEM_WIDTH)), _const_spec((MEM_WIDTH, MEM_WIDTH))],
        out_specs=[blk(mtok, 2 * MEM_WIDTH), blk(mtok, MEM_WIDTH), blk(MEM_WIDTH, mtok)],
        out_shape=[jax.ShapeDtypeStruct((b, mtok, 2 * MEM_WIDTH), F32), jax.ShapeDtypeStruct((b, mtok, MEM_WIDTH), BF16),
                   jax.ShapeDtypeStruct((b, MEM_WIDTH, mtok), BF16)],
        compiler_params=_params("parallel"),
        name="memkv",
    )(mem, lw["mem_g"], lw["w_mem_kv"], lw["w_mem_v_t"], lw["mk_g"], consts["ones128"])


def _compress_prompt_kernel(kv_ref, pe_ref, wk_ref, wv_ref, kcg_ref, ones64_ref, kc_ref, vct_ref):
    t = kv_ref.shape[0]
    nc = t // CMP_BLOCK
    pooled = jnp.sum(kv_ref[...].reshape(nc, CMP_BLOCK, 2 * KV_WIDTH), axis=1) * (1.0 / CMP_BLOCK)
    pooled = pooled + jnp.sum(pe_ref[...], axis=0, keepdims=True) * (1.0 / CMP_BLOCK)
    k = _dot_f32(pooled[:, 0:KV_WIDTH], wk_ref[...])
    v = _dot_f32(pooled[:, KV_WIDTH:2 * KV_WIDTH], wv_ref[...])
    kc_ref[...] = _seg_rmsnorm(k, ones64_ref[...], kcg_ref[...], HEAD_DIM).astype(BF16)
    vct_ref[...] = v.T.astype(BF16)


def _compress_prompt(nsakv, lw, consts):
    b, t, _ = nsakv.shape
    nc = t // CMP_BLOCK
    return pl.pallas_call(
        _compress_prompt_kernel,
        grid=(b,),
        in_specs=[pl.BlockSpec((None, t, 2 * KV_WIDTH), lambda i: (i, 0, 0)),
                  _const_spec((CMP_BLOCK, 2 * KV_WIDTH)), _const_spec((KV_WIDTH, KV_WIDTH)),
                  _const_spec((KV_WIDTH, KV_WIDTH)), _const_spec((1, KV_WIDTH)), _const_spec((KV_WIDTH, KV_WIDTH))],
        out_specs=[pl.BlockSpec((None, nc, KV_WIDTH), lambda i: (i, 0, 0)),
                   pl.BlockSpec((None, KV_WIDTH, nc), lambda i: (i, 0, 0))],
        out_shape=[jax.ShapeDtypeStruct((b, nc, KV_WIDTH), BF16), jax.ShapeDtypeStruct((b, KV_WIDTH, nc), BF16)],
        compiler_params=_params("parallel"),
        name="compress_prompt",
    )(nsakv, lw["cmp_pe"], lw["cmp_wk"], lw["cmp_wv"], lw["kc_g"], consts["ones64_kv"])


def _masked_softmax_rows(s, mask):
    maskf = jnp.where(mask, 1.0, 0.0)
    s = jnp.where(mask[None], s, NEG)
    m = jnp.max(s, axis=-1, keepdims=True)
    p = jnp.exp2(s - m) * maskf[None]
    den = jnp.maximum(jnp.sum(p, axis=-1, keepdims=True), 1e-30)
    return p * (1.0 / den)


def _stack_heads(q, k):
    return jnp.concatenate([q[:, (k * GROUP + g) * HEAD_DIM:(k * GROUP + g + 1) * HEAD_DIM] for g in range(GROUP)], axis=0)


def _topk_select_t(score_t, n_sel):
    ns, tq = score_t.shape
    sub = V7X_SUBLANES
    tiles = [score_t[v * sub:(v + 1) * sub] for v in range(ns // sub)]
    row = _iota((sub, tq), 0)
    cnts = [jnp.zeros((sub, tq), F32) for _ in tiles]
    for j in range(ns):
        r = jnp.broadcast_to(score_t[j:j + 1, :], (sub, tq))
        for v, tile in enumerate(tiles):
            if v * sub > j:
                wins = r >= tile
            elif v * sub + sub - 1 < j:
                wins = r > tile
            else:
                wins = jnp.where(row + v * sub > j, jnp.where(r >= tile, 1.0, 0.0), jnp.where(r > tile, 1.0, 0.0)) > 0.5
            cnts[v] = cnts[v] + jnp.where(wins, 1.0, 0.0)
    return jnp.where(jnp.concatenate(cnts, axis=0) < n_sel, 1.0, 0.0)


def _online_softmax_step(state, s_t, v_t):
    m, l, acc = state
    m_new = jnp.maximum(m, jnp.max(s_t, axis=0, keepdims=True))
    alpha = jnp.exp2(m - m_new)
    p = jnp.exp2(s_t - m_new)
    return m_new, alpha * l + jnp.sum(p, axis=0, keepdims=True), alpha * acc + _dot(v_t, p.astype(BF16))


def _prompt_attn_kernel(q_ref, kc_ref, vct_ref, kext_ref, kw_ref, vt_ref, poolc_ref, poolp_ref, memq_ref, mk_ref, mvt_ref,
                        ng_ref, pairt_ref, wpool_ref, pscale_ref,
                        onsa_ref, pooly_ref, memo_ref, pool_scr, part_scr):
    tq = q_ref.shape[0]
    nc = kc_ref.shape[0]
    ns = pairt_ref.shape[0]
    kc = vt_ref.shape[2]
    rows = GROUP * tq
    n_sel = min(TOP_K, ns)
    qi = pl.program_id(1)
    t0 = qi * tq
    q = q_ref[...]

    eye_tq = jnp.where(_iota((tq, tq), 0) == _iota((tq, tq), 1), 1.0, 0.0).astype(BF16)
    blk_t = _iota((ns, tq), 0)
    tpos_t = t0 + _iota((ns, tq), 1)
    cur_t = tpos_t // SEL_BLOCK
    valid_t = blk_t * SEL_BLOCK <= tpos_t
    forced_t = (blk_t == 0) | (blk_t == cur_t) | (blk_t == cur_t - 1)
    tok_of_row = lambda shape: t0 + (_iota(shape, 1) & (tq - 1))
    cmask = (_iota((nc, rows), 0) * CMP_BLOCK + (CMP_BLOCK - 1)) <= tok_of_row((nc, rows))
    cmaskf = jnp.where(cmask, 1.0, 0.0)
    ng_t = ng_ref[...].T
    nwb = WINDOW // tq + 1
    rel = _iota((tq, rows), 0) - (_iota((tq, rows), 1) & (tq - 1))
    win_first = jnp.where(rel > 0, 0.0, NEG)
    win_diag = jnp.where(rel <= 0, 0.0, NEG)
    init = (jnp.full((1, rows), NEG, F32), jnp.zeros((1, rows), F32), jnp.zeros((HEAD_DIM, rows), F32))

    def gate_row(k, br):
        idx = [N_NSA_BRANCH * (k * GROUP + g) + br for g in range(GROUP)]
        return jnp.concatenate([ng_t[i:i + 1, :] for i in idx], axis=1)

    q_exts = []
    for k in range(KV_HEADS):
        qk = _stack_heads(q, k)
        s_c = jnp.where(cmask, _dot_nt(kc_ref[:, k * HEAD_DIM:(k + 1) * HEAD_DIM], qk), NEG)
        p = jnp.exp2(s_c - jnp.max(s_c, axis=0, keepdims=True)) * cmaskf
        p = p * (1.0 / jnp.maximum(jnp.sum(p, axis=0, keepdims=True), 1e-30))
        o_cmp = _dot(vct_ref[k * HEAD_DIM:(k + 1) * HEAD_DIM, :], p.astype(BF16))
        imp_t = p[:, 0:tq]
        for g in range(1, GROUP):
            imp_t = imp_t + p[:, g * tq:(g + 1) * tq]
        imp2_t = _dot_split_rhs(pairt_ref[...], imp_t, 3)
        score_t = jnp.where(valid_t, jnp.where(forced_t, BIG, imp2_t), -BIG)
        sel_t = _topk_select_t(score_t, n_sel)
        sel = _dot_nt(eye_tq, sel_t.astype(BF16))
        selneg = jnp.where(sel > 0.5, 0.0, NEG).astype(BF16)
        if ns < HEAD_DIM:
            selneg = jnp.concatenate([selneg, jnp.zeros((tq, HEAD_DIM - ns), BF16)], axis=1)
        q_exts.append(jnp.concatenate([qk, jnp.concatenate([selneg] * GROUP, axis=0)], axis=1))
        s_blocks, v_blocks = [], []
        for i in range(nwb):
            kb = qi - (nwb - 1) + i
            inside = kb >= 0
            kbc = jnp.maximum(kb, 0)
            start = pl.multiple_of(kbc * tq, tq)
            s_b = jnp.where(inside, _dot_nt(kw_ref[pl.ds(start, tq), k * HEAD_DIM:(k + 1) * HEAD_DIM], qk), 0.0)
            if i == nwb - 1:
                s_b = s_b + win_diag
            elif i == 0:
                s_b = s_b + win_first
            v_pair = vt_ref[kbc // (kc // tq), KV_WIDTH + k * HEAD_DIM:KV_WIDTH + (k + 1) * HEAD_DIM, :]
            v_b = v_pair[:, 0:tq]
            for part in range(1, kc // tq):
                v_b = jnp.where(kbc % (kc // tq) == part, v_pair[:, part * tq:(part + 1) * tq], v_b)
            s_blocks.append(s_b)
            v_blocks.append(jnp.where(inside, v_b, jnp.zeros_like(v_b)))
        s_w = jnp.concatenate(s_blocks, axis=0)
        p_w = jnp.exp2(s_w - jnp.max(s_w, axis=0, keepdims=True))
        o_win = _dot(jnp.concatenate(v_blocks, axis=1), p_w.astype(BF16)) * (1.0 / jnp.sum(p_w, axis=0, keepdims=True))
        part_scr[k] = gate_row(k, 0) * o_cmp + gate_row(k, 2) * o_win

    mq = memq_ref[...]
    outs = []
    for h in range(MEM_HEADS):
        lo = h * MEM_HEAD_DIM
        s_m = _dot_nt(mk_ref[:, lo:lo + MEM_HEAD_DIM], mq[:, lo:lo + MEM_HEAD_DIM]) * MEM_HEAD_DIM ** -0.5
        pm = jnp.exp(s_m - jnp.max(s_m, axis=0, keepdims=True))
        o_m = _dot(mvt_ref[lo:lo + MEM_HEAD_DIM, :], pm.astype(BF16)) * (1.0 / jnp.sum(pm, axis=0, keepdims=True))
        outs.append(o_m.T)
    memo_ref[...] = jnp.concatenate(outs, axis=1).astype(memo_ref.dtype)

    cur = poolc_ref[...]
    pool_scr[0:16, :] = jnp.where(qi > 0, poolp_ref[...], 0.0)
    pool_scr[16:16 + tq, :] = cur
    tpos1 = t0 + _iota((tq, POOL_GROUP_WIDTH), 0) + 1
    ys = []
    for gi, win in enumerate(POOL_WINDOWS):
        lo = gi * POOL_GROUP_WIDTH
        acc = pool_scr[16:16 + tq, lo:lo + POOL_GROUP_WIDTH]
        for sft in range(1, win):
            acc = acc + pool_scr[16 - sft:16 - sft + tq, lo:lo + POOL_GROUP_WIDTH]
        cnt = jnp.minimum(tpos1, win).astype(F32)
        d = acc / cnt - cur[:, lo:lo + POOL_GROUP_WIDTH]
        ys.append(_dot(d.astype(BF16), wpool_ref[gi]))
    pooly_ref[...] = (jnp.concatenate(ys, axis=1) * pscale_ref[...]).astype(pooly_ref.dtype)

    per = SEL_CHUNKS_PER_STEP
    span = per * kc
    n_full = t0 // span

    def sel_scores(k, sc):
        start = pl.multiple_of(sc * span, span)
        return _dot_nt(kext_ref[pl.ds(start, span), 2 * k * HEAD_DIM:(2 * k + 2) * HEAD_DIM], q_exts[k])

    def sel_values(k, sc):
        return jnp.concatenate([vt_ref[sc * per + j, k * HEAD_DIM:(k + 1) * HEAD_DIM, :] for j in range(per)], axis=1)

    def sel_step(sc, states):
        return tuple(_online_softmax_step(states[k], sel_scores(k, sc), sel_values(k, sc)) for k in range(KV_HEADS))

    states = lax.fori_loop(0, n_full, sel_step, (init,) * KV_HEADS)

    causal = (n_full * span + _iota((span, rows), 0)) <= tok_of_row((span, rows))
    head_out = []
    for k in range(KV_HEADS):
        _, l_k, acc_k = _online_softmax_step(states[k], jnp.where(causal, sel_scores(k, n_full), NEG),
                                             sel_values(k, n_full))
        o_k = part_scr[k] + gate_row(k, 1) * (acc_k * (1.0 / l_k))
        head_out += [o_k[:, g * tq:(g + 1) * tq] for g in range(GROUP)]
    pairs = [jnp.concatenate(head_out[2 * j:2 * j + 2], axis=0).T for j in range(N_HEADS // 2)]
    onsa_ref[...] = jnp.concatenate(pairs, axis=1).astype(onsa_ref.dtype)


def _prompt_attn(q, kc, vct, kext, kw, vt, pool_u, memq, mk, mvt, ng, lw, consts):
    b, t, _ = q.shape
    nc = t // CMP_BLOCK
    ns = t // SEL_BLOCK
    tq = TQ
    chunk = vt.shape[3]
    m = mk.shape[1]
    span = SEL_CHUNKS_PER_STEP * chunk
    assert WINDOW % tq == 0 and chunk % tq == 0 and t % span == 0 and span % tq == 0
    assert ns <= HEAD_DIM and tq == NSAG_PAD
    tok = lambda w: pl.BlockSpec((None, tq, w), lambda i, j: (i, j, 0))
    per_b = lambda r, w: pl.BlockSpec((None, r, w), lambda i, j: (i, 0, 0))
    prev_rows = pl.BlockSpec((None, 16, POOL_WIDTH), lambda i, j: (i, jnp.maximum(j * (tq // 16) - 1, 0), 0))
    out = jax.ShapeDtypeStruct((b, t, NSA_WIDTH), BF16)
    return pl.pallas_call(
        _prompt_attn_kernel,
        grid=(b, t // tq),
        in_specs=[tok(NSA_WIDTH), per_b(nc, KV_WIDTH), per_b(KV_WIDTH, nc), per_b(t, 2 * KV_WIDTH), per_b(t, KV_WIDTH),
                  pl.BlockSpec((None, t // chunk, 2 * KV_WIDTH, chunk), lambda i, j: (i, 0, 0, 0)),
                  tok(POOL_WIDTH), prev_rows, tok(MEM_WIDTH), per_b(m, MEM_WIDTH), per_b(MEM_WIDTH, m), tok(NSAG_PAD),
                  _const_spec((ns, nc)),
                  _const_spec((POOL_GROUPS, POOL_GROUP_WIDTH, POOL_GROUP_WIDTH)), _const_spec((1, POOL_WIDTH))],
        out_specs=[tok(NSA_WIDTH), tok(POOL_WIDTH), tok(MEM_WIDTH)],
        out_shape=[out, out, out],
        scratch_shapes=[pltpu.VMEM((16 + tq, POOL_WIDTH), F32), pltpu.VMEM((KV_HEADS, HEAD_DIM, GROUP * tq), F32)],
        compiler_params=_params("parallel", "arbitrary"),
        name="prompt_attn",
    )(q, kc, vct, kext, kw, vt, pool_u, pool_u, memq, mk, mvt, ng, consts["pair_t"], lw["w_pool"], lw["pool_scale"])


def _merge_ffn_kernel(x_ref, onsa_ref, pooly_ref, memo_ref, mg_ref, wn_ref, wp_ref, wm_ref, wo_ref, fg_ref,
                      wgu_ref, wd_ref, y_ref):
    h = (mg_ref[:, 0:D_MODEL] * _dot(onsa_ref[...].astype(BF16), wn_ref[...])
         + mg_ref[:, D_MODEL:2 * D_MODEL] * _dot(pooly_ref[...].astype(BF16), wp_ref[...])
         + mg_ref[:, 2 * D_MODEL:3 * D_MODEL] * _dot(memo_ref[...].astype(BF16), wm_ref[...]))
    x1 = x_ref[...] + _dot(h.astype(BF16), wo_ref[...])
    hn = _rms_rows(x1, fg_ref[...]).astype(BF16)
    acc = x1
    for j in range(D_FF // FF_CHUNK):
        lo = j * FF_CHUNK
        gate = _dot(hn, wgu_ref[:, lo:lo + FF_CHUNK])
        up = _dot(hn, wgu_ref[:, D_FF + lo:D_FF + lo + FF_CHUNK])
        act = gate * jax.nn.sigmoid(gate) * up
        acc = acc + _dot(act.astype(BF16), wd_ref[lo:lo + FF_CHUNK, :])
    y_ref[...] = acc


def _merge_ffn(x, onsa, pooly, memo, mg, lw):
    n = x.shape[0]
    tm = min(TM_PROJ, n)
    row = lambda w: pl.BlockSpec((tm, w), lambda i: (i, 0))
    return pl.pallas_call(
        _merge_ffn_kernel,
        grid=(n // tm,),
        in_specs=[row(D_MODEL), row(NSA_WIDTH), row(POOL_WIDTH), row(MEM_WIDTH), row(N_BRANCH * D_MODEL),
                  _const_spec((NSA_WIDTH, D_MODEL)), _const_spec((POOL_WIDTH, D_MODEL)), _const_spec((MEM_WIDTH, D_MODEL)),
                  _const_spec((D_MODEL, D_MODEL)), _const_spec((1, D_MODEL)),
                  _const_spec((D_MODEL, 2 * D_FF)), _const_spec((D_FF, D_MODEL))],
        out_specs=row(D_MODEL),
        out_shape=jax.ShapeDtypeStruct((n, D_MODEL), F32),
        compiler_params=_params("parallel"),
        name="merge_ffn",
    )(x, onsa, pooly, memo, mg, lw["w_up_nsa"], lw["w_up_pool"], lw["w_up_mem"], lw["w_out"], lw["ffn_g"],
      lw["w_gate_up"], lw["w_down"])


def _page_specs(npg, half, layer_of, batch_of, step_of):
    def spec(i):
        def index_map(*args):
            pt = args[-1]
            grid = args[:-1]
            return (layer_of(grid), pt[batch_of(grid), step_of(grid) * npg + i], half, 0)
        return pl.BlockSpec((None, None, 2 * KV_WIDTH, PAGE_SIZE), index_map)
    return [spec(i) for i in range(npg)]


def _compress_pages_kernel(pt_ref, *refs):
    del pt_ref
    npg = len(refs) - 7
    pages = refs[:npg]
    pe_ref, wkt_ref, wvt_ref, kcg_ref, ones64_ref, poolm_ref, out_ref = refs[npg:]
    x = jnp.concatenate([pg[...] for pg in pages], axis=1)
    pooled = _dot_split_lhs(x, poolm_ref[...], 2)
    pooled = pooled + jnp.sum(pe_ref[...], axis=1, keepdims=True) * (1.0 / CMP_BLOCK)
    k = _dot_f32(wkt_ref[...], pooled[0:KV_WIDTH])
    v = _dot_f32(wvt_ref[...], pooled[KV_WIDTH:2 * KV_WIDTH])
    ssq = _dot_split_rhs(ones64_ref[...], k * k, 2)
    out_ref[0:KV_WIDTH, :] = k * lax.rsqrt(ssq * (1.0 / HEAD_DIM) + EPS) * kcg_ref[...]
    out_ref[KV_WIDTH:2 * KV_WIDTH, :] = v


def _compress_pages(cache_t, page_table, w):
    depth = cache_t.shape[0]
    bd, n_pages = page_table.shape
    npg = min(PAGES_PER_STEP_CMP, n_pages)
    per_page = PAGE_SIZE // CMP_BLOCK
    per_step = npg * per_page
    ncs = n_pages * per_page
    lconst = lambda r, c: pl.BlockSpec((None, r, c), lambda l, b, s, pt: (l, 0, 0))
    const = lambda r, c: pl.BlockSpec((r, c), lambda l, b, s, pt: (0, 0))
    grid_spec = pltpu.PrefetchScalarGridSpec(
        num_scalar_prefetch=1,
        grid=(depth, bd, n_pages // npg),
        in_specs=_page_specs(npg, 0, lambda g: g[0], lambda g: g[1], lambda g: g[2])
        + [lconst(2 * KV_WIDTH, CMP_BLOCK), lconst(KV_WIDTH, KV_WIDTH), lconst(KV_WIDTH, KV_WIDTH), lconst(KV_WIDTH, 1),
           const(KV_WIDTH, KV_WIDTH), const(npg * PAGE_SIZE, per_step)],
        out_specs=pl.BlockSpec((None, None, 2 * KV_WIDTH, per_step), lambda l, b, s, pt: (l, b, 0, s)),
    )
    rows = jnp.arange(npg * PAGE_SIZE)[:, None] // CMP_BLOCK
    poolm = jnp.where(rows == jnp.arange(per_step)[None, :], 1.0 / CMP_BLOCK, 0.0).astype(BF16)
    return pl.pallas_call(
        _compress_pages_kernel,
        grid_spec=grid_spec,
        out_shape=jax.ShapeDtypeStruct((depth, bd, 2 * KV_WIDTH, ncs), F32),
        compiler_params=_params("parallel", "parallel", "arbitrary"),
        name="compress_pages",
    )(page_table, *([cache_t] * npg), w["cmp_pe_t"], w["cmp_wk_t"], w["cmp_wv_t"], w["kc_g_col"], w["ones64_kv"], poolm)


def _rows8(row):
    return jnp.broadcast_to(row, (V7X_SUBLANES, row.shape[1]))


def _sample_attn_kernel(pt_ref, *refs, past_len):
    del pt_ref
    npg = len(refs) - 24
    pages = refs[:npg]
    (q_ref, nsakv_ref, winkv_ref, ng_ref, poolu_ref, memq_ref, kcvc_ref, wincache_ref, poolstate_ref, memcache_ref,
     e_ref, pair_ref, fold_ref, unfold_ref, wpool_ref, pscale_ref,
     onsa_ref, pooly_ref, memo_ref,
     qbd_scr, bias_scr, m_scr, l_scr, acc_scr) = refs[npg:]
    step = pl.program_id(1)
    n_steps = pl.num_programs(1)
    ncs = kcvc_ref.shape[1]
    nsp = pair_ref.shape[1]
    keys_per_step = npg * PAGE_SIZE
    pos = past_len
    row8 = _iota((8, KV_WIDTH), 0)
    lane8 = _iota((8, KV_WIDTH), 1)
    head_lanes = (lane8 // HEAD_DIM) == (row8 // GROUP)

    @pl.when(step == 0)
    def _():
        q8 = jnp.where((_iota((8, NSA_WIDTH), 1) // HEAD_DIM) == _iota((8, NSA_WIDTH), 0),
                       _rows8(q_ref[...].astype(F32)), 0.0)
        qbd = _dot(q8.astype(BF16), fold_ref[...])
        qbd_scr[...] = qbd
        s = _dot(qbd.astype(BF16), kcvc_ref[0:KV_WIDTH, :].astype(BF16))
        cmask = (_iota((8, ncs), 1) * CMP_BLOCK + (CMP_BLOCK - 1)) <= pos
        p = _masked_softmax_rows(s[None], cmask)[0]
        ocmp = _dot_nt(p.astype(BF16), kcvc_ref[KV_WIDTH:2 * KV_WIDTH, :].astype(BF16))
        acc_scr[0] = jnp.where(head_lanes, ocmp, 0.0)
        eye = _iota((nsp, nsp), 0) == _iota((nsp, nsp), 1)
        ii = _iota((nsp, nsp), 1)
        jj = _iota((nsp, nsp), 0)
        cur = pos // SEL_BLOCK
        sel_rows = []
        for k in range(KV_HEADS):
            imp = jnp.sum(p[k * GROUP:(k + 1) * GROUP], axis=0, keepdims=True)
            imp2 = _dot_split_lhs(_rows8(imp), pair_ref[...], 3)[0:1]
            blk = _iota((1, nsp), 1)
            forced = (blk == 0) | (blk == cur) | (blk == cur - 1)
            valid = blk * SEL_BLOCK <= pos
            score = jnp.where(valid, jnp.where(forced, BIG, imp2), -BIG)
            score_i = jnp.broadcast_to(score, (nsp, nsp))
            score_j = jnp.sum(jnp.where(eye, score_i, 0.0), axis=1, keepdims=True)
            beats = jnp.where((score_j > score_i) | ((score_j == score_i) & (jj < ii)), 1.0, 0.0)
            cnt = jnp.sum(beats, axis=0, keepdims=True)
            cnt = cnt + jnp.where(score < BIG, 1.0, 0.0)
            sel_rows.append(jnp.where(cnt < TOP_K, 1.0, 0.0))
        sel8 = jnp.where(_iota((8, nsp), 0) < GROUP, _rows8(sel_rows[0]), _rows8(sel_rows[1])).astype(BF16)
        for st in range(bias_scr.shape[0]):
            selx = _dot(sel8, e_ref[:, st * keys_per_step:(st + 1) * keys_per_step])
            bias_scr[st] = jnp.where(selx > 0.5, 0.0, NEG)
        m_scr[...] = jnp.full(m_scr.shape, NEG, F32)
        l_scr[...] = jnp.zeros(l_scr.shape, F32)
        acc_scr[1] = jnp.zeros((8, KV_WIDTH), F32)

    qbd = qbd_scr[...]
    qbd_bf = qbd.astype(BF16)
    k_t = jnp.concatenate([pg[0:KV_WIDTH, :] for pg in pages], axis=1).astype(BF16)
    v_t = jnp.concatenate([pg[KV_WIDTH:2 * KV_WIDTH, :] for pg in pages], axis=1).astype(BF16)
    s_all = _dot(qbd_bf, k_t) + bias_scr[step]
    m_old = m_scr[...]
    m_new = jnp.maximum(m_old, jnp.max(s_all, axis=-1, keepdims=True))
    p_all = jnp.exp2(s_all - m_new) * jnp.where(s_all > 0.5 * NEG, 1.0, 0.0)
    alpha = jnp.exp2(m_old - m_new)
    l_scr[...] = alpha * l_scr[...] + jnp.sum(p_all, axis=-1, keepdims=True)
    acc_scr[1] = alpha * acc_scr[1] + _dot_nt(p_all.astype(BF16), v_t)
    m_scr[...] = m_new

    @pl.when(step == n_steps - 1)
    def _():
        ks_new = _rows8(nsakv_ref[:, 2 * KV_WIDTH:3 * KV_WIDTH])
        vs_new = _rows8(nsakv_ref[:, 3 * KV_WIDTH:4 * KV_WIDTH])
        s_new = jnp.sum(qbd * ks_new, axis=-1, keepdims=True)
        m_old = m_scr[...]
        m_fin = jnp.maximum(m_old, s_new)
        alpha = jnp.exp2(m_old - m_fin)
        p_new = jnp.exp2(s_new - m_fin)
        l_fin = alpha * l_scr[...] + p_new
        osel = (alpha * acc_scr[1] + p_new * vs_new) * (1.0 / l_fin)
        wb = wincache_ref.shape[1]
        kw_new = _rows8(winkv_ref[:, 0:KV_WIDTH])
        vw_new = _rows8(winkv_ref[:, KV_WIDTH:2 * KV_WIDTH])
        s_w = _dot(qbd_bf, wincache_ref[0:KV_WIDTH, :].astype(BF16))
        kpos = pos - wb + _iota((8, wb), 1)
        wmask = (kpos <= pos) & (kpos > pos - WINDOW)
        s_wn = jnp.sum(qbd * kw_new, axis=-1, keepdims=True)
        s_w = jnp.where(wmask, s_w, NEG)
        m_w = jnp.maximum(jnp.max(s_w, axis=-1, keepdims=True), s_wn)
        p_w = jnp.exp2(s_w - m_w) * jnp.where(wmask, 1.0, 0.0)
        p_wn = jnp.exp2(s_wn - m_w)
        l_w = jnp.sum(p_w, axis=-1, keepdims=True) + p_wn
        owin = (_dot_nt(p_w.astype(BF16), wincache_ref[KV_WIDTH:2 * KV_WIDTH, :].astype(BF16)) + p_wn * vw_new) * (1.0 / l_w)
        ng8 = _rows8(ng_ref[...])
        glane = _iota((8, NSAG_PAD), 1)
        grow = _iota((8, NSAG_PAD), 0)
        gate = lambda br: jnp.sum(jnp.where(glane == N_NSA_BRANCH * grow + br, ng8, 0.0), axis=-1, keepdims=True)
        o8 = jnp.where(head_lanes, gate(0) * acc_scr[0] + gate(1) * osel + gate(2) * owin, 0.0)
        o512 = _dot_split_lhs(o8, unfold_ref[...], 3)
        own = (_iota((8, NSA_WIDTH), 1) // HEAD_DIM) == _iota((8, NSA_WIDTH), 0)
        onsa_ref[...] = jnp.sum(jnp.where(own, o512, 0.0), axis=0, keepdims=True)
        mtok = memcache_ref.shape[0] // (2 * MEM_HEADS)
        outs = []
        for h in range(MEM_HEADS):
            k_h = memcache_ref[pl.ds(h, mtok, stride=2 * MEM_HEADS), :].astype(BF16)
            v_h = memcache_ref[pl.ds(MEM_HEADS + h, mtok, stride=2 * MEM_HEADS), :].astype(BF16)
            q_h = _rows8(memq_ref[:, h * MEM_HEAD_DIM:(h + 1) * MEM_HEAD_DIM].astype(F32)).astype(BF16)
            s_m = _dot_nt(q_h, k_h) * MEM_HEAD_DIM ** -0.5
            p_m = jnp.exp(s_m - jnp.max(s_m, axis=-1, keepdims=True))
            p_m = p_m * (1.0 / jnp.sum(p_m, axis=-1, keepdims=True))
            outs.append(_dot(p_m.astype(BF16), v_h)[0:1])
        memo_ref[...] = jnp.concatenate(outs, axis=1)
        u_new = poolu_ref[...]
        state = poolstate_ref[...]
        srow = _iota((POOL_BUF, POOL_GROUP_WIDTH), 0)
        ys = []
        for gi, win in enumerate(POOL_WINDOWS):
            lo = gi * POOL_GROUP_WIDTH
            u_g = u_new[:, lo:lo + POOL_GROUP_WIDTH]
            tail = jnp.sum(jnp.where(srow >= POOL_BUF - (win - 1), state[:, lo:lo + POOL_GROUP_WIDTH], 0.0),
                           axis=0, keepdims=True)
            d = (tail + u_g) / float(min(pos + 1, win)) - u_g
            ys.append(_dot(_rows8(d).astype(BF16), wpool_ref[gi])[0:1])
        pooly_ref[...] = jnp.concatenate(ys, axis=1) * pscale_ref[...]


def _sample_attn(l, q, nsakv, winkv, ng, pool_u, memq, kcvc_s, cache_t, wincache_t, poolstate, memcache, page_table,
                 lw, consts):
    bd, n_pages = page_table.shape
    npg = min(PAGES_PER_STEP_SEL, n_pages)
    n_steps = n_pages // npg
    past_len = n_pages * PAGE_SIZE
    ncs = past_len // CMP_BLOCK
    nsp = past_len // SEL_BLOCK
    wb = wincache_t.shape[3]
    mrows = memcache.shape[2]
    row = lambda w: pl.BlockSpec((None, 1, w), lambda b, s, pt: (b, 0, 0))
    const = lambda shape: pl.BlockSpec(shape, lambda b, s, pt: (0,) * len(shape))
    per_b = lambda r, c: pl.BlockSpec((None, None, r, c), lambda b, s, pt: (l, b, 0, 0))
    in_specs = (
        _page_specs(npg, 1, lambda g: l, lambda g: g[0], lambda g: g[1])
        + [row(NSA_WIDTH), row(4 * KV_WIDTH), row(2 * KV_WIDTH), row(NSAG_PAD), row(POOL_WIDTH), row(MEM_WIDTH),
           per_b(2 * KV_WIDTH, ncs), per_b(2 * KV_WIDTH, wb), per_b(POOL_BUF, POOL_WIDTH), per_b(mrows, MEM_HEAD_DIM),
           const((nsp, past_len)), const((ncs, nsp)), const((NSA_WIDTH, KV_WIDTH)), const((KV_WIDTH, NSA_WIDTH)),
           const((POOL_GROUPS, POOL_GROUP_WIDTH, POOL_GROUP_WIDTH)), const((1, POOL_WIDTH))])
    grid_spec = pltpu.PrefetchScalarGridSpec(
        num_scalar_prefetch=1,
        grid=(bd, n_steps),
        in_specs=in_specs,
        out_specs=[row(NSA_WIDTH), row(POOL_WIDTH), row(MEM_WIDTH)],
        scratch_shapes=[pltpu.VMEM((8, KV_WIDTH), F32), pltpu.VMEM((n_steps, 8, npg * PAGE_SIZE), F32),
                        pltpu.VMEM((8, 1), F32), pltpu.VMEM((8, 1), F32), pltpu.VMEM((2, 8, KV_WIDTH), F32)],
    )
    out = jax.ShapeDtypeStruct((bd, 1, NSA_WIDTH), F32)
    return pl.pallas_call(
        functools.partial(_sample_attn_kernel, past_len=past_len),
        grid_spec=grid_spec,
        out_shape=[out, out, out],
        compiler_params=_params("parallel", "arbitrary"),
        name="sample_attn",
    )(page_table, *([cache_t] * npg), q, nsakv, winkv, ng, pool_u, memq, kcvc_s, wincache_t, poolstate, memcache,
      consts["sel_expand_s"], consts["pair_s"], consts["fold"], consts["unfold"], lw["w_pool"], lw["pool_scale"])


def _block_diag_ones(n, seg):
    return (jnp.arange(n)[:, None] // seg == jnp.arange(n)[None, :] // seg).astype(BF16)


def _constants(t, past_len):
    nc, ns = t // CMP_BLOCK, t // SEL_BLOCK
    ratio = SEL_BLOCK // CMP_BLOCK
    pair_t = (jnp.arange(ns)[:, None] == jnp.arange(nc)[None, :] // ratio).astype(BF16)
    ncs, nsp = past_len // CMP_BLOCK, past_len // SEL_BLOCK
    sel_expand_s = (jnp.arange(nsp)[:, None] == (jnp.arange(past_len) // SEL_BLOCK)[None, :]).astype(BF16)
    pair_s = (jnp.arange(ncs)[:, None] // ratio == jnp.arange(nsp)[None, :]).astype(BF16)
    c = jnp.arange(NSA_WIDTH)
    fold_col = (c // (GROUP * HEAD_DIM)) * HEAD_DIM + c % HEAD_DIM
    fold = (fold_col[:, None] == jnp.arange(KV_WIDTH)[None, :]).astype(BF16)
    return {
        "ones64": _block_diag_ones(NSA_WIDTH, HEAD_DIM), "ones128": _block_diag_ones(MEM_WIDTH, MEM_HEAD_DIM),
        "ones64_kv": _block_diag_ones(KV_WIDTH, HEAD_DIM),
        "pair_t": pair_t,
        "sel_expand_s": sel_expand_s, "pair_s": pair_s, "fold": fold, "unfold": fold.T,
    }


def _tile_row(g, reps):
    return jnp.tile(g, reps)[None, :].astype(F32)


def kernel(x_prompt, x_sample, mem_prompt, cache_nsa_kv, cache_win_kv, state_pool, cache_mem_kv, page_table,
           attn_norm_g, w_in, nsa_q_g, nsa_kc_g, nsa_ks_g, nsa_kw_g, cmp_pe_k, cmp_pe_v, cmp_wk, cmp_wv,
           w_pool, pool_scale, mem_norm_g, w_mem_kv, mem_q_g, mem_k_g, w_up_nsa, w_up_pool, w_up_mem, w_out,
           ffn_norm_g, w_gate_up, w_down):
    depth = w_in.shape[0]
    b, t, _ = x_prompt.shape
    bd = x_sample.shape[0]
    n_pages = page_table.shape[1]
    past_len = n_pages * PAGE_SIZE
    mtok = mem_prompt.shape[1]
    wb = cache_win_kv.shape[2]
    assert x_sample.shape[1] == 1 and wb == WINDOW and past_len >= WINDOW and mtok == cache_mem_kv.shape[2]
    consts = _constants(t, past_len)

    nsag_lo = NSA_WIDTH + 6 * KV_WIDTH
    nsag_hi = nsag_lo + N_NSA_BRANCH * N_HEADS
    w_in_r = jnp.concatenate(
        [w_in[:, :, :nsag_lo], w_in[:, :, nsag_hi:], w_in[:, :, nsag_lo:nsag_hi],
         jnp.zeros((depth, D_MODEL, NSAG_PAD - N_NSA_BRANCH * N_HEADS), w_in.dtype)], axis=2).astype(BF16)
    vs_lo, vw_lo = NSA_WIDTH + 3 * KV_WIDTH, NSA_WIDTH + 5 * KV_WIDTH
    w_v_t = jnp.concatenate([w_in[:, :, vs_lo:vs_lo + KV_WIDTH], w_in[:, :, vw_lo:vw_lo + KV_WIDTH]],
                            axis=2).transpose(0, 2, 1).astype(BF16)
    w_mem_v_t = w_mem_kv[:, :, MEM_WIDTH:].transpose(0, 2, 1).astype(BF16)
    eye_kv = jnp.eye(KV_HEADS, dtype=F32)
    layers = []
    for l in range(depth):
        layers.append({
            "attn_g": attn_norm_g[l][None, :], "w_in": w_in_r[l], "w_v_t": w_v_t[l], "w_mem_v_t": w_mem_v_t[l],
            "q_g": _tile_row(nsa_q_g[l], N_HEADS), "ks_g": _tile_row(nsa_ks_g[l], KV_HEADS),
            "kw_g": _tile_row(nsa_kw_g[l], KV_HEADS), "mq_g": _tile_row(mem_q_g[l], MEM_HEADS),
            "kc_g": _tile_row(nsa_kc_g[l], KV_HEADS),
            "cmp_pe": jnp.concatenate([jnp.tile(cmp_pe_k[l], (1, KV_HEADS)), jnp.tile(cmp_pe_v[l], (1, KV_HEADS))], axis=1),
            "cmp_wk": jnp.kron(eye_kv, cmp_wk[l]), "cmp_wv": jnp.kron(eye_kv, cmp_wv[l]),
            "w_pool": w_pool[l].astype(BF16), "pool_scale": pool_scale[l][None, :],
            "mem_g": mem_norm_g[l][None, :], "w_mem_kv": w_mem_kv[l].astype(BF16), "mk_g": _tile_row(mem_k_g[l], MEM_HEADS),
            "w_up_nsa": w_up_nsa[l].astype(BF16), "w_up_pool": w_up_pool[l].astype(BF16),
            "w_up_mem": w_up_mem[l].astype(BF16), "w_out": w_out[l].astype(BF16),
            "ffn_g": ffn_norm_g[l][None, :], "w_gate_up": w_gate_up[l].astype(BF16), "w_down": w_down[l].astype(BF16),
        })
    stack = lambda key, f: jnp.stack([f(lw[key]) for lw in layers])
    cmp_w = {"cmp_pe_t": stack("cmp_pe", jnp.transpose), "cmp_wk_t": stack("cmp_wk", jnp.transpose),
             "cmp_wv_t": stack("cmp_wv", jnp.transpose), "kc_g_col": stack("kc_g", jnp.transpose),
             "ones64_kv": consts["ones64_kv"]}

    cache_t = cache_nsa_kv.transpose(0, 1, 3, 4, 5, 2).reshape(depth, cache_nsa_kv.shape[1], 4 * KV_WIDTH, PAGE_SIZE)
    wincache_t = cache_win_kv.transpose(0, 1, 3, 4, 5, 2).reshape(depth, bd, 2 * KV_WIDTH, wb)
    memcache = cache_mem_kv.reshape(depth, bd, cache_mem_kv.shape[2] * 2 * MEM_HEADS, MEM_HEAD_DIM)
    kcvc_s = _compress_pages(cache_t, page_table, cmp_w)

    xp = x_prompt.reshape(b * t, D_MODEL)
    xs = x_sample.reshape(bd, D_MODEL)
    nsa_p, nsa_s, win_p, win_s, pool_p, pool_s, mem_p = [], [], [], [], [], [], []
    for l, lw in enumerate(layers):
        q, nsakv, winkv, kext, kwbf, vt, pool_u, memq, mg, ng = _inproj(xp, lw, consts, t)
        mkv, mk, mvt = _memkv(mem_prompt, lw, consts)
        nsakv3 = nsakv.reshape(b, t, 4 * KV_WIDTH)
        kc, vct = _compress_prompt(nsakv3, lw, consts)
        pool3 = pool_u.reshape(b, t, POOL_WIDTH)
        by_b = lambda a: a.reshape(b, t, a.shape[-1])
        onsa, pooly, memo = _prompt_attn(
            by_b(q), kc, vct, by_b(kext), by_b(kwbf), vt.reshape(b, -1, vt.shape[1], vt.shape[2]), pool3, by_b(memq),
            mk, mvt, by_b(ng), lw, consts)
        xp = _merge_ffn(xp, onsa.reshape(b * t, NSA_WIDTH), pooly.reshape(b * t, POOL_WIDTH),
                        memo.reshape(b * t, MEM_WIDTH), mg, lw)
        nsa_p.append(nsakv3.reshape(b, t, 4, KV_HEADS, HEAD_DIM))
        win_p.append(winkv.reshape(b, t, 2, KV_HEADS, HEAD_DIM)[:, t - min(WINDOW, t):])
        pool_p.append(pool3[:, t - POOL_BUF:])
        mem_p.append(mkv.reshape(b, mtok, 2, MEM_HEADS, MEM_HEAD_DIM))
        q, nsakv, winkv, _, _, _, pool_u, memq, mg, ng = _inproj(xs, lw, consts, bd)
        r3 = lambda a: a.reshape(bd, 1, a.shape[-1])
        onsa, pooly, memo = _sample_attn(l, r3(q), r3(nsakv), r3(winkv), r3(ng), r3(pool_u), r3(memq), kcvc_s, cache_t,
                                         wincache_t, state_pool, memcache, page_table, lw, consts)
        xs = _merge_ffn(xs, onsa.reshape(bd, NSA_WIDTH), pooly.reshape(bd, POOL_WIDTH), memo.reshape(bd, MEM_WIDTH), mg, lw)
        nsa_s.append(nsakv.reshape(bd, 1, 4, KV_HEADS, HEAD_DIM))
        win_s.append(jnp.concatenate([cache_win_kv[l][:, 1:], winkv.reshape(bd, 1, 2, KV_HEADS, HEAD_DIM)], axis=1))
        pool_s.append(jnp.concatenate([state_pool[l][:, 1:], pool_u.reshape(bd, 1, POOL_WIDTH)], axis=1))
    return (xp.reshape(b, t, D_MODEL), xs.reshape(bd, 1, D_MODEL), jnp.stack(nsa_p), jnp.stack(nsa_s),
            jnp.stack(win_p), jnp.stack(win_s), jnp.stack(pool_p), jnp.stack(pool_s), jnp.stack(mem_p))
```

```python
import functools

import jax
import jax.numpy as jnp
from jax import lax
from jax.experimental import pallas as pl
from jax.experimental.pallas import tpu as pltpu

F32 = jnp.float32
BF16 = jnp.bfloat16

D_MODEL = 1024
PAGE_SIZE = 128
N_HEADS = 8
HEAD_DIM = 64
KV_HEADS = 2
GROUP = N_HEADS // KV_HEADS
NSA_WIDTH = N_HEADS * HEAD_DIM
KV_WIDTH = KV_HEADS * HEAD_DIM
CMP_BLOCK = 32
SEL_BLOCK = 64
TOP_K = 16
WINDOW = 512
N_NSA_BRANCH = 3
POOL_WIDTH = 512
POOL_GROUPS = 4
POOL_GROUP_WIDTH = POOL_WIDTH // POOL_GROUPS
POOL_WINDOWS = (2, 4, 8, 16)
POOL_BUF = 15
MEM_HEADS = 4
MEM_HEAD_DIM = 128
MEM_WIDTH = MEM_HEADS * MEM_HEAD_DIM
N_BRANCH = 3
D_FF = ((8 * D_MODEL // 3 + 255) // 256) * 256
EPS = 1e-6
LOG2_E = 1.4426950408889634
NEG = -1e30
BIG = 1e9

V7X_LANES = 128
V7X_SUBLANES = 8
ONES_ROWS = 2 * V7X_SUBLANES
V7X_VMEM_LIMIT_BYTES = 56 * 1024 * 1024

C_Q = 0
C_NSAKV = C_Q + NSA_WIDTH
C_WINKV = C_NSAKV + 4 * KV_WIDTH
C_POOL = C_WINKV + 2 * KV_WIDTH
C_MEMQ = C_POOL + POOL_WIDTH
C_MERGE = C_MEMQ + MEM_WIDTH
C_NSAG = C_MERGE + N_BRANCH * D_MODEL
NSAG_PAD = V7X_LANES
PROJ_PAD = C_NSAG + NSAG_PAD

TM_PROJ = 256
TQ = 256
SEL_CHUNKS_PER_STEP = 2
FF_CHUNK = D_FF // 2
PAGES_PER_STEP_CMP = 32
PAGES_PER_STEP_SEL = 32


def _dot(a, b):
    return jnp.dot(a, b, preferred_element_type=F32)


def _dot_nt(a, b):
    return lax.dot_general(a, b, (((1,), (1,)), ((), ())), preferred_element_type=F32)


def _dot_f32(a, b):
    return jnp.dot(a, b, preferred_element_type=F32, precision=lax.Precision.HIGHEST)


def _split_bf16(a, n):
    parts, r = [], a
    for _ in range(n):
        p = r.astype(BF16)
        parts.append(p)
        r = r - p.astype(F32)
    return parts


def _dot_split_lhs(a, b, n):
    return sum(_dot(p, b) for p in _split_bf16(a, n))


def _dot_split_rhs(a, b, n):
    return sum(_dot(a, p) for p in _split_bf16(b, n))


def _iota(shape, dim):
    return lax.broadcasted_iota(jnp.int32, shape, dim)


def _rms_rows(x, g):
    return x * lax.rsqrt(jnp.mean(x * x, axis=-1, keepdims=True) + EPS) * g


def _seg_rmsnorm(v, ones_bd, gain, seg):
    ssq = _dot_split_lhs(v * v, ones_bd, 2)
    return v * lax.rsqrt(ssq * (1.0 / seg) + EPS) * gain


def _const_spec(shape):
    return pl.BlockSpec(shape, lambda *_: (0,) * len(shape), pipeline_mode=pl.Buffered(1))


def _params(*sem):
    return pltpu.CompilerParams(dimension_semantics=sem, vmem_limit_bytes=V7X_VMEM_LIMIT_BYTES)


def _inproj_kernel(x_ref, g_ref, w_ref, wvt_ref, qg_ref, ksg_ref, kwg_ref, mqg_ref, ones64_ref, ones128_ref,
                   q_ref, nsakv_ref, winkv_ref, kext_ref, kwbf_ref, vt_ref, pool_ref, memq_ref, mg_ref, ng_ref,
                   *, seq_len):
    tm = x_ref.shape[0]
    hb = _rms_rows(x_ref[...], g_ref[...]).astype(BF16)

    def proj(lo, width):
        return _dot(hb, w_ref[:, lo:lo + width])

    ones_kv = ones64_ref[0:KV_WIDTH, 0:KV_WIDTH]
    q = _seg_rmsnorm(proj(C_Q, NSA_WIDTH), ones64_ref[...], qg_ref[...], HEAD_DIM)
    q_ref[...] = (q * (HEAD_DIM ** -0.5 * LOG2_E)).astype(BF16)
    kcvc = proj(C_NSAKV, 2 * KV_WIDTH)
    ks = _seg_rmsnorm(proj(C_NSAKV + 2 * KV_WIDTH, KV_WIDTH), ones_kv, ksg_ref[...], HEAD_DIM)
    vs = proj(C_NSAKV + 3 * KV_WIDTH, KV_WIDTH)
    kw = _seg_rmsnorm(proj(C_WINKV, KV_WIDTH), ones_kv, kwg_ref[...], HEAD_DIM)
    vw = proj(C_WINKV + KV_WIDTH, KV_WIDTH)
    nsakv_ref[:, 0:2 * KV_WIDTH] = kcvc
    nsakv_ref[:, 2 * KV_WIDTH:3 * KV_WIDTH] = ks
    nsakv_ref[:, 3 * KV_WIDTH:4 * KV_WIDTH] = vs
    winkv_ref[:, 0:KV_WIDTH] = kw
    winkv_ref[:, KV_WIDTH:2 * KV_WIDTH] = vw
    t_seq = (pl.program_id(0) * tm + _iota((tm, HEAD_DIM), 0)) % seq_len
    onehot = jnp.where(t_seq // SEL_BLOCK == _iota((tm, HEAD_DIM), 1), 1.0, 0.0).astype(BF16)
    for k in range(KV_HEADS):
        kext_ref[:, 2 * k * HEAD_DIM:(2 * k + 1) * HEAD_DIM] = ks[:, k * HEAD_DIM:(k + 1) * HEAD_DIM].astype(BF16)
        kext_ref[:, (2 * k + 1) * HEAD_DIM:(2 * k + 2) * HEAD_DIM] = onehot
    kwbf_ref[...] = kw.astype(BF16)
    vt_ref[...] = _dot_nt(wvt_ref[...], hb).astype(BF16)
    pool_ref[...] = proj(C_POOL, POOL_WIDTH)
    mq = _seg_rmsnorm(proj(C_MEMQ, MEM_WIDTH), ones128_ref[...], mqg_ref[...], MEM_HEAD_DIM)
    memq_ref[...] = mq.astype(BF16)
    for j in range(N_BRANCH):
        mg_ref[:, j * D_MODEL:(j + 1) * D_MODEL] = jax.nn.sigmoid(proj(C_MERGE + j * D_MODEL, D_MODEL))
    ng_ref[...] = jax.nn.sigmoid(proj(C_NSAG, NSAG_PAD))


def _inproj(x, lw, consts, seq_len):
    n = x.shape[0]
    tm = min(TM_PROJ, n)
    row = lambda w: pl.BlockSpec((tm, w), lambda i: (i, 0))
    widths = (NSA_WIDTH, 4 * KV_WIDTH, 2 * KV_WIDTH, 2 * KV_WIDTH, KV_WIDTH, None, POOL_WIDTH, MEM_WIDTH,
              N_BRANCH * D_MODEL, NSAG_PAD)
    dtypes = (BF16, F32, F32, BF16, BF16, BF16, F32, BF16, F32, F32)
    vt_spec = pl.BlockSpec((None, 2 * KV_WIDTH, tm), lambda i: (i, 0, 0))
    vt_shape = jax.ShapeDtypeStruct((n // tm, 2 * KV_WIDTH, tm), BF16)
    return pl.pallas_call(
        functools.partial(_inproj_kernel, seq_len=seq_len),
        grid=(n // tm,),
        in_specs=[row(D_MODEL), _const_spec((1, D_MODEL)), _const_spec((D_MODEL, PROJ_PAD)),
                  _const_spec((2 * KV_WIDTH, D_MODEL)),
                  _const_spec((1, NSA_WIDTH)), _const_spec((1, KV_WIDTH)), _const_spec((1, KV_WIDTH)),
                  _const_spec((1, MEM_WIDTH)), _const_spec((NSA_WIDTH, NSA_WIDTH)), _const_spec((MEM_WIDTH, MEM_WIDTH))],
        out_specs=[vt_spec if w is None else row(w) for w in widths],
        out_shape=[vt_shape if w is None else jax.ShapeDtypeStruct((n, w), d) for w, d in zip(widths, dtypes)],
        compiler_params=_params("parallel"),
        name="inproj",
    )(x, lw["attn_g"], lw["w_in"], lw["w_v_t"], lw["q_g"], lw["ks_g"], lw["kw_g"], lw["mq_g"],
      consts["ones64"], consts["ones128"])


def _memkv_kernel(m_ref, g_ref, w_ref, wvt_ref, kg_ref, ones128_ref, kv_ref, kbf_ref, vt_ref):
    hb = _rms_rows(m_ref[...], g_ref[...]).astype(BF16)
    k = _seg_rmsnorm(_dot(hb, w_ref[:, 0:MEM_WIDTH]), ones128_ref[...], kg_ref[...], MEM_HEAD_DIM)
    kv_ref[:, 0:MEM_WIDTH] = k
    kv_ref[:, MEM_WIDTH:2 * MEM_WIDTH] = _dot(hb, w_ref[:, MEM_WIDTH:2 * MEM_WIDTH])
    kbf_ref[...] = k.astype(BF16)
    vt_ref[...] = _dot_nt(wvt_ref[...], hb).astype(BF16)


def _memkv(mem, lw, consts):
    b, mtok, _ = mem.shape
    blk = lambda r, c: pl.BlockSpec((None, r, c), lambda i: (i, 0, 0))
    return pl.pallas_call(
        _memkv_kernel,
        grid=(b,),
        in_specs=[blk(mtok, D_MODEL), _const_spec((1, D_MODEL)), _const_spec((D_MODEL, 2 * MEM_WIDTH)),
                  _const_spec((MEM_WIDTH, D_MODEL)), _const_spec((1, MEM_WIDTH)), _const_spec((MEM_WIDTH, MEM_WIDTH))],
        out_specs=[blk(mtok, 2 * MEM_WIDTH), blk(mtok, MEM_WIDTH), blk(MEM_WIDTH, mtok)],
        out_shape=[jax.ShapeDtypeStruct((b, mtok, 2 * MEM_WIDTH), F32), jax.ShapeDtypeStruct((b, mtok, MEM_WIDTH), BF16),
                   jax.ShapeDtypeStruct((b, MEM_WIDTH, mtok), BF16)],
        compiler_params=_params("parallel"),
        name="memkv",
    )(mem, lw["mem_g"], lw["w_mem_kv"], lw["w_mem_v_t"], lw["mk_g"], consts["ones128"])


def _compress_prompt_kernel(kv_ref, pe_ref, wk_ref, wv_ref, kcg_ref, ones64_ref, kc_ref, vct_ref):
    t = kv_ref.shape[0]
    nc = t // CMP_BLOCK
    pooled = jnp.sum(kv_ref[...].reshape(nc, CMP_BLOCK, 2 * KV_WIDTH), axis=1) * (1.0 / CMP_BLOCK)
    pooled = pooled + jnp.sum(pe_ref[...], axis=0, keepdims=True) * (1.0 / CMP_BLOCK)
    k = _dot_f32(pooled[:, 0:KV_WIDTH], wk_ref[...])
    v = _dot_f32(pooled[:, KV_WIDTH:2 * KV_WIDTH], wv_ref[...])
    kc_ref[...] = _seg_rmsnorm(k, ones64_ref[...], kcg_ref[...], HEAD_DIM).astype(BF16)
    vct_ref[...] = v.T.astype(BF16)


def _compress_prompt(nsakv, lw, consts):
    b, t, _ = nsakv.shape
    nc = t // CMP_BLOCK
    return pl.pallas_call(
        _compress_prompt_kernel,
        grid=(b,),
        in_specs=[pl.BlockSpec((None, t, 2 * KV_WIDTH), lambda i: (i, 0, 0)),
                  _const_spec((CMP_BLOCK, 2 * KV_WIDTH)), _const_spec((KV_WIDTH, KV_WIDTH)),
                  _const_spec((KV_WIDTH, KV_WIDTH)), _const_spec((1, KV_WIDTH)), _const_spec((KV_WIDTH, KV_WIDTH))],
        out_specs=[pl.BlockSpec((None, nc, KV_WIDTH), lambda i: (i, 0, 0)),
                   pl.BlockSpec((None, KV_WIDTH, nc), lambda i: (i, 0, 0))],
        out_shape=[jax.ShapeDtypeStruct((b, nc, KV_WIDTH), BF16), jax.ShapeDtypeStruct((b, KV_WIDTH, nc), BF16)],
        compiler_params=_params("parallel"),
        name="compress_prompt",
    )(nsakv, lw["cmp_pe"], lw["cmp_wk"], lw["cmp_wv"], lw["kc_g"], consts["ones64_kv"])


def _masked_softmax_rows(s, mask):
    maskf = jnp.where(mask, 1.0, 0.0)
    s = jnp.where(mask[None], s, NEG)
    m = jnp.max(s, axis=-1, keepdims=True)
    p = jnp.exp2(s - m) * maskf[None]
    den = jnp.maximum(jnp.sum(p, axis=-1, keepdims=True), 1e-30)
    return p * (1.0 / den)


def _stack_heads(q, k):
    return jnp.concatenate([q[:, (k * GROUP + g) * HEAD_DIM:(k * GROUP + g + 1) * HEAD_DIM] for g in range(GROUP)], axis=0)


def _transpose_tiles(x):
    r, c = x.shape
    n = V7X_LANES
    return jnp.concatenate(
        [jnp.concatenate([x[i * n:(i + 1) * n, j * n:(j + 1) * n].T for i in range(r // n)], axis=1) for j in range(c // n)],
        axis=0)


def _topk_select_t(score_t, n_sel):
    ns, tq = score_t.shape
    sub = V7X_SUBLANES
    tiles = [score_t[v * sub:(v + 1) * sub] for v in range(ns // sub)]
    row = _iota((sub, tq), 0)
    cnts = [jnp.zeros((sub, tq), F32) for _ in tiles]
    for j in range(ns):
        r = jnp.broadcast_to(score_t[j:j + 1, :], (sub, tq))
        for v, tile in enumerate(tiles):
            if v * sub > j:
                wins = r >= tile
            elif v * sub + sub - 1 < j:
                wins = r > tile
            else:
                wins = jnp.where(row + v * sub > j, jnp.where(r >= tile, 1.0, 0.0), jnp.where(r > tile, 1.0, 0.0)) > 0.5
            cnts[v] = cnts[v] + jnp.where(wins, 1.0, 0.0)
    return jnp.where(jnp.concatenate(cnts, axis=0) < n_sel, 1.0, 0.0)


def _with_ones_rows(v_t):
    return jnp.concatenate([v_t, jnp.ones((ONES_ROWS, v_t.shape[1]), BF16)], axis=0)


def _pv_and_sum(v_t, p):
    d = v_t.shape[0]
    r = _dot(_with_ones_rows(v_t), p.astype(BF16))
    return r[0:d], r[d:d + 1]


def _online_softmax_step(state, s_t, v_t):
    m, l, acc = state
    m_new = jnp.maximum(m, jnp.max(s_t, axis=0, keepdims=True))
    alpha = jnp.exp2(m - m_new)
    pv, psum = _pv_and_sum(v_t, jnp.exp2(s_t - m_new))
    return m_new, alpha * l + psum, alpha * acc + pv


def _merge_softmax_states(states):
    m = states[0][0]
    for st in states[1:]:
        m = jnp.maximum(m, st[0])
    l, acc = 0.0, 0.0
    for m_i, l_i, acc_i in states:
        w = jnp.exp2(m_i - m)
        l, acc = l + w * l_i, acc + w * acc_i
    return m, l, acc


def _prompt_attn_kernel(q_ref, kc_ref, vct_ref, kext_ref, kw_ref, vt_ref, poolc_ref, poolp_ref, memq_ref, mk_ref, mvt_ref,
                        ng_ref, pairt_ref, wpool_ref, pscale_ref,
                        onsa_ref, pooly_ref, memo_ref, pool_scr, part_scr):
    tq = q_ref.shape[0]
    nc = kc_ref.shape[0]
    ns = pairt_ref.shape[0]
    kc = vt_ref.shape[2]
    rows = GROUP * tq
    n_sel = min(TOP_K, ns)
    qi = pl.program_id(1)
    t0 = qi * tq
    q = q_ref[...]

    eye_tq = jnp.where(_iota((tq, tq), 0) == _iota((tq, tq), 1), 1.0, 0.0).astype(BF16)
    blk_t = _iota((ns, tq), 0)
    tpos_t = t0 + _iota((ns, tq), 1)
    cur_t = tpos_t // SEL_BLOCK
    valid_t = blk_t * SEL_BLOCK <= tpos_t
    forced_t = (blk_t == 0) | (blk_t == cur_t) | (blk_t == cur_t - 1)
    tok_of_row = lambda shape: t0 + (_iota(shape, 1) & (tq - 1))
    cmask = (_iota((nc, rows), 0) * CMP_BLOCK + (CMP_BLOCK - 1)) <= tok_of_row((nc, rows))
    cmaskf = jnp.where(cmask, 1.0, 0.0)
    ng_t = _transpose_tiles(ng_ref[...])
    nwb = WINDOW // tq + 1
    rel = _iota((tq, rows), 0) - (_iota((tq, rows), 1) & (tq - 1))
    win_first = jnp.where(rel > 0, 0.0, NEG)
    win_diag = jnp.where(rel <= 0, 0.0, NEG)
    init = (jnp.full((1, rows), NEG, F32), jnp.zeros((1, rows), F32), jnp.zeros((HEAD_DIM, rows), F32))

    def gate_row(k, br):
        idx = [N_NSA_BRANCH * (k * GROUP + g) + br for g in range(GROUP)]
        return jnp.concatenate([ng_t[i:i + 1, :] for i in idx], axis=1)

    q_exts = []
    for k in range(KV_HEADS):
        qk = _stack_heads(q, k)
        s_c = jnp.where(cmask, _dot_nt(kc_ref[:, k * HEAD_DIM:(k + 1) * HEAD_DIM], qk), NEG)
        p = jnp.exp2(s_c - jnp.max(s_c, axis=0, keepdims=True)) * cmaskf
        p = p * (1.0 / jnp.maximum(jnp.sum(p, axis=0, keepdims=True), 1e-30))
        o_cmp = _dot(vct_ref[k * HEAD_DIM:(k + 1) * HEAD_DIM, :], p.astype(BF16))
        imp_t = p[:, 0:tq]
        for g in range(1, GROUP):
            imp_t = imp_t + p[:, g * tq:(g + 1) * tq]
        imp2_t = _dot_split_rhs(pairt_ref[...], imp_t, 3)
        score_t = jnp.where(valid_t, jnp.where(forced_t, BIG, imp2_t), -BIG)
        sel_t = _topk_select_t(score_t, n_sel)
        sel = _dot_nt(eye_tq, sel_t.astype(BF16))
        selneg = jnp.where(sel > 0.5, 0.0, NEG).astype(BF16)
        if ns < HEAD_DIM:
            selneg = jnp.concatenate([selneg, jnp.zeros((tq, HEAD_DIM - ns), BF16)], axis=1)
        q_exts.append(jnp.concatenate([qk, jnp.concatenate([selneg] * GROUP, axis=0)], axis=1))
        s_blocks, v_blocks = [], []
        for i in range(nwb):
            kb = qi - (nwb - 1) + i
            inside = kb >= 0
            kbc = jnp.maximum(kb, 0)
            start = pl.multiple_of(kbc * tq, tq)
            k_b = kw_ref[pl.ds(start, tq), k * HEAD_DIM:(k + 1) * HEAD_DIM]
            s_b = _dot_nt(jnp.where(inside, k_b, jnp.zeros_like(k_b)), qk)
            if i == nwb - 1:
                s_b = s_b + win_diag
            elif i == 0:
                s_b = s_b + win_first
            v_pair = vt_ref[kbc // (kc // tq), KV_WIDTH + k * HEAD_DIM:KV_WIDTH + (k + 1) * HEAD_DIM, :]
            v_b = v_pair[:, 0:tq]
            for part in range(1, kc // tq):
                v_b = jnp.where(kbc % (kc // tq) == part, v_pair[:, part * tq:(part + 1) * tq], v_b)
            s_blocks.append(s_b)
            v_blocks.append(jnp.where(inside, v_b, jnp.zeros_like(v_b)))
        s_w = jnp.concatenate(s_blocks, axis=0)
        pv_w, l_w = _pv_and_sum(jnp.concatenate(v_blocks, axis=1), jnp.exp2(s_w - jnp.max(s_w, axis=0, keepdims=True)))
        o_win = pv_w * (1.0 / l_w)
        part_scr[k] = gate_row(k, 0) * o_cmp + gate_row(k, 2) * o_win

    mq = memq_ref[...]
    outs = []
    for h in range(MEM_HEADS):
        lo = h * MEM_HEAD_DIM
        s_m = _dot_nt(mk_ref[:, lo:lo + MEM_HEAD_DIM], mq[:, lo:lo + MEM_HEAD_DIM]) * MEM_HEAD_DIM ** -0.5
        pv_m, l_m = _pv_and_sum(mvt_ref[lo:lo + MEM_HEAD_DIM, :], jnp.exp(s_m - jnp.max(s_m, axis=0, keepdims=True)))
        outs.append(_transpose_tiles(pv_m * (1.0 / l_m)))
    memo_ref[...] = jnp.concatenate(outs, axis=1).astype(memo_ref.dtype)

    cur = poolc_ref[...]
    pool_scr[0:16, :] = jnp.where(qi > 0, poolp_ref[...], 0.0)
    pool_scr[16:16 + tq, :] = cur
    tpos1 = t0 + _iota((tq, POOL_GROUP_WIDTH), 0) + 1
    ys = []
    for gi, win in enumerate(POOL_WINDOWS):
        lo = gi * POOL_GROUP_WIDTH
        acc = pool_scr[16:16 + tq, lo:lo + POOL_GROUP_WIDTH]
        for sft in range(1, win):
            acc = acc + pool_scr[16 - sft:16 - sft + tq, lo:lo + POOL_GROUP_WIDTH]
        cnt = jnp.minimum(tpos1, win).astype(F32)
        d = acc / cnt - cur[:, lo:lo + POOL_GROUP_WIDTH]
        ys.append(_dot(d.astype(BF16), wpool_ref[gi]))
    pooly_ref[...] = (jnp.concatenate(ys, axis=1) * pscale_ref[...]).astype(pooly_ref.dtype)

    per = SEL_CHUNKS_PER_STEP
    span = per * kc
    n_full = t0 // span

    def sel_scores(k, sc):
        start = pl.multiple_of(sc * span, span)
        return _dot_nt(kext_ref[pl.ds(start, span), 2 * k * HEAD_DIM:(2 * k + 2) * HEAD_DIM], q_exts[k])

    def sel_values(k, sc):
        return jnp.concatenate([vt_ref[sc * per + j, k * HEAD_DIM:(k + 1) * HEAD_DIM, :] for j in range(per)], axis=1)

    def sel_step(sc, states):
        return tuple(_online_softmax_step(states[k], sel_scores(k, sc), sel_values(k, sc)) for k in range(KV_HEADS))

    states = lax.fori_loop(0, n_full, sel_step, (init,) * KV_HEADS)

    causal = (n_full * span + _iota((span, rows), 0)) <= tok_of_row((span, rows))
    head_out = []
    for k in range(KV_HEADS):
        _, l_k, acc_k = _online_softmax_step(states[k], jnp.where(causal, sel_scores(k, n_full), NEG),
                                             sel_values(k, n_full))
        o_k = part_scr[k] + gate_row(k, 1) * (acc_k * (1.0 / l_k))
        head_out += [o_k[:, g * tq:(g + 1) * tq] for g in range(GROUP)]
    pairs = [_transpose_tiles(jnp.concatenate(head_out[2 * j:2 * j + 2], axis=0)) for j in range(N_HEADS // 2)]
    onsa_ref[...] = jnp.concatenate(pairs, axis=1).astype(onsa_ref.dtype)


def _prompt_attn(q, kc, vct, kext, kw, vt, pool_u, memq, mk, mvt, ng, lw, consts):
    b, t, _ = q.shape
    nc = t // CMP_BLOCK
    ns = t // SEL_BLOCK
    tq = TQ
    chunk = vt.shape[3]
    m = mk.shape[1]
    span = SEL_CHUNKS_PER_STEP * chunk
    assert WINDOW % tq == 0 and chunk % tq == 0 and t % span == 0 and span % tq == 0
    assert ns <= HEAD_DIM and tq % V7X_LANES == 0 and tq & (tq - 1) == 0
    tok = lambda w: pl.BlockSpec((None, tq, w), lambda i, j: (i, j, 0))
    per_b = lambda r, w: pl.BlockSpec((None, r, w), lambda i, j: (i, 0, 0))
    prev_rows = pl.BlockSpec((None, 16, POOL_WIDTH), lambda i, j: (i, jnp.maximum(j * (tq // 16) - 1, 0), 0))
    out = jax.ShapeDtypeStruct((b, t, NSA_WIDTH), BF16)
    return pl.pallas_call(
        _prompt_attn_kernel,
        grid=(b, t // tq),
        in_specs=[tok(NSA_WIDTH), per_b(nc, KV_WIDTH), per_b(KV_WIDTH, nc), per_b(t, 2 * KV_WIDTH), per_b(t, KV_WIDTH),
                  pl.BlockSpec((None, t // chunk, 2 * KV_WIDTH, chunk), lambda i, j: (i, 0, 0, 0)),
                  tok(POOL_WIDTH), prev_rows, tok(MEM_WIDTH), per_b(m, MEM_WIDTH), per_b(MEM_WIDTH, m), tok(NSAG_PAD),
                  _const_spec((ns, nc)),
                  _const_spec((POOL_GROUPS, POOL_GROUP_WIDTH, POOL_GROUP_WIDTH)), _const_spec((1, POOL_WIDTH))],
        out_specs=[tok(NSA_WIDTH), tok(POOL_WIDTH), tok(MEM_WIDTH)],
        out_shape=[out, out, out],
        scratch_shapes=[pltpu.VMEM((16 + tq, POOL_WIDTH), F32), pltpu.VMEM((KV_HEADS, HEAD_DIM, GROUP * tq), F32)],
        compiler_params=_params("parallel", "arbitrary"),
        name="prompt_attn",
    )(q, kc, vct, kext, kw, vt, pool_u, pool_u, memq, mk, mvt, ng, consts["pair_t"], lw["w_pool"], lw["pool_scale"])


def _merge_ffn_kernel(x_ref, onsa_ref, pooly_ref, memo_ref, mg_ref, wn_ref, wp_ref, wm_ref, wo_ref, fg_ref,
                      wgu_ref, wd_ref, y_ref):
    h = (mg_ref[:, 0:D_MODEL] * _dot(onsa_ref[...].astype(BF16), wn_ref[...])
         + mg_ref[:, D_MODEL:2 * D_MODEL] * _dot(pooly_ref[...].astype(BF16), wp_ref[...])
         + mg_ref[:, 2 * D_MODEL:3 * D_MODEL] * _dot(memo_ref[...].astype(BF16), wm_ref[...]))
    x1 = x_ref[...] + _dot(h.astype(BF16), wo_ref[...])
    hn = _rms_rows(x1, fg_ref[...]).astype(BF16)
    acc = x1
    for j in range(D_FF // FF_CHUNK):
        lo = j * FF_CHUNK
        gate = _dot(hn, wgu_ref[:, lo:lo + FF_CHUNK])
        up = _dot(hn, wgu_ref[:, D_FF + lo:D_FF + lo + FF_CHUNK])
        act = gate * jax.nn.sigmoid(gate) * up
        acc = acc + _dot(act.astype(BF16), wd_ref[lo:lo + FF_CHUNK, :])
    y_ref[...] = acc


def _merge_ffn(x, onsa, pooly, memo, mg, lw):
    n = x.shape[0]
    tm = min(TM_PROJ, n)
    row = lambda w: pl.BlockSpec((tm, w), lambda i: (i, 0))
    return pl.pallas_call(
        _merge_ffn_kernel,
        grid=(n // tm,),
        in_specs=[row(D_MODEL), row(NSA_WIDTH), row(POOL_WIDTH), row(MEM_WIDTH), row(N_BRANCH * D_MODEL),
                  _const_spec((NSA_WIDTH, D_MODEL)), _const_spec((POOL_WIDTH, D_MODEL)), _const_spec((MEM_WIDTH, D_MODEL)),
                  _const_spec((D_MODEL, D_MODEL)), _const_spec((1, D_MODEL)),
                  _const_spec((D_MODEL, 2 * D_FF)), _const_spec((D_FF, D_MODEL))],
        out_specs=row(D_MODEL),
        out_shape=jax.ShapeDtypeStruct((n, D_MODEL), F32),
        compiler_params=_params("parallel"),
        name="merge_ffn",
    )(x, onsa, pooly, memo, mg, lw["w_up_nsa"], lw["w_up_pool"], lw["w_up_mem"], lw["w_out"], lw["ffn_g"],
      lw["w_gate_up"], lw["w_down"])


def _page_specs(npg, half, layer_of, batch_of, step_of):
    def spec(i):
        def index_map(*args):
            pt = args[-1]
            grid = args[:-1]
            return (layer_of(grid), pt[batch_of(grid), step_of(grid) * npg + i], half, 0)
        return pl.BlockSpec((None, None, 2 * KV_WIDTH, PAGE_SIZE), index_map)
    return [spec(i) for i in range(npg)]


def _compress_pages_kernel(pt_ref, *refs):
    del pt_ref
    npg = len(refs) - 7
    pages = refs[:npg]
    pe_ref, wkt_ref, wvt_ref, kcg_ref, ones64_ref, poolm_ref, out_ref = refs[npg:]
    x = jnp.concatenate([pg[...] for pg in pages], axis=1)
    pooled = _dot(x.astype(BF16), poolm_ref[...])
    pooled = pooled + jnp.sum(pe_ref[...], axis=1, keepdims=True) * (1.0 / CMP_BLOCK)
    k = _dot_f32(wkt_ref[...], pooled[0:KV_WIDTH])
    v = _dot_f32(wvt_ref[...], pooled[KV_WIDTH:2 * KV_WIDTH])
    ssq = _dot_split_rhs(ones64_ref[...], k * k, 2)
    out_ref[0:KV_WIDTH, :] = k * lax.rsqrt(ssq * (1.0 / HEAD_DIM) + EPS) * kcg_ref[...]
    out_ref[KV_WIDTH:2 * KV_WIDTH, :] = v


def _compress_pages(cache_t, page_table, w):
    depth = cache_t.shape[0]
    bd, n_pages = page_table.shape
    npg = min(PAGES_PER_STEP_CMP, n_pages)
    per_page = PAGE_SIZE // CMP_BLOCK
    per_step = npg * per_page
    ncs = n_pages * per_page
    lconst = lambda r, c: pl.BlockSpec((None, r, c), lambda l, b, s, pt: (l, 0, 0))
    const = lambda r, c: pl.BlockSpec((r, c), lambda l, b, s, pt: (0, 0))
    grid_spec = pltpu.PrefetchScalarGridSpec(
        num_scalar_prefetch=1,
        grid=(depth, bd, n_pages // npg),
        in_specs=_page_specs(npg, 0, lambda g: g[0], lambda g: g[1], lambda g: g[2])
        + [lconst(2 * KV_WIDTH, CMP_BLOCK), lconst(KV_WIDTH, KV_WIDTH), lconst(KV_WIDTH, KV_WIDTH), lconst(KV_WIDTH, 1),
           const(KV_WIDTH, KV_WIDTH), const(npg * PAGE_SIZE, per_step)],
        out_specs=pl.BlockSpec((None, None, 2 * KV_WIDTH, per_step), lambda l, b, s, pt: (l, b, 0, s)),
    )
    rows = jnp.arange(npg * PAGE_SIZE)[:, None] // CMP_BLOCK
    poolm = jnp.where(rows == jnp.arange(per_step)[None, :], 1.0 / CMP_BLOCK, 0.0).astype(BF16)
    return pl.pallas_call(
        _compress_pages_kernel,
        grid_spec=grid_spec,
        out_shape=jax.ShapeDtypeStruct((depth, bd, 2 * KV_WIDTH, ncs), F32),
        compiler_params=_params("parallel", "parallel", "arbitrary"),
        name="compress_pages",
    )(page_table, *([cache_t] * npg), w["cmp_pe_t"], w["cmp_wk_t"], w["cmp_wv_t"], w["kc_g_col"], w["ones64_kv"], poolm)


def _rows8(row):
    return jnp.broadcast_to(row, (V7X_SUBLANES, row.shape[1]))


def _sample_attn_kernel(pt_ref, *refs, past_len):
    del pt_ref
    npg = len(refs) - 24
    pages = refs[:npg]
    (q_ref, nsakv_ref, winkv_ref, ng_ref, poolu_ref, memq_ref, kcvc_ref, wincache_ref, poolstate_ref, memcache_ref,
     e_ref, pair_ref, fold_ref, unfold_ref, wpool_ref, pscale_ref,
     onsa_ref, pooly_ref, memo_ref,
     qbd_scr, bias_scr, m_scr, l_scr, acc_scr) = refs[npg:]
    step = pl.program_id(1)
    n_steps = pl.num_programs(1)
    ncs = kcvc_ref.shape[1]
    nsp = pair_ref.shape[1]
    keys_per_step = npg * PAGE_SIZE
    pos = past_len
    row8 = _iota((8, KV_WIDTH), 0)
    lane8 = _iota((8, KV_WIDTH), 1)
    head_lanes = (lane8 // HEAD_DIM) == (row8 // GROUP)

    @pl.when(step == 0)
    def _():
        q8 = jnp.where((_iota((8, NSA_WIDTH), 1) // HEAD_DIM) == _iota((8, NSA_WIDTH), 0),
                       _rows8(q_ref[...].astype(F32)), 0.0)
        qbd = _dot(q8.astype(BF16), fold_ref[...])
        qbd_scr[...] = qbd
        s = _dot(qbd.astype(BF16), kcvc_ref[0:KV_WIDTH, :].astype(BF16))
        cmask = (_iota((8, ncs), 1) * CMP_BLOCK + (CMP_BLOCK - 1)) <= pos
        p = _masked_softmax_rows(s[None], cmask)[0]
        ocmp = _dot_nt(p.astype(BF16), kcvc_ref[KV_WIDTH:2 * KV_WIDTH, :].astype(BF16))
        acc_scr[0] = jnp.where(head_lanes, ocmp, 0.0)
        eye = _iota((nsp, nsp), 0) == _iota((nsp, nsp), 1)
        ii = _iota((nsp, nsp), 1)
        jj = _iota((nsp, nsp), 0)
        cur = pos // SEL_BLOCK
        sel_rows = []
        for k in range(KV_HEADS):
            imp = jnp.sum(p[k * GROUP:(k + 1) * GROUP], axis=0, keepdims=True)
            imp2 = _dot_split_lhs(_rows8(imp), pair_ref[...], 3)[0:1]
            blk = _iota((1, nsp), 1)
            forced = (blk == 0) | (blk == cur) | (blk == cur - 1)
            valid = blk * SEL_BLOCK <= pos
            score = jnp.where(valid, jnp.where(forced, BIG, imp2), -BIG)
            score_i = jnp.broadcast_to(score, (nsp, nsp))
            score_j = jnp.sum(jnp.where(eye, score_i, 0.0), axis=1, keepdims=True)
            beats = jnp.where((score_j > score_i) | ((score_j == score_i) & (jj < ii)), 1.0, 0.0)
            cnt = jnp.sum(beats, axis=0, keepdims=True)
            cnt = cnt + jnp.where(score < BIG, 1.0, 0.0)
            sel_rows.append(jnp.where(cnt < TOP_K, 1.0, 0.0))
        sel8 = jnp.where(_iota((8, nsp), 0) < GROUP, _rows8(sel_rows[0]), _rows8(sel_rows[1])).astype(BF16)
        for st in range(bias_scr.shape[0]):
            selx = _dot(sel8, e_ref[:, st * keys_per_step:(st + 1) * keys_per_step])
            bias_scr[st] = jnp.where(selx > 0.5, 0.0, NEG)
        m_scr[...] = jnp.full(m_scr.shape, NEG, F32)
        l_scr[...] = jnp.zeros(l_scr.shape, F32)
        acc_scr[1] = jnp.zeros((8, KV_WIDTH), F32)

    qbd = qbd_scr[...]
    qbd_bf = qbd.astype(BF16)
    k_t = jnp.concatenate([pg[0:KV_WIDTH, :] for pg in pages], axis=1).astype(BF16)
    v_t = jnp.concatenate([pg[KV_WIDTH:2 * KV_WIDTH, :] for pg in pages], axis=1).astype(BF16)
    s_all = _dot(qbd_bf, k_t) + bias_scr[step]
    m_old = m_scr[...]
    m_new = jnp.maximum(m_old, jnp.max(s_all, axis=-1, keepdims=True))
    p_all = jnp.exp2(s_all - m_new) * jnp.where(s_all > 0.5 * NEG, 1.0, 0.0)
    alpha = jnp.exp2(m_old - m_new)
    l_scr[...] = alpha * l_scr[...] + jnp.sum(p_all, axis=-1, keepdims=True)
    acc_scr[1] = alpha * acc_scr[1] + _dot_nt(p_all.astype(BF16), v_t)
    m_scr[...] = m_new

    @pl.when(step == n_steps - 1)
    def _():
        ks_new = _rows8(nsakv_ref[:, 2 * KV_WIDTH:3 * KV_WIDTH])
        vs_new = _rows8(nsakv_ref[:, 3 * KV_WIDTH:4 * KV_WIDTH])
        s_new = jnp.sum(qbd * ks_new, axis=-1, keepdims=True)
        m_old = m_scr[...]
        m_fin = jnp.maximum(m_old, s_new)
        alpha = jnp.exp2(m_old - m_fin)
        p_new = jnp.exp2(s_new - m_fin)
        l_fin = alpha * l_scr[...] + p_new
        osel = (alpha * acc_scr[1] + p_new * vs_new) * (1.0 / l_fin)
        wb = wincache_ref.shape[1]
        kw_new = _rows8(winkv_ref[:, 0:KV_WIDTH])
        vw_new = _rows8(winkv_ref[:, KV_WIDTH:2 * KV_WIDTH])
        s_w = _dot(qbd_bf, wincache_ref[0:KV_WIDTH, :].astype(BF16))
        kpos = pos - wb + _iota((8, wb), 1)
        wmask = (kpos <= pos) & (kpos > pos - WINDOW)
        s_wn = jnp.sum(qbd * kw_new, axis=-1, keepdims=True)
        s_w = jnp.where(wmask, s_w, NEG)
        m_w = jnp.maximum(jnp.max(s_w, axis=-1, keepdims=True), s_wn)
        p_w = jnp.exp2(s_w - m_w) * jnp.where(wmask, 1.0, 0.0)
        p_wn = jnp.exp2(s_wn - m_w)
        l_w = jnp.sum(p_w, axis=-1, keepdims=True) + p_wn
        owin = (_dot_nt(p_w.astype(BF16), wincache_ref[KV_WIDTH:2 * KV_WIDTH, :].astype(BF16)) + p_wn * vw_new) * (1.0 / l_w)
        ng8 = _rows8(ng_ref[...])
        glane = _iota((8, NSAG_PAD), 1)
        grow = _iota((8, NSAG_PAD), 0)
        gate = lambda br: jnp.sum(jnp.where(glane == N_NSA_BRANCH * grow + br, ng8, 0.0), axis=-1, keepdims=True)
        o8 = jnp.where(head_lanes, gate(0) * acc_scr[0] + gate(1) * osel + gate(2) * owin, 0.0)
        o512 = _dot_split_lhs(o8, unfold_ref[...], 3)
        own = (_iota((8, NSA_WIDTH), 1) // HEAD_DIM) == _iota((8, NSA_WIDTH), 0)
        onsa_ref[...] = jnp.sum(jnp.where(own, o512, 0.0), axis=0, keepdims=True)
        mtok = memcache_ref.shape[0] // (2 * MEM_HEADS)
        outs = []
        for h in range(MEM_HEADS):
            k_h = memcache_ref[pl.ds(h, mtok, stride=2 * MEM_HEADS), :].astype(BF16)
            v_h = memcache_ref[pl.ds(MEM_HEADS + h, mtok, stride=2 * MEM_HEADS), :].astype(BF16)
            q_h = _rows8(memq_ref[:, h * MEM_HEAD_DIM:(h + 1) * MEM_HEAD_DIM].astype(F32)).astype(BF16)
            s_m = _dot_nt(q_h, k_h) * MEM_HEAD_DIM ** -0.5
            p_m = jnp.exp(s_m - jnp.max(s_m, axis=-1, keepdims=True))
            p_m = p_m * (1.0 / jnp.sum(p_m, axis=-1, keepdims=True))
            outs.append(_dot(p_m.astype(BF16), v_h)[0:1])
        memo_ref[...] = jnp.concatenate(outs, axis=1)
        u_new = poolu_ref[...]
        state = poolstate_ref[...]
        srow = _iota((POOL_BUF, POOL_GROUP_WIDTH), 0)
        ys = []
        for gi, win in enumerate(POOL_WINDOWS):
            lo = gi * POOL_GROUP_WIDTH
            u_g = u_new[:, lo:lo + POOL_GROUP_WIDTH]
            tail = jnp.sum(jnp.where(srow >= POOL_BUF - (win - 1), state[:, lo:lo + POOL_GROUP_WIDTH], 0.0),
                           axis=0, keepdims=True)
            d = (tail + u_g) / float(min(pos + 1, win)) - u_g
            ys.append(_dot(_rows8(d).astype(BF16), wpool_ref[gi])[0:1])
        pooly_ref[...] = jnp.concatenate(ys, axis=1) * pscale_ref[...]


def _sample_attn(l, q, nsakv, winkv, ng, pool_u, memq, kcvc_s, cache_t, wincache_t, poolstate, memcache, page_table,
                 lw, consts):
    bd, n_pages = page_table.shape
    npg = min(PAGES_PER_STEP_SEL, n_pages)
    n_steps = n_pages // npg
    past_len = n_pages * PAGE_SIZE
    ncs = past_len // CMP_BLOCK
    nsp = past_len // SEL_BLOCK
    wb = wincache_t.shape[3]
    mrows = memcache.shape[2]
    row = lambda w: pl.BlockSpec((None, 1, w), lambda b, s, pt: (b, 0, 0))
    const = lambda shape: pl.BlockSpec(shape, lambda b, s, pt: (0,) * len(shape))
    per_b = lambda r, c: pl.BlockSpec((None, None, r, c), lambda b, s, pt: (l, b, 0, 0))
    in_specs = (
        _page_specs(npg, 1, lambda g: l, lambda g: g[0], lambda g: g[1])
        + [row(NSA_WIDTH), row(4 * KV_WIDTH), row(2 * KV_WIDTH), row(NSAG_PAD), row(POOL_WIDTH), row(MEM_WIDTH),
           per_b(2 * KV_WIDTH, ncs), per_b(2 * KV_WIDTH, wb), per_b(POOL_BUF, POOL_WIDTH), per_b(mrows, MEM_HEAD_DIM),
           const((nsp, past_len)), const((ncs, nsp)), const((NSA_WIDTH, KV_WIDTH)), const((KV_WIDTH, NSA_WIDTH)),
           const((POOL_GROUPS, POOL_GROUP_WIDTH, POOL_GROUP_WIDTH)), const((1, POOL_WIDTH))])
    grid_spec = pltpu.PrefetchScalarGridSpec(
        num_scalar_prefetch=1,
        grid=(bd, n_steps),
        in_specs=in_specs,
        out_specs=[row(NSA_WIDTH), row(POOL_WIDTH), row(MEM_WIDTH)],
        scratch_shapes=[pltpu.VMEM((8, KV_WIDTH), F32), pltpu.VMEM((n_steps, 8, npg * PAGE_SIZE), F32),
                        pltpu.VMEM((8, 1), F32), pltpu.VMEM((8, 1), F32), pltpu.VMEM((2, 8, KV_WIDTH), F32)],
    )
    out = jax.ShapeDtypeStruct((bd, 1, NSA_WIDTH), F32)
    return pl.pallas_call(
        functools.partial(_sample_attn_kernel, past_len=past_len),
        grid_spec=grid_spec,
        out_shape=[out, out, out],
        compiler_params=_params("parallel", "arbitrary"),
        name="sample_attn",
    )(page_table, *([cache_t] * npg), q, nsakv, winkv, ng, pool_u, memq, kcvc_s, wincache_t, poolstate, memcache,
      consts["sel_expand_s"], consts["pair_s"], consts["fold"], consts["unfold"], lw["w_pool"], lw["pool_scale"])


def _block_diag_ones(n, seg):
    return (jnp.arange(n)[:, None] // seg == jnp.arange(n)[None, :] // seg).astype(BF16)


def _constants(t, past_len):
    nc, ns = t // CMP_BLOCK, t // SEL_BLOCK
    ratio = SEL_BLOCK // CMP_BLOCK
    pair_t = (jnp.arange(ns)[:, None] == jnp.arange(nc)[None, :] // ratio).astype(BF16)
    ncs, nsp = past_len // CMP_BLOCK, past_len // SEL_BLOCK
    sel_expand_s = (jnp.arange(nsp)[:, None] == (jnp.arange(past_len) // SEL_BLOCK)[None, :]).astype(BF16)
    pair_s = (jnp.arange(ncs)[:, None] // ratio == jnp.arange(nsp)[None, :]).astype(BF16)
    c = jnp.arange(NSA_WIDTH)
    fold_col = (c // (GROUP * HEAD_DIM)) * HEAD_DIM + c % HEAD_DIM
    fold = (fold_col[:, None] == jnp.arange(KV_WIDTH)[None, :]).astype(BF16)
    return {
        "ones64": _block_diag_ones(NSA_WIDTH, HEAD_DIM), "ones128": _block_diag_ones(MEM_WIDTH, MEM_HEAD_DIM),
        "ones64_kv": _block_diag_ones(KV_WIDTH, HEAD_DIM),
        "pair_t": pair_t,
        "sel_expand_s": sel_expand_s, "pair_s": pair_s, "fold": fold, "unfold": fold.T,
    }


def _tile_row(g, reps):
    return jnp.tile(g, reps)[None, :].astype(F32)


def kernel(x_prompt, x_sample, mem_prompt, cache_nsa_kv, cache_win_kv, state_pool, cache_mem_kv, page_table,
           attn_norm_g, w_in, nsa_q_g, nsa_kc_g, nsa_ks_g, nsa_kw_g, cmp_pe_k, cmp_pe_v, cmp_wk, cmp_wv,
           w_pool, pool_scale, mem_norm_g, w_mem_kv, mem_q_g, mem_k_g, w_up_nsa, w_up_pool, w_up_mem, w_out,
           ffn_norm_g, w_gate_up, w_down):
    depth = w_in.shape[0]
    b, t, _ = x_prompt.shape
    bd = x_sample.shape[0]
    n_pages = page_table.shape[1]
    past_len = n_pages * PAGE_SIZE
    mtok = mem_prompt.shape[1]
    wb = cache_win_kv.shape[2]
    assert x_sample.shape[1] == 1 and wb == WINDOW and past_len >= WINDOW and mtok == cache_mem_kv.shape[2]
    consts = _constants(t, past_len)

    nsag_lo = NSA_WIDTH + 6 * KV_WIDTH
    nsag_hi = nsag_lo + N_NSA_BRANCH * N_HEADS
    w_in_r = jnp.concatenate(
        [w_in[:, :, :nsag_lo], w_in[:, :, nsag_hi:], w_in[:, :, nsag_lo:nsag_hi],
         jnp.zeros((depth, D_MODEL, NSAG_PAD - N_NSA_BRANCH * N_HEADS), w_in.dtype)], axis=2).astype(BF16)
    vs_lo, vw_lo = NSA_WIDTH + 3 * KV_WIDTH, NSA_WIDTH + 5 * KV_WIDTH
    w_v_t = jnp.concatenate([w_in[:, :, vs_lo:vs_lo + KV_WIDTH], w_in[:, :, vw_lo:vw_lo + KV_WIDTH]],
                            axis=2).transpose(0, 2, 1).astype(BF16)
    w_mem_v_t = w_mem_kv[:, :, MEM_WIDTH:].transpose(0, 2, 1).astype(BF16)
    eye_kv = jnp.eye(KV_HEADS, dtype=F32)
    layers = []
    for l in range(depth):
        layers.append({
            "attn_g": attn_norm_g[l][None, :], "w_in": w_in_r[l], "w_v_t": w_v_t[l], "w_mem_v_t": w_mem_v_t[l],
            "q_g": _tile_row(nsa_q_g[l], N_HEADS), "ks_g": _tile_row(nsa_ks_g[l], KV_HEADS),
            "kw_g": _tile_row(nsa_kw_g[l], KV_HEADS), "mq_g": _tile_row(mem_q_g[l], MEM_HEADS),
            "kc_g": _tile_row(nsa_kc_g[l], KV_HEADS),
            "cmp_pe": jnp.concatenate([jnp.tile(cmp_pe_k[l], (1, KV_HEADS)), jnp.tile(cmp_pe_v[l], (1, KV_HEADS))], axis=1),
            "cmp_wk": jnp.kron(eye_kv, cmp_wk[l]), "cmp_wv": jnp.kron(eye_kv, cmp_wv[l]),
            "w_pool": w_pool[l].astype(BF16), "pool_scale": pool_scale[l][None, :],
            "mem_g": mem_norm_g[l][None, :], "w_mem_kv": w_mem_kv[l].astype(BF16), "mk_g": _tile_row(mem_k_g[l], MEM_HEADS),
            "w_up_nsa": w_up_nsa[l].astype(BF16), "w_up_pool": w_up_pool[l].astype(BF16),
            "w_up_mem": w_up_mem[l].astype(BF16), "w_out": w_out[l].astype(BF16),
            "ffn_g": ffn_norm_g[l][None, :], "w_gate_up": w_gate_up[l].astype(BF16), "w_down": w_down[l].astype(BF16),
        })
    stack = lambda key, f: jnp.stack([f(lw[key]) for lw in layers])
    cmp_w = {"cmp_pe_t": stack("cmp_pe", jnp.transpose), "cmp_wk_t": stack("cmp_wk", jnp.transpose),
             "cmp_wv_t": stack("cmp_wv", jnp.transpose), "kc_g_col": stack("kc_g", jnp.transpose),
             "ones64_kv": consts["ones64_kv"]}

    cache_t = cache_nsa_kv.transpose(0, 1, 3, 4, 5, 2).reshape(depth, cache_nsa_kv.shape[1], 4 * KV_WIDTH, PAGE_SIZE)
    wincache_t = cache_win_kv.transpose(0, 1, 3, 4, 5, 2).reshape(depth, bd, 2 * KV_WIDTH, wb)
    memcache = cache_mem_kv.reshape(depth, bd, cache_mem_kv.shape[2] * 2 * MEM_HEADS, MEM_HEAD_DIM)
    kcvc_s = _compress_pages(cache_t, page_table, cmp_w)

    xp = x_prompt.reshape(b * t, D_MODEL)
    xs = x_sample.reshape(bd, D_MODEL)
    nsa_p, nsa_s, win_p, win_s, pool_p, pool_s, mem_p = [], [], [], [], [], [], []
    for l, lw in enumerate(layers):
        q, nsakv, winkv, kext, kwbf, vt, pool_u, memq, mg, ng = _inproj(xp, lw, consts, t)
        mkv, mk, mvt = _memkv(mem_prompt, lw, consts)
        nsakv3 = nsakv.reshape(b, t, 4 * KV_WIDTH)
        kc, vct = _compress_prompt(nsakv3, lw, consts)
        pool3 = pool_u.reshape(b, t, POOL_WIDTH)
        by_b = lambda a: a.reshape(b, t, a.shape[-1])
        onsa, pooly, memo = _prompt_attn(
            by_b(q), kc, vct, by_b(kext), by_b(kwbf), vt.reshape(b, -1, vt.shape[1], vt.shape[2]), pool3, by_b(memq),
            mk, mvt, by_b(ng), lw, consts)
        xp = _merge_ffn(xp, onsa.reshape(b * t, NSA_WIDTH), pooly.reshape(b * t, POOL_WIDTH),
                        memo.reshape(b * t, MEM_WIDTH), mg, lw)
        nsa_p.append(nsakv3.reshape(b, t, 4, KV_HEADS, HEAD_DIM))
        win_p.append(winkv.reshape(b, t, 2, KV_HEADS, HEAD_DIM)[:, t - min(WINDOW, t):])
        pool_p.append(pool3[:, t - POOL_BUF:])
        mem_p.append(mkv.reshape(b, mtok, 2, MEM_HEADS, MEM_HEAD_DIM))
        q, nsakv, winkv, _, _, _, pool_u, memq, mg, ng = _inproj(xs, lw, consts, bd)
        r3 = lambda a: a.reshape(bd, 1, a.shape[-1])
        onsa, pooly, memo = _sample_attn(l, r3(q), r3(nsakv), r3(winkv), r3(ng), r3(pool_u), r3(memq), kcvc_s, cache_t,
                                         wincache_t, state_pool, memcache, page_table, lw, consts)
        xs = _merge_ffn(xs, onsa.reshape(bd, NSA_WIDTH), pooly.reshape(bd, POOL_WIDTH), memo.reshape(bd, MEM_WIDTH), mg, lw)
        nsa_s.append(nsakv.reshape(bd, 1, 4, KV_HEADS, HEAD_DIM))
        win_s.append(jnp.concatenate([cache_win_kv[l][:, 1:], winkv.reshape(bd, 1, 2, KV_HEADS, HEAD_DIM)], axis=1))
        pool_s.append(jnp.concatenate([state_pool[l][:, 1:], pool_u.reshape(bd, 1, POOL_WIDTH)], axis=1))
    return (xp.reshape(b, t, D_MODEL), xs.reshape(bd, 1, D_MODEL), jnp.stack(nsa_p), jnp.stack(nsa_s),
            jnp.stack(win_p), jnp.stack(win_s), jnp.stack(pool_p), jnp.stack(pool_s), jnp.stack(mem_p))
```

```python
import functools

import jax
import jax.numpy as jnp
from jax import lax
from jax.experimental import pallas as pl
from jax.experimental.pallas import tpu as pltpu

F32 = jnp.float32
BF16 = jnp.bfloat16

D_MODEL = 1024
PAGE_SIZE = 128
N_HEADS = 8
HEAD_DIM = 64
KV_HEADS = 2
GROUP = N_HEADS // KV_HEADS
NSA_WIDTH = N_HEADS * HEAD_DIM
KV_WIDTH = KV_HEADS * HEAD_DIM
CMP_BLOCK = 32
SEL_BLOCK = 64
TOP_K = 16
WINDOW = 512
N_NSA_BRANCH = 3
POOL_WIDTH = 512
POOL_GROUPS = 4
POOL_GROUP_WIDTH = POOL_WIDTH // POOL_GROUPS
POOL_WINDOWS = (2, 4, 8, 16)
POOL_BUF = 15
MEM_HEADS = 4
MEM_HEAD_DIM = 128
MEM_WIDTH = MEM_HEADS * MEM_HEAD_DIM
N_BRANCH = 3
D_FF = ((8 * D_MODEL // 3 + 255) // 256) * 256
EPS = 1e-6
LOG2_E = 1.4426950408889634
NEG = -1e30
BIG = 1e9

V7X_LANES = 128
V7X_SUBLANES = 8
ONES_ROWS = 2 * V7X_SUBLANES
V7X_VMEM_LIMIT_BYTES = 56 * 1024 * 1024

C_Q = 0
C_NSAKV = C_Q + NSA_WIDTH
C_WINKV = C_NSAKV + 4 * KV_WIDTH
C_POOL = C_WINKV + 2 * KV_WIDTH
C_MEMQ = C_POOL + POOL_WIDTH
C_MERGE = C_MEMQ + MEM_WIDTH
C_NSAG = C_MERGE + N_BRANCH * D_MODEL
NSAG_PAD = V7X_LANES
PROJ_PAD = C_NSAG + NSAG_PAD

TM_PROJ = 256
TQ = 256
SEL_CHUNKS_PER_STEP = 2
FF_CHUNK = D_FF // 2
PAGES_PER_STEP_CMP = 32
PAGES_PER_STEP_SEL = 32


def _dot(a, b):
    return jnp.dot(a, b, preferred_element_type=F32)


def _dot_nt(a, b):
    return lax.dot_general(a, b, (((1,), (1,)), ((), ())), preferred_element_type=F32)


def _dot_f32(a, b):
    return jnp.dot(a, b, preferred_element_type=F32, precision=lax.Precision.HIGHEST)


def _split_bf16(a, n):
    parts, r = [], a
    for _ in range(n):
        p = r.astype(BF16)
        parts.append(p)
        r = r - p.astype(F32)
    return parts


def _dot_split_lhs(a, b, n):
    return sum(_dot(p, b) for p in _split_bf16(a, n))


def _dot_split_rhs(a, b, n):
    return sum(_dot(a, p) for p in _split_bf16(b, n))


def _iota(shape, dim):
    return lax.broadcasted_iota(jnp.int32, shape, dim)


def _rms_rows(x, g):
    return x * lax.rsqrt(jnp.mean(x * x, axis=-1, keepdims=True) + EPS) * g


def _seg_rmsnorm(v, ones_bd, gain, seg):
    ssq = _dot_split_lhs(v * v, ones_bd, 2)
    return v * lax.rsqrt(ssq * (1.0 / seg) + EPS) * gain


def _const_spec(shape):
    return pl.BlockSpec(shape, lambda *_: (0,) * len(shape), pipeline_mode=pl.Buffered(1))


def _layer_spec(l, shape):
    return pl.BlockSpec((None,) + tuple(shape), lambda *_: (l,) + (0,) * len(shape), pipeline_mode=pl.Buffered(1))


def _params(*sem):
    return pltpu.CompilerParams(dimension_semantics=sem, vmem_limit_bytes=V7X_VMEM_LIMIT_BYTES)


def _inproj_kernel(x_ref, g_ref, w_ref, qg_ref, ksg_ref, kwg_ref, mqg_ref, ones64_ref, ones128_ref,
                   q_ref, nsakv_ref, winkv_ref, kext_ref, kwbf_ref, vt_ref, pool_ref, memq_ref, mg_ref, ng_ref,
                   *, seq_len):
    tm = x_ref.shape[0]
    hb = _rms_rows(x_ref[...], g_ref[...]).astype(BF16)

    def proj(lo, width):
        return _dot_nt(hb, w_ref[lo:lo + width, :])

    ones_kv = ones64_ref[0:KV_WIDTH, 0:KV_WIDTH]
    q = _seg_rmsnorm(proj(C_Q, NSA_WIDTH), ones64_ref[...], qg_ref[...], HEAD_DIM)
    q_ref[...] = (q * (HEAD_DIM ** -0.5 * LOG2_E)).astype(BF16)
    kcvc = proj(C_NSAKV, 2 * KV_WIDTH)
    ks = _seg_rmsnorm(proj(C_NSAKV + 2 * KV_WIDTH, KV_WIDTH), ones_kv, ksg_ref[...], HEAD_DIM)
    vs = proj(C_NSAKV + 3 * KV_WIDTH, KV_WIDTH)
    kw = _seg_rmsnorm(proj(C_WINKV, KV_WIDTH), ones_kv, kwg_ref[...], HEAD_DIM)
    vw = proj(C_WINKV + KV_WIDTH, KV_WIDTH)
    nsakv_ref[:, 0:2 * KV_WIDTH] = kcvc
    nsakv_ref[:, 2 * KV_WIDTH:3 * KV_WIDTH] = ks
    nsakv_ref[:, 3 * KV_WIDTH:4 * KV_WIDTH] = vs
    winkv_ref[:, 0:KV_WIDTH] = kw
    winkv_ref[:, KV_WIDTH:2 * KV_WIDTH] = vw
    t_seq = (pl.program_id(0) * tm + _iota((tm, HEAD_DIM), 0)) % seq_len
    onehot = jnp.where(t_seq // SEL_BLOCK == _iota((tm, HEAD_DIM), 1), 1.0, 0.0).astype(BF16)
    for k in range(KV_HEADS):
        kext_ref[:, 2 * k * HEAD_DIM:(2 * k + 1) * HEAD_DIM] = ks[:, k * HEAD_DIM:(k + 1) * HEAD_DIM].astype(BF16)
        kext_ref[:, (2 * k + 1) * HEAD_DIM:(2 * k + 2) * HEAD_DIM] = onehot
    kwbf_ref[...] = kw.astype(BF16)
    vs_lo, vw_lo = C_NSAKV + 3 * KV_WIDTH, C_WINKV + KV_WIDTH
    vt_ref[0:KV_WIDTH, :] = _dot_nt(w_ref[vs_lo:vs_lo + KV_WIDTH, :], hb).astype(BF16)
    vt_ref[KV_WIDTH:2 * KV_WIDTH, :] = _dot_nt(w_ref[vw_lo:vw_lo + KV_WIDTH, :], hb).astype(BF16)
    pool_ref[...] = proj(C_POOL, POOL_WIDTH)
    mq = _seg_rmsnorm(proj(C_MEMQ, MEM_WIDTH), ones128_ref[...], mqg_ref[...], MEM_HEAD_DIM)
    memq_ref[...] = mq.astype(BF16)
    for j in range(N_BRANCH):
        mg_ref[:, j * D_MODEL:(j + 1) * D_MODEL] = jax.nn.sigmoid(proj(C_MERGE + j * D_MODEL, D_MODEL))
    ng_ref[...] = jax.nn.sigmoid(proj(C_NSAG, NSAG_PAD))


def _inproj(x, l, w, consts, seq_len):
    n = x.shape[0]
    lay = functools.partial(_layer_spec, l)
    tm = min(TM_PROJ, n)
    row = lambda w: pl.BlockSpec((tm, w), lambda i: (i, 0))
    widths = (NSA_WIDTH, 4 * KV_WIDTH, 2 * KV_WIDTH, 2 * KV_WIDTH, KV_WIDTH, None, POOL_WIDTH, MEM_WIDTH,
              N_BRANCH * D_MODEL, NSAG_PAD)
    dtypes = (BF16, F32, F32, BF16, BF16, BF16, F32, BF16, F32, F32)
    vt_spec = pl.BlockSpec((None, 2 * KV_WIDTH, tm), lambda i: (i, 0, 0))
    vt_shape = jax.ShapeDtypeStruct((n // tm, 2 * KV_WIDTH, tm), BF16)
    return pl.pallas_call(
        functools.partial(_inproj_kernel, seq_len=seq_len),
        grid=(n // tm,),
        in_specs=[row(D_MODEL), lay((1, D_MODEL)), lay((PROJ_PAD, D_MODEL)),
                  lay((1, NSA_WIDTH)), lay((1, KV_WIDTH)), lay((1, KV_WIDTH)), lay((1, MEM_WIDTH)),
                  _const_spec((NSA_WIDTH, NSA_WIDTH)), _const_spec((MEM_WIDTH, MEM_WIDTH))],
        out_specs=[vt_spec if wd is None else row(wd) for wd in widths],
        out_shape=[vt_shape if wd is None else jax.ShapeDtypeStruct((n, wd), d) for wd, d in zip(widths, dtypes)],
        compiler_params=_params("parallel"),
        name="inproj",
    )(x, w["attn_g"], w["w_in_t"], w["q_g"], w["ks_g"], w["kw_g"], w["mq_g"], consts["ones64"], consts["ones128"])


def _memkv_kernel(m_ref, g_ref, w_ref, wvt_ref, kg_ref, ones128_ref, kv_ref, kbf_ref, vt_ref):
    hb = _rms_rows(m_ref[...], g_ref[...]).astype(BF16)
    k = _seg_rmsnorm(_dot(hb, w_ref[:, 0:MEM_WIDTH]), ones128_ref[...], kg_ref[...], MEM_HEAD_DIM)
    kv_ref[:, 0:MEM_WIDTH] = k
    kv_ref[:, MEM_WIDTH:2 * MEM_WIDTH] = _dot(hb, w_ref[:, MEM_WIDTH:2 * MEM_WIDTH])
    kbf_ref[...] = k.astype(BF16)
    vt_ref[...] = _dot_nt(wvt_ref[...], hb).astype(BF16)


def _memkv(mem, l, w, consts):
    b, mtok, _ = mem.shape
    lay = functools.partial(_layer_spec, l)
    blk = lambda r, c: pl.BlockSpec((None, r, c), lambda i: (i, 0, 0))
    return pl.pallas_call(
        _memkv_kernel,
        grid=(b,),
        in_specs=[blk(mtok, D_MODEL), lay((1, D_MODEL)), lay((D_MODEL, 2 * MEM_WIDTH)),
                  lay((MEM_WIDTH, D_MODEL)), lay((1, MEM_WIDTH)), _const_spec((MEM_WIDTH, MEM_WIDTH))],
        out_specs=[blk(mtok, 2 * MEM_WIDTH), blk(mtok, MEM_WIDTH), blk(MEM_WIDTH, mtok)],
        out_shape=[jax.ShapeDtypeStruct((b, mtok, 2 * MEM_WIDTH), F32), jax.ShapeDtypeStruct((b, mtok, MEM_WIDTH), BF16),
                   jax.ShapeDtypeStruct((b, MEM_WIDTH, mtok), BF16)],
        compiler_params=_params("parallel"),
        name="memkv",
    )(mem, w["mem_g"], w["w_mem_kv"], w["w_mem_v_t"], w["mk_g"], consts["ones128"])


def _compress_prompt_kernel(kv_ref, pe_ref, wk_ref, wv_ref, kcg_ref, ones64_ref, kc_ref, vct_ref):
    t = kv_ref.shape[0]
    nc = t // CMP_BLOCK
    pooled = jnp.sum(kv_ref[...].reshape(nc, CMP_BLOCK, 2 * KV_WIDTH), axis=1) * (1.0 / CMP_BLOCK)
    pooled = pooled + jnp.sum(pe_ref[...], axis=0, keepdims=True) * (1.0 / CMP_BLOCK)
    k = _dot_f32(pooled[:, 0:KV_WIDTH], wk_ref[...])
    v = _dot_f32(pooled[:, KV_WIDTH:2 * KV_WIDTH], wv_ref[...])
    kc_ref[...] = _seg_rmsnorm(k, ones64_ref[...], kcg_ref[...], HEAD_DIM).astype(BF16)
    vct_ref[...] = v.T.astype(BF16)


def _compress_prompt(nsakv, l, w, consts):
    b, t, _ = nsakv.shape
    nc = t // CMP_BLOCK
    return pl.pallas_call(
        _compress_prompt_kernel,
        grid=(b,),
        in_specs=[pl.BlockSpec((None, t, 2 * KV_WIDTH), lambda i: (i, 0, 0)),
                  _layer_spec(l, (CMP_BLOCK, 2 * KV_WIDTH)), _layer_spec(l, (KV_WIDTH, KV_WIDTH)),
                  _layer_spec(l, (KV_WIDTH, KV_WIDTH)), _layer_spec(l, (1, KV_WIDTH)), _const_spec((KV_WIDTH, KV_WIDTH))],
        out_specs=[pl.BlockSpec((None, nc, KV_WIDTH), lambda i: (i, 0, 0)),
                   pl.BlockSpec((None, KV_WIDTH, nc), lambda i: (i, 0, 0))],
        out_shape=[jax.ShapeDtypeStruct((b, nc, KV_WIDTH), BF16), jax.ShapeDtypeStruct((b, KV_WIDTH, nc), BF16)],
        compiler_params=_params("parallel"),
        name="compress_prompt",
    )(nsakv, w["cmp_pe"], w["cmp_wk"], w["cmp_wv"], w["kc_g"], consts["ones64_kv"])


def _masked_softmax_rows(s, mask):
    maskf = jnp.where(mask, 1.0, 0.0)
    s = jnp.where(mask[None], s, NEG)
    m = jnp.max(s, axis=-1, keepdims=True)
    p = jnp.exp2(s - m) * maskf[None]
    den = jnp.maximum(jnp.sum(p, axis=-1, keepdims=True), 1e-30)
    return p * (1.0 / den)


def _stack_heads(q, k):
    return jnp.concatenate([q[:, (k * GROUP + g) * HEAD_DIM:(k * GROUP + g + 1) * HEAD_DIM] for g in range(GROUP)], axis=0)


def _transpose_tiles(x):
    r, c = x.shape
    n = V7X_LANES
    return jnp.concatenate(
        [jnp.concatenate([x[i * n:(i + 1) * n, j * n:(j + 1) * n].T for i in range(r // n)], axis=1) for j in range(c // n)],
        axis=0)


def _topk_select_t(score_t, n_sel):
    ns, tq = score_t.shape
    sub = V7X_SUBLANES
    tiles = [score_t[v * sub:(v + 1) * sub] for v in range(ns // sub)]
    row = _iota((sub, tq), 0)
    cnts = [jnp.zeros((sub, tq), F32) for _ in tiles]
    for j in range(ns):
        r = jnp.broadcast_to(score_t[j:j + 1, :], (sub, tq))
        for v, tile in enumerate(tiles):
            if v * sub > j:
                wins = r >= tile
            elif v * sub + sub - 1 < j:
                wins = r > tile
            else:
                wins = jnp.where(row + v * sub > j, jnp.where(r >= tile, 1.0, 0.0), jnp.where(r > tile, 1.0, 0.0)) > 0.5
            cnts[v] = cnts[v] + jnp.where(wins, 1.0, 0.0)
    return jnp.where(jnp.concatenate(cnts, axis=0) < n_sel, 1.0, 0.0)


def _with_ones_rows(v_t):
    return jnp.concatenate([v_t, jnp.ones((ONES_ROWS, v_t.shape[1]), BF16)], axis=0)


def _pv_and_sum(v_t, p):
    d = v_t.shape[0]
    r = _dot(_with_ones_rows(v_t), p.astype(BF16))
    return r[0:d], r[d:d + 1]


def _online_softmax_step(state, s_t, v_t):
    m, l, acc = state
    m_new = jnp.maximum(m, jnp.max(s_t, axis=0, keepdims=True))
    alpha = jnp.exp2(m - m_new)
    pv, psum = _pv_and_sum(v_t, jnp.exp2(s_t - m_new))
    return m_new, alpha * l + psum, alpha * acc + pv


def _merge_softmax_states(states):
    m = states[0][0]
    for st in states[1:]:
        m = jnp.maximum(m, st[0])
    l, acc = 0.0, 0.0
    for m_i, l_i, acc_i in states:
        w = jnp.exp2(m_i - m)
        l, acc = l + w * l_i, acc + w * acc_i
    return m, l, acc


def _prompt_attn_kernel(q_ref, kc_ref, vct_ref, kext_ref, kw_ref, vt_ref, poolc_ref, poolp_ref, memq_ref, mk_ref, mvt_ref,
                        ng_ref, pairt_ref, wpool_ref, pscale_ref,
                        onsa_ref, pooly_ref, memo_ref, pool_scr, part_scr):
    tq = q_ref.shape[0]
    nc = kc_ref.shape[0]
    ns = pairt_ref.shape[0]
    kc = vt_ref.shape[2]
    rows = GROUP * tq
    n_sel = min(TOP_K, ns)
    qi = pl.program_id(1)
    t0 = qi * tq
    q = q_ref[...]

    eye_tq = jnp.where(_iota((tq, tq), 0) == _iota((tq, tq), 1), 1.0, 0.0).astype(BF16)
    blk_t = _iota((ns, tq), 0)
    tpos_t = t0 + _iota((ns, tq), 1)
    cur_t = tpos_t // SEL_BLOCK
    valid_t = blk_t * SEL_BLOCK <= tpos_t
    forced_t = (blk_t == 0) | (blk_t == cur_t) | (blk_t == cur_t - 1)
    tok_of_row = lambda shape: t0 + (_iota(shape, 1) & (tq - 1))
    cmask = (_iota((nc, rows), 0) * CMP_BLOCK + (CMP_BLOCK - 1)) <= tok_of_row((nc, rows))
    cmaskf = jnp.where(cmask, 1.0, 0.0)
    ng_t = _transpose_tiles(ng_ref[...])
    nwb = WINDOW // tq + 1
    rel = _iota((tq, rows), 0) - (_iota((tq, rows), 1) & (tq - 1))
    win_first = jnp.where(rel > 0, 0.0, NEG)
    win_diag = jnp.where(rel <= 0, 0.0, NEG)
    init = (jnp.full((1, rows), NEG, F32), jnp.zeros((1, rows), F32), jnp.zeros((HEAD_DIM, rows), F32))

    def gate_row(k, br):
        idx = [N_NSA_BRANCH * (k * GROUP + g) + br for g in range(GROUP)]
        return jnp.concatenate([ng_t[i:i + 1, :] for i in idx], axis=1)

    q_exts = []
    for k in range(KV_HEADS):
        qk = _stack_heads(q, k)
        s_c = jnp.where(cmask, _dot_nt(kc_ref[:, k * HEAD_DIM:(k + 1) * HEAD_DIM], qk), NEG)
        p = jnp.exp2(s_c - jnp.max(s_c, axis=0, keepdims=True)) * cmaskf
        p = p * (1.0 / jnp.maximum(jnp.sum(p, axis=0, keepdims=True), 1e-30))
        o_cmp = _dot(vct_ref[k * HEAD_DIM:(k + 1) * HEAD_DIM, :], p.astype(BF16))
        imp_t = p[:, 0:tq]
        for g in range(1, GROUP):
            imp_t = imp_t + p[:, g * tq:(g + 1) * tq]
        imp2_t = _dot_split_rhs(pairt_ref[...], imp_t, 3)
        score_t = jnp.where(valid_t, jnp.where(forced_t, BIG, imp2_t), -BIG)
        sel_t = _topk_select_t(score_t, n_sel)
        sel = _dot_nt(eye_tq, sel_t.astype(BF16))
        selneg = jnp.where(sel > 0.5, 0.0, NEG).astype(BF16)
        if ns < HEAD_DIM:
            selneg = jnp.concatenate([selneg, jnp.zeros((tq, HEAD_DIM - ns), BF16)], axis=1)
        q_exts.append(jnp.concatenate([qk, jnp.concatenate([selneg] * GROUP, axis=0)], axis=1))
        s_blocks, v_blocks = [], []
        for i in range(nwb):
            kb = qi - (nwb - 1) + i
            inside = kb >= 0
            kbc = jnp.maximum(kb, 0)
            start = pl.multiple_of(kbc * tq, tq)
            k_b = kw_ref[pl.ds(start, tq), k * HEAD_DIM:(k + 1) * HEAD_DIM]
            s_b = _dot_nt(jnp.where(inside, k_b, jnp.zeros_like(k_b)), qk)
            if i == nwb - 1:
                s_b = s_b + win_diag
            elif i == 0:
                s_b = s_b + win_first
            v_pair = vt_ref[kbc // (kc // tq), KV_WIDTH + k * HEAD_DIM:KV_WIDTH + (k + 1) * HEAD_DIM, :]
            v_b = v_pair[:, 0:tq]
            for part in range(1, kc // tq):
                v_b = jnp.where(kbc % (kc // tq) == part, v_pair[:, part * tq:(part + 1) * tq], v_b)
            s_blocks.append(s_b)
            v_blocks.append(jnp.where(inside, v_b, jnp.zeros_like(v_b)))
        s_w = jnp.concatenate(s_blocks, axis=0)
        pv_w, l_w = _pv_and_sum(jnp.concatenate(v_blocks, axis=1), jnp.exp2(s_w - jnp.max(s_w, axis=0, keepdims=True)))
        o_win = pv_w * (1.0 / l_w)
        part_scr[k] = gate_row(k, 0) * o_cmp + gate_row(k, 2) * o_win

    mq = memq_ref[...]
    outs = []
    for h in range(MEM_HEADS):
        lo = h * MEM_HEAD_DIM
        s_m = _dot_nt(mk_ref[:, lo:lo + MEM_HEAD_DIM], mq[:, lo:lo + MEM_HEAD_DIM]) * MEM_HEAD_DIM ** -0.5
        pv_m, l_m = _pv_and_sum(mvt_ref[lo:lo + MEM_HEAD_DIM, :], jnp.exp(s_m - jnp.max(s_m, axis=0, keepdims=True)))
        outs.append(_transpose_tiles(pv_m * (1.0 / l_m)))
    memo_ref[...] = jnp.concatenate(outs, axis=1).astype(memo_ref.dtype)

    cur = poolc_ref[...]
    pool_scr[0:16, :] = jnp.where(qi > 0, poolp_ref[...], 0.0)
    pool_scr[16:16 + tq, :] = cur
    tpos1 = t0 + _iota((tq, POOL_GROUP_WIDTH), 0) + 1
    ys = []
    for gi, win in enumerate(POOL_WINDOWS):
        lo = gi * POOL_GROUP_WIDTH
        acc = pool_scr[16:16 + tq, lo:lo + POOL_GROUP_WIDTH]
        for sft in range(1, win):
            acc = acc + pool_scr[16 - sft:16 - sft + tq, lo:lo + POOL_GROUP_WIDTH]
        cnt = jnp.minimum(tpos1, win).astype(F32)
        d = acc / cnt - cur[:, lo:lo + POOL_GROUP_WIDTH]
        ys.append(_dot(d.astype(BF16), wpool_ref[gi]))
    pooly_ref[...] = (jnp.concatenate(ys, axis=1) * pscale_ref[...]).astype(pooly_ref.dtype)

    per = SEL_CHUNKS_PER_STEP
    span = per * kc
    n_full = t0 // span

    def sel_scores(k, sc):
        start = pl.multiple_of(sc * span, span)
        return _dot_nt(kext_ref[pl.ds(start, span), 2 * k * HEAD_DIM:(2 * k + 2) * HEAD_DIM], q_exts[k])

    def sel_values(k, sc):
        return jnp.concatenate([vt_ref[sc * per + j, k * HEAD_DIM:(k + 1) * HEAD_DIM, :] for j in range(per)], axis=1)

    def sel_step(sc, states):
        return tuple(_online_softmax_step(states[k], sel_scores(k, sc), sel_values(k, sc)) for k in range(KV_HEADS))

    states = lax.fori_loop(0, n_full, sel_step, (init,) * KV_HEADS)

    causal = (n_full * span + _iota((span, rows), 0)) <= tok_of_row((span, rows))
    head_out = []
    for k in range(KV_HEADS):
        _, l_k, acc_k = _online_softmax_step(states[k], jnp.where(causal, sel_scores(k, n_full), NEG),
                                             sel_values(k, n_full))
        o_k = part_scr[k] + gate_row(k, 1) * (acc_k * (1.0 / l_k))
        head_out += [o_k[:, g * tq:(g + 1) * tq] for g in range(GROUP)]
    pairs = [_transpose_tiles(jnp.concatenate(head_out[2 * j:2 * j + 2], axis=0)) for j in range(N_HEADS // 2)]
    onsa_ref[...] = jnp.concatenate(pairs, axis=1).astype(onsa_ref.dtype)


def _prompt_attn(q, kc, vct, kext, kw, vt, pool_u, memq, mk, mvt, ng, l, w, consts):
    b, t, _ = q.shape
    nc = t // CMP_BLOCK
    ns = t // SEL_BLOCK
    tq = TQ
    chunk = vt.shape[3]
    m = mk.shape[1]
    span = SEL_CHUNKS_PER_STEP * chunk
    assert WINDOW % tq == 0 and chunk % tq == 0 and t % span == 0 and span % tq == 0
    assert ns <= HEAD_DIM and tq % V7X_LANES == 0 and tq & (tq - 1) == 0
    tok = lambda w: pl.BlockSpec((None, tq, w), lambda i, j: (i, j, 0))
    per_b = lambda r, w: pl.BlockSpec((None, r, w), lambda i, j: (i, 0, 0))
    prev_rows = pl.BlockSpec((None, 16, POOL_WIDTH), lambda i, j: (i, jnp.maximum(j * (tq // 16) - 1, 0), 0))
    out = jax.ShapeDtypeStruct((b, t, NSA_WIDTH), BF16)
    return pl.pallas_call(
        _prompt_attn_kernel,
        grid=(b, t // tq),
        in_specs=[tok(NSA_WIDTH), per_b(nc, KV_WIDTH), per_b(KV_WIDTH, nc), per_b(t, 2 * KV_WIDTH), per_b(t, KV_WIDTH),
                  pl.BlockSpec((None, t // chunk, 2 * KV_WIDTH, chunk), lambda i, j: (i, 0, 0, 0)),
                  tok(POOL_WIDTH), prev_rows, tok(MEM_WIDTH), per_b(m, MEM_WIDTH), per_b(MEM_WIDTH, m), tok(NSAG_PAD),
                  _const_spec((ns, nc)),
                  _layer_spec(l, (POOL_GROUPS, POOL_GROUP_WIDTH, POOL_GROUP_WIDTH)), _layer_spec(l, (1, POOL_WIDTH))],
        out_specs=[tok(NSA_WIDTH), tok(POOL_WIDTH), tok(MEM_WIDTH)],
        out_shape=[out, out, out],
        scratch_shapes=[pltpu.VMEM((16 + tq, POOL_WIDTH), F32), pltpu.VMEM((KV_HEADS, HEAD_DIM, GROUP * tq), F32)],
        compiler_params=_params("parallel", "arbitrary"),
        name="prompt_attn",
    )(q, kc, vct, kext, kw, vt, pool_u, pool_u, memq, mk, mvt, ng, consts["pair_t"], w["w_pool"], w["pool_scale"])


def _merge_ffn_kernel(x_ref, onsa_ref, pooly_ref, memo_ref, mg_ref, wn_ref, wp_ref, wm_ref, wo_ref, fg_ref,
                      wgu_ref, wd_ref, y_ref):
    h = (mg_ref[:, 0:D_MODEL] * _dot(onsa_ref[...].astype(BF16), wn_ref[...])
         + mg_ref[:, D_MODEL:2 * D_MODEL] * _dot(pooly_ref[...].astype(BF16), wp_ref[...])
         + mg_ref[:, 2 * D_MODEL:3 * D_MODEL] * _dot(memo_ref[...].astype(BF16), wm_ref[...]))
    x1 = x_ref[...] + _dot(h.astype(BF16), wo_ref[...])
    hn = _rms_rows(x1, fg_ref[...]).astype(BF16)
    acc = x1
    for j in range(D_FF // FF_CHUNK):
        lo = j * FF_CHUNK
        gate = _dot(hn, wgu_ref[:, lo:lo + FF_CHUNK])
        up = _dot(hn, wgu_ref[:, D_FF + lo:D_FF + lo + FF_CHUNK])
        act = gate * jax.nn.sigmoid(gate) * up
        acc = acc + _dot(act.astype(BF16), wd_ref[lo:lo + FF_CHUNK, :])
    y_ref[...] = acc


def _merge_ffn(x, onsa, pooly, memo, mg, l, w):
    n = x.shape[0]
    lay = functools.partial(_layer_spec, l)
    tm = min(TM_PROJ, n)
    row = lambda w: pl.BlockSpec((tm, w), lambda i: (i, 0))
    return pl.pallas_call(
        _merge_ffn_kernel,
        grid=(n // tm,),
        in_specs=[row(D_MODEL), row(NSA_WIDTH), row(POOL_WIDTH), row(MEM_WIDTH), row(N_BRANCH * D_MODEL),
                  lay((NSA_WIDTH, D_MODEL)), lay((POOL_WIDTH, D_MODEL)), lay((MEM_WIDTH, D_MODEL)),
                  lay((D_MODEL, D_MODEL)), lay((1, D_MODEL)), lay((D_MODEL, 2 * D_FF)), lay((D_FF, D_MODEL))],
        out_specs=row(D_MODEL),
        out_shape=jax.ShapeDtypeStruct((n, D_MODEL), F32),
        compiler_params=_params("parallel"),
        name="merge_ffn",
    )(x, onsa, pooly, memo, mg, w["w_up_nsa"], w["w_up_pool"], w["w_up_mem"], w["w_out"], w["ffn_g"],
      w["w_gate_up"], w["w_down"])


def _page_specs(npg, half, layer_of, batch_of, step_of):
    def spec(i):
        def index_map(*args):
            pt = args[-1]
            grid = args[:-1]
            return (layer_of(grid), pt[batch_of(grid), step_of(grid) * npg + i], half, 0)
        return pl.BlockSpec((None, None, 2 * KV_WIDTH, PAGE_SIZE), index_map)
    return [spec(i) for i in range(npg)]


def _compress_pages_kernel(pt_ref, *refs):
    del pt_ref
    npg = len(refs) - 7
    pages = refs[:npg]
    pe_ref, wkt_ref, wvt_ref, kcg_ref, ones64_ref, poolm_ref, out_ref = refs[npg:]
    x = jnp.concatenate([pg[...] for pg in pages], axis=1)
    pooled = _dot(x.astype(BF16), poolm_ref[...])
    pooled = pooled + jnp.sum(pe_ref[...], axis=1, keepdims=True) * (1.0 / CMP_BLOCK)
    k = _dot_f32(wkt_ref[...], pooled[0:KV_WIDTH])
    v = _dot_f32(wvt_ref[...], pooled[KV_WIDTH:2 * KV_WIDTH])
    ssq = _dot_split_rhs(ones64_ref[...], k * k, 2)
    out_ref[0:KV_WIDTH, :] = k * lax.rsqrt(ssq * (1.0 / HEAD_DIM) + EPS) * kcg_ref[...]
    out_ref[KV_WIDTH:2 * KV_WIDTH, :] = v


def _compress_pages(cache_t, page_table, w):
    depth = cache_t.shape[0]
    bd, n_pages = page_table.shape
    npg = min(PAGES_PER_STEP_CMP, n_pages)
    per_page = PAGE_SIZE // CMP_BLOCK
    per_step = npg * per_page
    ncs = n_pages * per_page
    lconst = lambda r, c: pl.BlockSpec((None, r, c), lambda l, b, s, pt: (l, 0, 0))
    const = lambda r, c: pl.BlockSpec((r, c), lambda l, b, s, pt: (0, 0))
    grid_spec = pltpu.PrefetchScalarGridSpec(
        num_scalar_prefetch=1,
        grid=(depth, bd, n_pages // npg),
        in_specs=_page_specs(npg, 0, lambda g: g[0], lambda g: g[1], lambda g: g[2])
        + [lconst(2 * KV_WIDTH, CMP_BLOCK), lconst(KV_WIDTH, KV_WIDTH), lconst(KV_WIDTH, KV_WIDTH), lconst(KV_WIDTH, 1),
           const(KV_WIDTH, KV_WIDTH), const(npg * PAGE_SIZE, per_step)],
        out_specs=pl.BlockSpec((None, None, 2 * KV_WIDTH, per_step), lambda l, b, s, pt: (l, b, 0, s)),
    )
    rows = jnp.arange(npg * PAGE_SIZE)[:, None] // CMP_BLOCK
    poolm = jnp.where(rows == jnp.arange(per_step)[None, :], 1.0 / CMP_BLOCK, 0.0).astype(BF16)
    return pl.pallas_call(
        _compress_pages_kernel,
        grid_spec=grid_spec,
        out_shape=jax.ShapeDtypeStruct((depth, bd, 2 * KV_WIDTH, ncs), F32),
        compiler_params=_params("parallel", "parallel", "arbitrary"),
        name="compress_pages",
    )(page_table, *([cache_t] * npg), w["cmp_pe_t"], w["cmp_wk_t"], w["cmp_wv_t"], w["kc_g_col"], w["ones64_kv"], poolm)


def _rows8(row):
    return jnp.broadcast_to(row, (V7X_SUBLANES, row.shape[1]))


def _sample_attn_kernel(pt_ref, *refs, past_len):
    del pt_ref
    npg = len(refs) - 25
    pages = refs[:npg]
    (q_ref, nsakv_ref, winkv_ref, ng_ref, poolu_ref, memq_ref, kcvc_ref, wincache_ref, poolstate_ref, memcache_ref,
     e_ref, pair_ref, fold_ref, unfold_ref, wpool_ref, pscale_ref,
     onsa_ref, pooly_ref, memo_ref, winout_ref,
     qbd_scr, bias_scr, m_scr, l_scr, acc_scr) = refs[npg:]
    step = pl.program_id(1)
    n_steps = pl.num_programs(1)
    ncs = kcvc_ref.shape[1]
    nsp = pair_ref.shape[1]
    keys_per_step = npg * PAGE_SIZE
    pos = past_len
    row8 = _iota((8, KV_WIDTH), 0)
    lane8 = _iota((8, KV_WIDTH), 1)
    head_lanes = (lane8 // HEAD_DIM) == (row8 // GROUP)

    @pl.when(step == 0)
    def _():
        q8 = jnp.where((_iota((8, NSA_WIDTH), 1) // HEAD_DIM) == _iota((8, NSA_WIDTH), 0),
                       _rows8(q_ref[...].astype(F32)), 0.0)
        qbd = _dot(q8.astype(BF16), fold_ref[...])
        qbd_scr[...] = qbd
        s = _dot(qbd.astype(BF16), kcvc_ref[0:KV_WIDTH, :].astype(BF16))
        cmask = (_iota((8, ncs), 1) * CMP_BLOCK + (CMP_BLOCK - 1)) <= pos
        p = _masked_softmax_rows(s[None], cmask)[0]
        ocmp = _dot_nt(p.astype(BF16), kcvc_ref[KV_WIDTH:2 * KV_WIDTH, :].astype(BF16))
        acc_scr[0] = jnp.where(head_lanes, ocmp, 0.0)
        eye = _iota((nsp, nsp), 0) == _iota((nsp, nsp), 1)
        ii = _iota((nsp, nsp), 1)
        jj = _iota((nsp, nsp), 0)
        cur = pos // SEL_BLOCK
        sel_rows = []
        for k in range(KV_HEADS):
            imp = jnp.sum(p[k * GROUP:(k + 1) * GROUP], axis=0, keepdims=True)
            imp2 = _dot_split_lhs(_rows8(imp), pair_ref[...], 3)[0:1]
            blk = _iota((1, nsp), 1)
            forced = (blk == 0) | (blk == cur) | (blk == cur - 1)
            valid = blk * SEL_BLOCK <= pos
            score = jnp.where(valid, jnp.where(forced, BIG, imp2), -BIG)
            score_i = jnp.broadcast_to(score, (nsp, nsp))
            score_j = jnp.sum(jnp.where(eye, score_i, 0.0), axis=1, keepdims=True)
            beats = jnp.where((score_j > score_i) | ((score_j == score_i) & (jj < ii)), 1.0, 0.0)
            cnt = jnp.sum(beats, axis=0, keepdims=True)
            cnt = cnt + jnp.where(score < BIG, 1.0, 0.0)
            sel_rows.append(jnp.where(cnt < TOP_K, 1.0, 0.0))
        sel8 = jnp.where(_iota((8, nsp), 0) < GROUP, _rows8(sel_rows[0]), _rows8(sel_rows[1])).astype(BF16)
        for st in range(bias_scr.shape[0]):
            selx = _dot(sel8, e_ref[:, st * keys_per_step:(st + 1) * keys_per_step])
            bias_scr[st] = jnp.where(selx > 0.5, 0.0, NEG)
        m_scr[...] = jnp.full(m_scr.shape, NEG, F32)
        l_scr[...] = jnp.zeros(l_scr.shape, F32)
        acc_scr[1] = jnp.zeros((8, KV_WIDTH), F32)

    qbd = qbd_scr[...]
    qbd_bf = qbd.astype(BF16)
    k_t = jnp.concatenate([pg[0:KV_WIDTH, :] for pg in pages], axis=1).astype(BF16)
    v_t = jnp.concatenate([pg[KV_WIDTH:2 * KV_WIDTH, :] for pg in pages], axis=1).astype(BF16)
    s_all = _dot(qbd_bf, k_t) + bias_scr[step]
    m_old = m_scr[...]
    m_new = jnp.maximum(m_old, jnp.max(s_all, axis=-1, keepdims=True))
    p_all = jnp.exp2(s_all - m_new) * jnp.where(s_all > 0.5 * NEG, 1.0, 0.0)
    alpha = jnp.exp2(m_old - m_new)
    l_scr[...] = alpha * l_scr[...] + jnp.sum(p_all, axis=-1, keepdims=True)
    acc_scr[1] = alpha * acc_scr[1] + _dot_nt(p_all.astype(BF16), v_t)
    m_scr[...] = m_new

    @pl.when(step == n_steps - 1)
    def _():
        ks_new = _rows8(nsakv_ref[:, 2 * KV_WIDTH:3 * KV_WIDTH])
        vs_new = _rows8(nsakv_ref[:, 3 * KV_WIDTH:4 * KV_WIDTH])
        s_new = jnp.sum(qbd * ks_new, axis=-1, keepdims=True)
        m_old = m_scr[...]
        m_fin = jnp.maximum(m_old, s_new)
        alpha = jnp.exp2(m_old - m_fin)
        p_new = jnp.exp2(s_new - m_fin)
        l_fin = alpha * l_scr[...] + p_new
        osel = (alpha * acc_scr[1] + p_new * vs_new) * (1.0 / l_fin)
        wb = wincache_ref.shape[1]
        kw_new = _rows8(winkv_ref[:, 0:KV_WIDTH])
        vw_new = _rows8(winkv_ref[:, KV_WIDTH:2 * KV_WIDTH])
        s_w = _dot(qbd_bf, wincache_ref[0:KV_WIDTH, :].astype(BF16))
        kpos = pos - wb + _iota((8, wb), 1)
        wmask = (kpos <= pos) & (kpos > pos - WINDOW)
        s_wn = jnp.sum(qbd * kw_new, axis=-1, keepdims=True)
        s_w = jnp.where(wmask, s_w, NEG)
        m_w = jnp.maximum(jnp.max(s_w, axis=-1, keepdims=True), s_wn)
        p_w = jnp.exp2(s_w - m_w) * jnp.where(wmask, 1.0, 0.0)
        p_wn = jnp.exp2(s_wn - m_w)
        l_w = jnp.sum(p_w, axis=-1, keepdims=True) + p_wn
        owin = (_dot_nt(p_w.astype(BF16), wincache_ref[KV_WIDTH:2 * KV_WIDTH, :].astype(BF16)) + p_wn * vw_new) * (1.0 / l_w)
        ng8 = _rows8(ng_ref[...])
        glane = _iota((8, NSAG_PAD), 1)
        grow = _iota((8, NSAG_PAD), 0)
        gate = lambda br: jnp.sum(jnp.where(glane == N_NSA_BRANCH * grow + br, ng8, 0.0), axis=-1, keepdims=True)
        o8 = jnp.where(head_lanes, gate(0) * acc_scr[0] + gate(1) * osel + gate(2) * owin, 0.0)
        o512 = _dot_split_lhs(o8, unfold_ref[...], 3)
        own = (_iota((8, NSA_WIDTH), 1) // HEAD_DIM) == _iota((8, NSA_WIDTH), 0)
        onsa_ref[...] = jnp.sum(jnp.where(own, o512, 0.0), axis=0, keepdims=True)
        mtok = memcache_ref.shape[0] // (2 * MEM_HEADS)
        outs = []
        for h in range(MEM_HEADS):
            k_h = memcache_ref[pl.ds(h, mtok, stride=2 * MEM_HEADS), :].astype(BF16)
            v_h = memcache_ref[pl.ds(MEM_HEADS + h, mtok, stride=2 * MEM_HEADS), :].astype(BF16)
            q_h = _rows8(memq_ref[:, h * MEM_HEAD_DIM:(h + 1) * MEM_HEAD_DIM].astype(F32)).astype(BF16)
            s_m = _dot_nt(q_h, k_h) * MEM_HEAD_DIM ** -0.5
            p_m = jnp.exp(s_m - jnp.max(s_m, axis=-1, keepdims=True))
            p_m = p_m * (1.0 / jnp.sum(p_m, axis=-1, keepdims=True))
            outs.append(_dot(p_m.astype(BF16), v_h)[0:1])
        memo_ref[...] = jnp.concatenate(outs, axis=1)
        u_new = poolu_ref[...]
        state = poolstate_ref[...]
        srow = _iota((POOL_BUF, POOL_GROUP_WIDTH), 0)
        ys = []
        for gi, win in enumerate(POOL_WINDOWS):
            lo = gi * POOL_GROUP_WIDTH
            u_g = u_new[:, lo:lo + POOL_GROUP_WIDTH]
            tail = jnp.sum(jnp.where(srow >= POOL_BUF - (win - 1), state[:, lo:lo + POOL_GROUP_WIDTH], 0.0),
                           axis=0, keepdims=True)
            d = (tail + u_g) / float(min(pos + 1, win)) - u_g
            ys.append(_dot(_rows8(d).astype(BF16), wpool_ref[gi])[0:1])
        pooly_ref[...] = jnp.concatenate(ys, axis=1) * pscale_ref[...]
        nf = 2 * KV_WIDTH
        eye_f = _iota((nf, nf), 0) == _iota((nf, nf), 1)
        new_col = jnp.sum(jnp.where(eye_f, jnp.broadcast_to(winkv_ref[...], (nf, nf)), 0.0), axis=1, keepdims=True)
        shifted = pltpu.roll(wincache_ref[...], wb - 1, axis=1)
        winout_ref[...] = jnp.where(_iota((nf, wb), 1) == wb - 1, new_col, shifted)


def _sample_attn(l, q, nsakv, winkv, ng, pool_u, memq, kcvc_s, cache_t, wincache_t, poolstate, memcache, page_table,
                 w, consts):
    bd, n_pages = page_table.shape
    npg = min(PAGES_PER_STEP_SEL, n_pages)
    n_steps = n_pages // npg
    past_len = n_pages * PAGE_SIZE
    ncs = past_len // CMP_BLOCK
    nsp = past_len // SEL_BLOCK
    wb = wincache_t.shape[3]
    mrows = memcache.shape[2]
    row = lambda w: pl.BlockSpec((None, 1, w), lambda b, s, pt: (b, 0, 0))
    const = lambda shape: pl.BlockSpec(shape, lambda b, s, pt: (0,) * len(shape))
    per_b = lambda r, c: pl.BlockSpec((None, None, r, c), lambda b, s, pt: (l, b, 0, 0))
    in_specs = (
        _page_specs(npg, 1, lambda g: l, lambda g: g[0], lambda g: g[1])
        + [row(NSA_WIDTH), row(4 * KV_WIDTH), row(2 * KV_WIDTH), row(NSAG_PAD), row(POOL_WIDTH), row(MEM_WIDTH),
           per_b(2 * KV_WIDTH, ncs), per_b(2 * KV_WIDTH, wb), per_b(POOL_BUF, POOL_WIDTH), per_b(mrows, MEM_HEAD_DIM),
           const((nsp, past_len)), const((ncs, nsp)), const((NSA_WIDTH, KV_WIDTH)), const((KV_WIDTH, NSA_WIDTH)),
           pl.BlockSpec((None, POOL_GROUPS, POOL_GROUP_WIDTH, POOL_GROUP_WIDTH), lambda b, s, pt: (l, 0, 0, 0)),
           pl.BlockSpec((None, 1, POOL_WIDTH), lambda b, s, pt: (l, 0, 0))])
    grid_spec = pltpu.PrefetchScalarGridSpec(
        num_scalar_prefetch=1,
        grid=(bd, n_steps),
        in_specs=in_specs,
        out_specs=[row(NSA_WIDTH), row(POOL_WIDTH), row(MEM_WIDTH),
                   pl.BlockSpec((None, 2 * KV_WIDTH, wb), lambda b, s, pt: (b, 0, 0))],
        scratch_shapes=[pltpu.VMEM((8, KV_WIDTH), F32), pltpu.VMEM((n_steps, 8, npg * PAGE_SIZE), F32),
                        pltpu.VMEM((8, 1), F32), pltpu.VMEM((8, 1), F32), pltpu.VMEM((2, 8, KV_WIDTH), F32)],
    )
    out = jax.ShapeDtypeStruct((bd, 1, NSA_WIDTH), F32)
    return pl.pallas_call(
        functools.partial(_sample_attn_kernel, past_len=past_len),
        grid_spec=grid_spec,
        out_shape=[out, out, out, jax.ShapeDtypeStruct((bd, 2 * KV_WIDTH, wb), F32)],
        compiler_params=_params("parallel", "arbitrary"),
        name="sample_attn",
    )(page_table, *([cache_t] * npg), q, nsakv, winkv, ng, pool_u, memq, kcvc_s, wincache_t, poolstate, memcache,
      consts["sel_expand_s"], consts["pair_s"], consts["fold"], consts["unfold"], w["w_pool"], w["pool_scale"])


def _block_diag_ones(n, seg):
    return (jnp.arange(n)[:, None] // seg == jnp.arange(n)[None, :] // seg).astype(BF16)


def _constants(t, past_len):
    nc, ns = t // CMP_BLOCK, t // SEL_BLOCK
    ratio = SEL_BLOCK // CMP_BLOCK
    pair_t = (jnp.arange(ns)[:, None] == jnp.arange(nc)[None, :] // ratio).astype(BF16)
    ncs, nsp = past_len // CMP_BLOCK, past_len // SEL_BLOCK
    sel_expand_s = (jnp.arange(nsp)[:, None] == (jnp.arange(past_len) // SEL_BLOCK)[None, :]).astype(BF16)
    pair_s = (jnp.arange(ncs)[:, None] // ratio == jnp.arange(nsp)[None, :]).astype(BF16)
    c = jnp.arange(NSA_WIDTH)
    fold_col = (c // (GROUP * HEAD_DIM)) * HEAD_DIM + c % HEAD_DIM
    fold = (fold_col[:, None] == jnp.arange(KV_WIDTH)[None, :]).astype(BF16)
    return {
        "ones64": _block_diag_ones(NSA_WIDTH, HEAD_DIM), "ones128": _block_diag_ones(MEM_WIDTH, MEM_HEAD_DIM),
        "ones64_kv": _block_diag_ones(KV_WIDTH, HEAD_DIM),
        "pair_t": pair_t,
        "sel_expand_s": sel_expand_s, "pair_s": pair_s, "fold": fold, "unfold": fold.T,
    }


def kernel(x_prompt, x_sample, mem_prompt, cache_nsa_kv, cache_win_kv, state_pool, cache_mem_kv, page_table,
           attn_norm_g, w_in, nsa_q_g, nsa_kc_g, nsa_ks_g, nsa_kw_g, cmp_pe_k, cmp_pe_v, cmp_wk, cmp_wv,
           w_pool, pool_scale, mem_norm_g, w_mem_kv, mem_q_g, mem_k_g, w_up_nsa, w_up_pool, w_up_mem, w_out,
           ffn_norm_g, w_gate_up, w_down):
    depth = w_in.shape[0]
    b, t, _ = x_prompt.shape
    bd = x_sample.shape[0]
    n_pages = page_table.shape[1]
    past_len = n_pages * PAGE_SIZE
    mtok = mem_prompt.shape[1]
    wb = cache_win_kv.shape[2]
    assert x_sample.shape[1] == 1 and wb == WINDOW and past_len >= WINDOW and mtok == cache_mem_kv.shape[2]
    consts = _constants(t, past_len)

    nsag_lo = NSA_WIDTH + 6 * KV_WIDTH
    nsag_hi = nsag_lo + N_NSA_BRANCH * N_HEADS
    w_t = w_in.transpose(0, 2, 1)
    w_in_t = jnp.concatenate(
        [w_t[:, :nsag_lo], w_t[:, nsag_hi:], w_t[:, nsag_lo:nsag_hi],
         jnp.zeros((depth, NSAG_PAD - N_NSA_BRANCH * N_HEADS, D_MODEL), w_in.dtype)], axis=1).astype(BF16)
    eye_kv = jnp.eye(KV_HEADS, dtype=F32)
    rows = lambda g, reps: jnp.tile(g, (1, reps))[:, None, :].astype(F32)
    w = {
        "attn_g": attn_norm_g[:, None, :], "w_in_t": w_in_t,
        "q_g": rows(nsa_q_g, N_HEADS), "ks_g": rows(nsa_ks_g, KV_HEADS), "kw_g": rows(nsa_kw_g, KV_HEADS),
        "mq_g": rows(mem_q_g, MEM_HEADS), "kc_g": rows(nsa_kc_g, KV_HEADS), "mk_g": rows(mem_k_g, MEM_HEADS),
        "cmp_pe": jnp.concatenate([jnp.tile(cmp_pe_k, (1, 1, KV_HEADS)), jnp.tile(cmp_pe_v, (1, 1, KV_HEADS))], axis=2),
        "cmp_wk": jax.vmap(lambda m: jnp.kron(eye_kv, m))(cmp_wk), "cmp_wv": jax.vmap(lambda m: jnp.kron(eye_kv, m))(cmp_wv),
        "w_pool": w_pool.astype(BF16), "pool_scale": pool_scale[:, None, :],
        "mem_g": mem_norm_g[:, None, :], "w_mem_kv": w_mem_kv.astype(BF16),
        "w_mem_v_t": w_mem_kv[:, :, MEM_WIDTH:].transpose(0, 2, 1).astype(BF16),
        "w_up_nsa": w_up_nsa.astype(BF16), "w_up_pool": w_up_pool.astype(BF16), "w_up_mem": w_up_mem.astype(BF16),
        "w_out": w_out.astype(BF16), "ffn_g": ffn_norm_g[:, None, :],
        "w_gate_up": w_gate_up.astype(BF16), "w_down": w_down.astype(BF16),
    }
    swap = lambda a: a.transpose(0, 2, 1)
    cmp_w = {"cmp_pe_t": swap(w["cmp_pe"]), "cmp_wk_t": swap(w["cmp_wk"]), "cmp_wv_t": swap(w["cmp_wv"]),
             "kc_g_col": swap(w["kc_g"]), "ones64_kv": consts["ones64_kv"]}

    cache_t = cache_nsa_kv.transpose(0, 1, 3, 4, 5, 2).reshape(depth, cache_nsa_kv.shape[1], 4 * KV_WIDTH, PAGE_SIZE)
    wincache_t = cache_win_kv.transpose(0, 1, 3, 4, 5, 2).reshape(depth, bd, 2 * KV_WIDTH, wb)
    memcache = cache_mem_kv.reshape(depth, bd, cache_mem_kv.shape[2] * 2 * MEM_HEADS, MEM_HEAD_DIM)
    kcvc_s = _compress_pages(cache_t, page_table, cmp_w)

    xp = x_prompt.reshape(b * t, D_MODEL)
    xs = x_sample.reshape(bd, D_MODEL)
    nsa_p, nsa_s, win_p, win_s, pool_p, pool_s, mem_p = [], [], [], [], [], [], []
    for l in range(depth):
        q, nsakv, winkv, kext, kwbf, vt, pool_u, memq, mg, ng = _inproj(xp, l, w, consts, t)
        mkv, mk, mvt = _memkv(mem_prompt, l, w, consts)
        nsakv3 = nsakv.reshape(b, t, 4 * KV_WIDTH)
        kc, vct = _compress_prompt(nsakv3, l, w, consts)
        pool3 = pool_u.reshape(b, t, POOL_WIDTH)
        by_b = lambda a: a.reshape(b, t, a.shape[-1])
        onsa, pooly, memo = _prompt_attn(
            by_b(q), kc, vct, by_b(kext), by_b(kwbf), vt.reshape(b, -1, vt.shape[1], vt.shape[2]), pool3, by_b(memq),
            mk, mvt, by_b(ng), l, w, consts)
        xp = _merge_ffn(xp, onsa.reshape(b * t, NSA_WIDTH), pooly.reshape(b * t, POOL_WIDTH),
                        memo.reshape(b * t, MEM_WIDTH), mg, l, w)
        nsa_p.append(nsakv3.reshape(b, t, 4, KV_HEADS, HEAD_DIM))
        win_rows = winkv.reshape(b, t, 2 * KV_WIDTH)[:, t - min(WINDOW, t):]
        win_p.append(win_rows.reshape(b, win_rows.shape[1], 2, KV_HEADS, HEAD_DIM))
        pool_p.append(pool3[:, t - POOL_BUF:])
        mem_p.append(mkv.reshape(b, mtok, 2, MEM_HEADS, MEM_HEAD_DIM))
        q, nsakv, winkv, _, _, _, pool_u, memq, mg, ng = _inproj(xs, l, w, consts, bd)
        r3 = lambda a: a.reshape(bd, 1, a.shape[-1])
        onsa, pooly, memo, win_t = _sample_attn(l, r3(q), r3(nsakv), r3(winkv), r3(ng), r3(pool_u), r3(memq), kcvc_s,
                                                cache_t, wincache_t, state_pool, memcache, page_table, w, consts)
        xs = _merge_ffn(xs, onsa.reshape(bd, NSA_WIDTH), pooly.reshape(bd, POOL_WIDTH), memo.reshape(bd, MEM_WIDTH), mg, l, w)
        nsa_s.append(nsakv.reshape(bd, 1, 4, KV_HEADS, HEAD_DIM))
        win_s.append(win_t)
        pool_s.append(jnp.concatenate([state_pool[l][:, 1:], pool_u.reshape(bd, 1, POOL_WIDTH)], axis=1))
    win_s_out = jnp.stack(win_s).reshape(depth, bd, 2, KV_HEADS, HEAD_DIM, wb).transpose(0, 1, 5, 2, 3, 4)
    return (xp.reshape(b, t, D_MODEL), xs.reshape(bd, 1, D_MODEL), jnp.stack(nsa_p), jnp.stack(nsa_s),
            jnp.stack(win_p), win_s_out, jnp.stack(pool_p), jnp.stack(pool_s), jnp.stack(mem_p))
```

```python
import functools

import jax
import jax.numpy as jnp
from jax import lax
from jax.experimental import pallas as pl
from jax.experimental.pallas import tpu as pltpu

F32 = jnp.float32
BF16 = jnp.bfloat16

D_MODEL = 1024
PAGE_SIZE = 128
N_HEADS = 8
HEAD_DIM = 64
KV_HEADS = 2
GROUP = N_HEADS // KV_HEADS
NSA_WIDTH = N_HEADS * HEAD_DIM
KV_WIDTH = KV_HEADS * HEAD_DIM
CMP_BLOCK = 32
SEL_BLOCK = 64
TOP_K = 16
WINDOW = 512
N_NSA_BRANCH = 3
POOL_WIDTH = 512
POOL_GROUPS = 4
POOL_GROUP_WIDTH = POOL_WIDTH // POOL_GROUPS
POOL_WINDOWS = (2, 4, 8, 16)
POOL_BUF = 15
MEM_HEADS = 4
MEM_HEAD_DIM = 128
MEM_WIDTH = MEM_HEADS * MEM_HEAD_DIM
N_BRANCH = 3
D_FF = ((8 * D_MODEL // 3 + 255) // 256) * 256
EPS = 1e-6
LOG2_E = 1.4426950408889634
NEG = -1e30
BIG = 1e9

V7X_LANES = 128
V7X_SUBLANES = 8
ONES_ROWS = 2 * V7X_SUBLANES
V7X_VMEM_LIMIT_BYTES = 56 * 1024 * 1024

C_Q = 0
C_NSAKV = C_Q + NSA_WIDTH
C_WINKV = C_NSAKV + 4 * KV_WIDTH
C_POOL = C_WINKV + 2 * KV_WIDTH
C_MEMQ = C_POOL + POOL_WIDTH
C_MERGE = C_MEMQ + MEM_WIDTH
C_NSAG = C_MERGE + N_BRANCH * D_MODEL
NSAG_PAD = V7X_LANES
PROJ_PAD = C_NSAG + NSAG_PAD

TM_PROJ = 512
TQ = 256
SEL_CHUNKS_PER_STEP = 1
FF_CHUNK = D_FF // 2
PAGES_PER_STEP_CMP = 32
PAGES_PER_STEP_SEL = 32
ROWS_PER_DECODE_STEP = 2


def _dot(a, b):
    return jnp.dot(a, b, preferred_element_type=F32)


def _dot_nt(a, b):
    return lax.dot_general(a, b, (((1,), (1,)), ((), ())), preferred_element_type=F32)


def _dot_f32(a, b):
    return jnp.dot(a, b, preferred_element_type=F32, precision=lax.Precision.HIGHEST)


def _split_bf16(a, n):
    parts, r = [], a
    for _ in range(n):
        p = r.astype(BF16)
        parts.append(p)
        r = r - p.astype(F32)
    return parts


def _dot_split_lhs(a, b, n):
    return sum(_dot(p, b) for p in _split_bf16(a, n))


def _dot_split_rhs(a, b, n):
    return sum(_dot(a, p) for p in _split_bf16(b, n))


def _iota(shape, dim):
    return lax.broadcasted_iota(jnp.int32, shape, dim)


def _rms_rows(x, g):
    return x * lax.rsqrt(jnp.mean(x * x, axis=-1, keepdims=True) + EPS) * g


def _seg_rmsnorm(v, ones_bd, gain, seg):
    ssq = _dot((v * v).astype(BF16), ones_bd)
    return v * lax.rsqrt(ssq * (1.0 / seg) + EPS) * gain


def _const_spec(shape):
    return pl.BlockSpec(shape, lambda *_: (0,) * len(shape), pipeline_mode=pl.Buffered(1))


def _layer_spec(l, shape):
    return pl.BlockSpec((None,) + tuple(shape), lambda *_: (l,) + (0,) * len(shape), pipeline_mode=pl.Buffered(1))


def _params(*sem):
    return pltpu.CompilerParams(dimension_semantics=sem, vmem_limit_bytes=V7X_VMEM_LIMIT_BYTES)


def _inproj_kernel(x_ref, g_ref, w_ref, qg_ref, ksg_ref, kwg_ref, mqg_ref, ones64_ref, ones128_ref,
                   q_ref, nsakv_ref, winkv_ref, kext_ref, kwbf_ref, vt_ref, pool_ref, memq_ref, mg_ref, ng_ref,
                   *, seq_len):
    tm = x_ref.shape[0]
    hb = _rms_rows(x_ref[...], g_ref[...]).astype(BF16)

    def proj(lo, width):
        return _dot_nt(hb, w_ref[lo:lo + width, :])

    ones_kv = ones64_ref[0:KV_WIDTH, 0:KV_WIDTH]
    q = _seg_rmsnorm(proj(C_Q, NSA_WIDTH), ones64_ref[...], qg_ref[...], HEAD_DIM)
    q_ref[...] = (q * (HEAD_DIM ** -0.5 * LOG2_E)).astype(BF16)
    kcvc = proj(C_NSAKV, 2 * KV_WIDTH)
    ks = _seg_rmsnorm(proj(C_NSAKV + 2 * KV_WIDTH, KV_WIDTH), ones_kv, ksg_ref[...], HEAD_DIM)
    vs = proj(C_NSAKV + 3 * KV_WIDTH, KV_WIDTH)
    kw = _seg_rmsnorm(proj(C_WINKV, KV_WIDTH), ones_kv, kwg_ref[...], HEAD_DIM)
    vw = proj(C_WINKV + KV_WIDTH, KV_WIDTH)
    nsakv_ref[:, 0:2 * KV_WIDTH] = kcvc
    nsakv_ref[:, 2 * KV_WIDTH:3 * KV_WIDTH] = ks
    nsakv_ref[:, 3 * KV_WIDTH:4 * KV_WIDTH] = vs
    winkv_ref[:, 0:KV_WIDTH] = kw
    winkv_ref[:, KV_WIDTH:2 * KV_WIDTH] = vw
    t_seq = (pl.program_id(0) * tm + _iota((tm, HEAD_DIM), 0)) % seq_len
    onehot = jnp.where(t_seq // SEL_BLOCK == _iota((tm, HEAD_DIM), 1), 1.0, 0.0).astype(BF16)
    for k in range(KV_HEADS):
        kext_ref[:, 2 * k * HEAD_DIM:(2 * k + 1) * HEAD_DIM] = ks[:, k * HEAD_DIM:(k + 1) * HEAD_DIM].astype(BF16)
        kext_ref[:, (2 * k + 1) * HEAD_DIM:(2 * k + 2) * HEAD_DIM] = onehot
    kwbf_ref[...] = kw.astype(BF16)
    vs_lo, vw_lo = C_NSAKV + 3 * KV_WIDTH, C_WINKV + KV_WIDTH
    vt_ref[0:KV_WIDTH, :] = _dot_nt(w_ref[vs_lo:vs_lo + KV_WIDTH, :], hb).astype(BF16)
    vt_ref[KV_WIDTH:2 * KV_WIDTH, :] = _dot_nt(w_ref[vw_lo:vw_lo + KV_WIDTH, :], hb).astype(BF16)
    pool_ref[...] = proj(C_POOL, POOL_WIDTH)
    mq = _seg_rmsnorm(proj(C_MEMQ, MEM_WIDTH), ones128_ref[...], mqg_ref[...], MEM_HEAD_DIM)
    memq_ref[...] = mq.astype(BF16)
    for j in range(N_BRANCH):
        mg_ref[:, j * D_MODEL:(j + 1) * D_MODEL] = jax.nn.sigmoid(proj(C_MERGE + j * D_MODEL, D_MODEL))
    ng_ref[...] = jax.nn.sigmoid(proj(C_NSAG, NSAG_PAD))


def _inproj(x, l, w, consts, seq_len):
    n = x.shape[0]
    lay = functools.partial(_layer_spec, l)
    tm = min(TM_PROJ, n)
    row = lambda w: pl.BlockSpec((tm, w), lambda i: (i, 0))
    widths = (NSA_WIDTH, 4 * KV_WIDTH, 2 * KV_WIDTH, 2 * KV_WIDTH, KV_WIDTH, None, POOL_WIDTH, MEM_WIDTH,
              N_BRANCH * D_MODEL, NSAG_PAD)
    dtypes = (BF16, F32, F32, BF16, BF16, BF16, F32, BF16, F32, F32)
    vt_spec = pl.BlockSpec((None, 2 * KV_WIDTH, tm), lambda i: (i, 0, 0))
    vt_shape = jax.ShapeDtypeStruct((n // tm, 2 * KV_WIDTH, tm), BF16)
    return pl.pallas_call(
        functools.partial(_inproj_kernel, seq_len=seq_len),
        grid=(n // tm,),
        in_specs=[row(D_MODEL), lay((1, D_MODEL)), lay((PROJ_PAD, D_MODEL)),
                  lay((1, NSA_WIDTH)), lay((1, KV_WIDTH)), lay((1, KV_WIDTH)), lay((1, MEM_WIDTH)),
                  _const_spec((NSA_WIDTH, NSA_WIDTH)), _const_spec((MEM_WIDTH, MEM_WIDTH))],
        out_specs=[vt_spec if wd is None else row(wd) for wd in widths],
        out_shape=[vt_shape if wd is None else jax.ShapeDtypeStruct((n, wd), d) for wd, d in zip(widths, dtypes)],
        compiler_params=_params("parallel"),
        name="inproj",
    )(x, w["attn_g"], w["w_in_t"], w["q_g"], w["ks_g"], w["kw_g"], w["mq_g"], consts["ones64"], consts["ones128"])


def _memkv_kernel(m_ref, g_ref, w_ref, wvt_ref, kg_ref, ones128_ref, kv_ref, kbf_ref, vt_ref):
    hb = _rms_rows(m_ref[...], g_ref[...]).astype(BF16)
    k = _seg_rmsnorm(_dot(hb, w_ref[:, 0:MEM_WIDTH]), ones128_ref[...], kg_ref[...], MEM_HEAD_DIM)
    kv_ref[:, 0:MEM_WIDTH] = k
    kv_ref[:, MEM_WIDTH:2 * MEM_WIDTH] = _dot(hb, w_ref[:, MEM_WIDTH:2 * MEM_WIDTH])
    kbf_ref[...] = k.astype(BF16)
    vt_ref[...] = _dot_nt(wvt_ref[...], hb).astype(BF16)


def _memkv(mem, l, w, consts):
    b, mtok, _ = mem.shape
    lay = functools.partial(_layer_spec, l)
    blk = lambda r, c: pl.BlockSpec((None, r, c), lambda i: (i, 0, 0))
    return pl.pallas_call(
        _memkv_kernel,
        grid=(b,),
        in_specs=[blk(mtok, D_MODEL), lay((1, D_MODEL)), lay((D_MODEL, 2 * MEM_WIDTH)),
                  lay((MEM_WIDTH, D_MODEL)), lay((1, MEM_WIDTH)), _const_spec((MEM_WIDTH, MEM_WIDTH))],
        out_specs=[blk(mtok, 2 * MEM_WIDTH), blk(mtok, MEM_WIDTH), blk(MEM_WIDTH, mtok)],
        out_shape=[jax.ShapeDtypeStruct((b, mtok, 2 * MEM_WIDTH), F32), jax.ShapeDtypeStruct((b, mtok, MEM_WIDTH), BF16),
                   jax.ShapeDtypeStruct((b, MEM_WIDTH, mtok), BF16)],
        compiler_params=_params("parallel"),
        name="memkv",
    )(mem, w["mem_g"], w["w_mem_kv"], w["w_mem_v_t"], w["mk_g"], consts["ones128"])


def _compress_prompt_kernel(kv_ref, pe_ref, wk_ref, wv_ref, kcg_ref, ones64_ref, kc_ref, vct_ref):
    t = kv_ref.shape[0]
    nc = t // CMP_BLOCK
    pooled = jnp.sum(kv_ref[...].reshape(nc, CMP_BLOCK, 2 * KV_WIDTH), axis=1) * (1.0 / CMP_BLOCK)
    pooled = pooled + jnp.sum(pe_ref[...], axis=0, keepdims=True) * (1.0 / CMP_BLOCK)
    k = _dot_f32(pooled[:, 0:KV_WIDTH], wk_ref[...])
    v = _dot_f32(pooled[:, KV_WIDTH:2 * KV_WIDTH], wv_ref[...])
    kc_ref[...] = _seg_rmsnorm(k, ones64_ref[...], kcg_ref[...], HEAD_DIM).astype(BF16)
    vct_ref[...] = v.T.astype(BF16)


def _compress_prompt(nsakv, l, w, consts):
    b, t, _ = nsakv.shape
    nc = t // CMP_BLOCK
    return pl.pallas_call(
        _compress_prompt_kernel,
        grid=(b,),
        in_specs=[pl.BlockSpec((None, t, 2 * KV_WIDTH), lambda i: (i, 0, 0)),
                  _layer_spec(l, (CMP_BLOCK, 2 * KV_WIDTH)), _layer_spec(l, (KV_WIDTH, KV_WIDTH)),
                  _layer_spec(l, (KV_WIDTH, KV_WIDTH)), _layer_spec(l, (1, KV_WIDTH)), _const_spec((KV_WIDTH, KV_WIDTH))],
        out_specs=[pl.BlockSpec((None, nc, KV_WIDTH), lambda i: (i, 0, 0)),
                   pl.BlockSpec((None, KV_WIDTH, nc), lambda i: (i, 0, 0))],
        out_shape=[jax.ShapeDtypeStruct((b, nc, KV_WIDTH), BF16), jax.ShapeDtypeStruct((b, KV_WIDTH, nc), BF16)],
        compiler_params=_params("parallel"),
        name="compress_prompt",
    )(nsakv, w["cmp_pe"], w["cmp_wk"], w["cmp_wv"], w["kc_g"], consts["ones64_kv"])


def _masked_softmax_rows(s, mask):
    maskf = jnp.where(mask, 1.0, 0.0)
    s = jnp.where(mask[None], s, NEG)
    m = jnp.max(s, axis=-1, keepdims=True)
    p = jnp.exp2(s - m) * maskf[None]
    den = jnp.maximum(jnp.sum(p, axis=-1, keepdims=True), 1e-30)
    return p * (1.0 / den)


def _stack_heads(q, k):
    return jnp.concatenate([q[:, (k * GROUP + g) * HEAD_DIM:(k * GROUP + g + 1) * HEAD_DIM] for g in range(GROUP)], axis=0)


def _transpose_tiles(x):
    r, c = x.shape
    n = V7X_LANES
    return jnp.concatenate(
        [jnp.concatenate([x[i * n:(i + 1) * n, j * n:(j + 1) * n].T for i in range(r // n)], axis=1) for j in range(c // n)],
        axis=0)


def _topk_select_t(score_t, n_sel):
    ns, tq = score_t.shape
    sub = V7X_SUBLANES
    tiles = [score_t[v * sub:(v + 1) * sub] for v in range(ns // sub)]
    row = _iota((sub, tq), 0)
    cnts = [jnp.zeros((sub, tq), F32) for _ in tiles]
    for j in range(ns):
        r = jnp.broadcast_to(score_t[j:j + 1, :], (sub, tq))
        for v, tile in enumerate(tiles):
            if v * sub > j:
                wins = r >= tile
            elif v * sub + sub - 1 < j:
                wins = r > tile
            else:
                wins = jnp.where(row + v * sub > j, jnp.where(r >= tile, 1.0, 0.0), jnp.where(r > tile, 1.0, 0.0)) > 0.5
            cnts[v] = cnts[v] + jnp.where(wins, 1.0, 0.0)
    return jnp.where(jnp.concatenate(cnts, axis=0) < n_sel, 1.0, 0.0)


def _with_ones_rows(v_t):
    return jnp.concatenate([v_t, jnp.ones((ONES_ROWS, v_t.shape[1]), BF16)], axis=0)


def _pv_and_sum(v_t, p):
    d = v_t.shape[0]
    r = _dot(_with_ones_rows(v_t), p.astype(BF16))
    return r[0:d], r[d:d + 1]


def _online_softmax_step(state, s_t, v_t):
    m, l, acc = state
    m_new = jnp.maximum(m, jnp.max(s_t, axis=0, keepdims=True))
    alpha = jnp.exp2(m - m_new)
    pv, psum = _pv_and_sum(v_t, jnp.exp2(s_t - m_new))
    return m_new, alpha * l + psum, alpha * acc + pv


def _merge_softmax_states(states):
    m = states[0][0]
    for st in states[1:]:
        m = jnp.maximum(m, st[0])
    l, acc = 0.0, 0.0
    for m_i, l_i, acc_i in states:
        w = jnp.exp2(m_i - m)
        l, acc = l + w * l_i, acc + w * acc_i
    return m, l, acc


def _prompt_attn_kernel(q_ref, kc_ref, vct_ref, kext_ref, kw_ref, vt_ref, poolc_ref, poolp_ref, memq_ref, mk_ref, mvt_ref,
                        ng_ref, pairt_ref, wpool_ref, pscale_ref,
                        onsa_ref, pooly_ref, memo_ref, pool_scr, part_scr):
    tq = q_ref.shape[0]
    nc = kc_ref.shape[0]
    ns = pairt_ref.shape[0]
    kc = vt_ref.shape[2]
    rows = GROUP * tq
    n_sel = min(TOP_K, ns)
    qi = pl.program_id(1)
    t0 = qi * tq
    q = q_ref[...]

    eye_tq = jnp.where(_iota((tq, tq), 0) == _iota((tq, tq), 1), 1.0, 0.0).astype(BF16)
    blk_t = _iota((ns, tq), 0)
    tpos_t = t0 + _iota((ns, tq), 1)
    cur_t = tpos_t // SEL_BLOCK
    valid_t = blk_t * SEL_BLOCK <= tpos_t
    forced_t = (blk_t == 0) | (blk_t == cur_t) | (blk_t == cur_t - 1)
    tok_of_row = lambda shape: t0 + (_iota(shape, 1) & (tq - 1))
    cmask = (_iota((nc, rows), 0) * CMP_BLOCK + (CMP_BLOCK - 1)) <= tok_of_row((nc, rows))
    cmaskf = jnp.where(cmask, 1.0, 0.0)
    ng_t = _transpose_tiles(ng_ref[...])
    nwb = WINDOW // tq + 1
    rel = _iota((tq, rows), 0) - (_iota((tq, rows), 1) & (tq - 1))
    win_first = jnp.where(rel > 0, 0.0, NEG)
    win_diag = jnp.where(rel <= 0, 0.0, NEG)
    init = (jnp.full((1, rows), NEG, F32), jnp.zeros((1, rows), F32), jnp.zeros((HEAD_DIM, rows), F32))

    def gate_row(k, br):
        idx = [N_NSA_BRANCH * (k * GROUP + g) + br for g in range(GROUP)]
        return jnp.concatenate([ng_t[i:i + 1, :] for i in idx], axis=1)

    q_exts = []
    for k in range(KV_HEADS):
        qk = _stack_heads(q, k)
        s_c = jnp.where(cmask, _dot_nt(kc_ref[:, k * HEAD_DIM:(k + 1) * HEAD_DIM], qk), NEG)
        p = jnp.exp2(s_c - jnp.max(s_c, axis=0, keepdims=True)) * cmaskf
        p = p * (1.0 / jnp.maximum(jnp.sum(p, axis=0, keepdims=True), 1e-30))
        o_cmp = _dot(vct_ref[k * HEAD_DIM:(k + 1) * HEAD_DIM, :], p.astype(BF16))
        imp_t = p[:, 0:tq]
        for g in range(1, GROUP):
            imp_t = imp_t + p[:, g * tq:(g + 1) * tq]
        imp2_t = _dot_split_rhs(pairt_ref[...], imp_t, 3)
        score_t = jnp.where(valid_t, jnp.where(forced_t, BIG, imp2_t), -BIG)
        sel_t = _topk_select_t(score_t, n_sel)
        sel = _dot_nt(eye_tq, sel_t.astype(BF16))
        selneg = jnp.where(sel > 0.5, 0.0, NEG).astype(BF16)
        if ns < HEAD_DIM:
            selneg = jnp.concatenate([selneg, jnp.zeros((tq, HEAD_DIM - ns), BF16)], axis=1)
        q_exts.append(jnp.concatenate([qk, jnp.concatenate([selneg] * GROUP, axis=0)], axis=1))
        s_blocks, v_blocks = [], []
        for i in range(nwb):
            kb = qi - (nwb - 1) + i
            inside = kb >= 0
            kbc = jnp.maximum(kb, 0)
            start = pl.multiple_of(kbc * tq, tq)
            k_b = kw_ref[pl.ds(start, tq), k * HEAD_DIM:(k + 1) * HEAD_DIM]
            s_b = _dot_nt(jnp.where(inside, k_b, jnp.zeros_like(k_b)), qk)
            if i == nwb - 1:
                s_b = s_b + win_diag
            elif i == 0:
                s_b = s_b + win_first
            v_pair = vt_ref[kbc // (kc // tq), KV_WIDTH + k * HEAD_DIM:KV_WIDTH + (k + 1) * HEAD_DIM, :]
            v_b = v_pair[:, 0:tq]
            for part in range(1, kc // tq):
                v_b = jnp.where(kbc % (kc // tq) == part, v_pair[:, part * tq:(part + 1) * tq], v_b)
            s_blocks.append(s_b)
            v_blocks.append(jnp.where(inside, v_b, jnp.zeros_like(v_b)))
        s_w = jnp.concatenate(s_blocks, axis=0)
        pv_w, l_w = _pv_and_sum(jnp.concatenate(v_blocks, axis=1), jnp.exp2(s_w - jnp.max(s_w, axis=0, keepdims=True)))
        o_win = pv_w * (1.0 / l_w)
        part_scr[k] = gate_row(k, 0) * o_cmp + gate_row(k, 2) * o_win

    mq = memq_ref[...]
    outs = []
    for h in range(MEM_HEADS):
        lo = h * MEM_HEAD_DIM
        s_m = _dot_nt(mk_ref[:, lo:lo + MEM_HEAD_DIM], mq[:, lo:lo + MEM_HEAD_DIM]) * MEM_HEAD_DIM ** -0.5
        pv_m, l_m = _pv_and_sum(mvt_ref[lo:lo + MEM_HEAD_DIM, :], jnp.exp(s_m - jnp.max(s_m, axis=0, keepdims=True)))
        outs.append(_transpose_tiles(pv_m * (1.0 / l_m)))
    memo_ref[...] = jnp.concatenate(outs, axis=1).astype(memo_ref.dtype)

    cur = poolc_ref[...]
    pool_scr[0:16, :] = jnp.where(qi > 0, poolp_ref[...], 0.0)
    pool_scr[16:16 + tq, :] = cur
    tpos1 = t0 + _iota((tq, POOL_GROUP_WIDTH), 0) + 1
    ys = []
    for gi, win in enumerate(POOL_WINDOWS):
        lo = gi * POOL_GROUP_WIDTH
        acc = pool_scr[16:16 + tq, lo:lo + POOL_GROUP_WIDTH]
        for sft in range(1, win):
            acc = acc + pool_scr[16 - sft:16 - sft + tq, lo:lo + POOL_GROUP_WIDTH]
        cnt = jnp.minimum(tpos1, win).astype(F32)
        d = acc / cnt - cur[:, lo:lo + POOL_GROUP_WIDTH]
        ys.append(_dot(d.astype(BF16), wpool_ref[gi]))
    pooly_ref[...] = (jnp.concatenate(ys, axis=1) * pscale_ref[...]).astype(pooly_ref.dtype)

    per = SEL_CHUNKS_PER_STEP
    span = per * kc
    n_full = t0 // span

    def sel_scores(k, sc):
        start = pl.multiple_of(sc * span, span)
        return _dot_nt(kext_ref[pl.ds(start, span), 2 * k * HEAD_DIM:(2 * k + 2) * HEAD_DIM], q_exts[k])

    def sel_values(k, sc):
        return jnp.concatenate([vt_ref[sc * per + j, k * HEAD_DIM:(k + 1) * HEAD_DIM, :] for j in range(per)], axis=1)

    def sel_step(sc, states):
        return tuple(_online_softmax_step(states[k], sel_scores(k, sc), sel_values(k, sc)) for k in range(KV_HEADS))

    states = lax.fori_loop(0, n_full, sel_step, (init,) * KV_HEADS)

    causal = (n_full * span + _iota((span, rows), 0)) <= tok_of_row((span, rows))
    head_out = []
    for k in range(KV_HEADS):
        _, l_k, acc_k = _online_softmax_step(states[k], jnp.where(causal, sel_scores(k, n_full), NEG),
                                             sel_values(k, n_full))
        o_k = part_scr[k] + gate_row(k, 1) * (acc_k * (1.0 / l_k))
        head_out += [o_k[:, g * tq:(g + 1) * tq] for g in range(GROUP)]
    pairs = [_transpose_tiles(jnp.concatenate(head_out[2 * j:2 * j + 2], axis=0)) for j in range(N_HEADS // 2)]
    onsa_ref[...] = jnp.concatenate(pairs, axis=1).astype(onsa_ref.dtype)


def _prompt_attn(q, kc, vct, kext, kw, vt, pool_u, memq, mk, mvt, ng, l, w, consts):
    b, t, _ = q.shape
    nc = t // CMP_BLOCK
    ns = t // SEL_BLOCK
    tq = TQ
    chunk = vt.shape[3]
    m = mk.shape[1]
    span = SEL_CHUNKS_PER_STEP * chunk
    assert WINDOW % tq == 0 and chunk % tq == 0 and t % span == 0 and span % tq == 0
    assert ns <= HEAD_DIM and tq % V7X_LANES == 0 and tq & (tq - 1) == 0
    tok = lambda w: pl.BlockSpec((None, tq, w), lambda i, j: (i, j, 0))
    per_b = lambda r, w: pl.BlockSpec((None, r, w), lambda i, j: (i, 0, 0))
    prev_rows = pl.BlockSpec((None, 16, POOL_WIDTH), lambda i, j: (i, jnp.maximum(j * (tq // 16) - 1, 0), 0))
    out = jax.ShapeDtypeStruct((b, t, NSA_WIDTH), BF16)
    return pl.pallas_call(
        _prompt_attn_kernel,
        grid=(b, t // tq),
        in_specs=[tok(NSA_WIDTH), per_b(nc, KV_WIDTH), per_b(KV_WIDTH, nc), per_b(t, 2 * KV_WIDTH), per_b(t, KV_WIDTH),
                  pl.BlockSpec((None, t // chunk, 2 * KV_WIDTH, chunk), lambda i, j: (i, 0, 0, 0)),
                  tok(POOL_WIDTH), prev_rows, tok(MEM_WIDTH), per_b(m, MEM_WIDTH), per_b(MEM_WIDTH, m), tok(NSAG_PAD),
                  _const_spec((ns, nc)),
                  _layer_spec(l, (POOL_GROUPS, POOL_GROUP_WIDTH, POOL_GROUP_WIDTH)), _layer_spec(l, (1, POOL_WIDTH))],
        out_specs=[tok(NSA_WIDTH), tok(POOL_WIDTH), tok(MEM_WIDTH)],
        out_shape=[out, out, out],
        scratch_shapes=[pltpu.VMEM((16 + tq, POOL_WIDTH), F32), pltpu.VMEM((KV_HEADS, HEAD_DIM, GROUP * tq), F32)],
        compiler_params=_params("parallel", "arbitrary"),
        name="prompt_attn",
    )(q, kc, vct, kext, kw, vt, pool_u, pool_u, memq, mk, mvt, ng, consts["pair_t"], w["w_pool"], w["pool_scale"])


def _merge_ffn_kernel(x_ref, onsa_ref, pooly_ref, memo_ref, mg_ref, wn_ref, wp_ref, wm_ref, wo_ref, fg_ref,
                      wgu_ref, wd_ref, y_ref):
    h = (mg_ref[:, 0:D_MODEL] * _dot(onsa_ref[...].astype(BF16), wn_ref[...])
         + mg_ref[:, D_MODEL:2 * D_MODEL] * _dot(pooly_ref[...].astype(BF16), wp_ref[...])
         + mg_ref[:, 2 * D_MODEL:3 * D_MODEL] * _dot(memo_ref[...].astype(BF16), wm_ref[...]))
    x1 = x_ref[...] + _dot(h.astype(BF16), wo_ref[...])
    hn = _rms_rows(x1, fg_ref[...]).astype(BF16)
    acc = x1
    for j in range(D_FF // FF_CHUNK):
        lo = j * FF_CHUNK
        gate = _dot(hn, wgu_ref[:, lo:lo + FF_CHUNK])
        up = _dot(hn, wgu_ref[:, D_FF + lo:D_FF + lo + FF_CHUNK])
        act = gate * jax.nn.sigmoid(gate) * up
        acc = acc + _dot(act.astype(BF16), wd_ref[lo:lo + FF_CHUNK, :])
    y_ref[...] = acc


def _merge_ffn(x, onsa, pooly, memo, mg, l, w):
    n = x.shape[0]
    lay = functools.partial(_layer_spec, l)
    tm = min(TM_PROJ, n)
    row = lambda w: pl.BlockSpec((tm, w), lambda i: (i, 0))
    return pl.pallas_call(
        _merge_ffn_kernel,
        grid=(n // tm,),
        in_specs=[row(D_MODEL), row(NSA_WIDTH), row(POOL_WIDTH), row(MEM_WIDTH), row(N_BRANCH * D_MODEL),
                  lay((NSA_WIDTH, D_MODEL)), lay((POOL_WIDTH, D_MODEL)), lay((MEM_WIDTH, D_MODEL)),
                  lay((D_MODEL, D_MODEL)), lay((1, D_MODEL)), lay((D_MODEL, 2 * D_FF)), lay((D_FF, D_MODEL))],
        out_specs=row(D_MODEL),
        out_shape=jax.ShapeDtypeStruct((n, D_MODEL), F32),
        compiler_params=_params("parallel"),
        name="merge_ffn",
    )(x, onsa, pooly, memo, mg, w["w_up_nsa"], w["w_up_pool"], w["w_up_mem"], w["w_out"], w["ffn_g"],
      w["w_gate_up"], w["w_down"])


def _page_specs(npg, half, layer_of, batch_of, step_of):
    def spec(i):
        def index_map(*args):
            pt = args[-1]
            grid = args[:-1]
            return (layer_of(grid), pt[batch_of(grid), step_of(grid) * npg + i], half, 0)
        return pl.BlockSpec((None, None, 2 * KV_WIDTH, PAGE_SIZE), index_map)
    return [spec(i) for i in range(npg)]


def _compress_pages_kernel(pt_ref, *refs):
    del pt_ref
    npg = len(refs) - 7
    pages = refs[:npg]
    pe_ref, wkt_ref, wvt_ref, kcg_ref, ones64_ref, poolm_ref, out_ref = refs[npg:]
    x = jnp.concatenate([pg[...] for pg in pages], axis=1)
    pooled = _dot(x.astype(BF16), poolm_ref[...])
    pooled = pooled + jnp.sum(pe_ref[...], axis=1, keepdims=True) * (1.0 / CMP_BLOCK)
    k = _dot_f32(wkt_ref[...], pooled[0:KV_WIDTH])
    v = _dot_f32(wvt_ref[...], pooled[KV_WIDTH:2 * KV_WIDTH])
    ssq = _dot_split_rhs(ones64_ref[...], k * k, 2)
    out_ref[0:KV_WIDTH, :] = k * lax.rsqrt(ssq * (1.0 / HEAD_DIM) + EPS) * kcg_ref[...]
    out_ref[KV_WIDTH:2 * KV_WIDTH, :] = v


def _compress_pages(cache_t, page_table, w):
    depth = cache_t.shape[0]
    bd, n_pages = page_table.shape
    npg = min(PAGES_PER_STEP_CMP, n_pages)
    per_page = PAGE_SIZE // CMP_BLOCK
    per_step = npg * per_page
    ncs = n_pages * per_page
    lconst = lambda r, c: pl.BlockSpec((None, r, c), lambda l, b, s, pt: (l, 0, 0))
    const = lambda r, c: pl.BlockSpec((r, c), lambda l, b, s, pt: (0, 0))
    grid_spec = pltpu.PrefetchScalarGridSpec(
        num_scalar_prefetch=1,
        grid=(depth, bd, n_pages // npg),
        in_specs=_page_specs(npg, 0, lambda g: g[0], lambda g: g[1], lambda g: g[2])
        + [lconst(2 * KV_WIDTH, CMP_BLOCK), lconst(KV_WIDTH, KV_WIDTH), lconst(KV_WIDTH, KV_WIDTH), lconst(KV_WIDTH, 1),
           const(KV_WIDTH, KV_WIDTH), const(npg * PAGE_SIZE, per_step)],
        out_specs=pl.BlockSpec((None, None, 2 * KV_WIDTH, per_step), lambda l, b, s, pt: (l, b, 0, s)),
    )
    rows = jnp.arange(npg * PAGE_SIZE)[:, None] // CMP_BLOCK
    poolm = jnp.where(rows == jnp.arange(per_step)[None, :], 1.0 / CMP_BLOCK, 0.0).astype(BF16)
    return pl.pallas_call(
        _compress_pages_kernel,
        grid_spec=grid_spec,
        out_shape=jax.ShapeDtypeStruct((depth, bd, 2 * KV_WIDTH, ncs), F32),
        compiler_params=_params("parallel", "parallel", "arbitrary"),
        name="compress_pages",
    )(page_table, *([cache_t] * npg), w["cmp_pe_t"], w["cmp_wk_t"], w["cmp_wv_t"], w["kc_g_col"], w["ones64_kv"], poolm)


def _rows8(row):
    return jnp.broadcast_to(row, (V7X_SUBLANES, row.shape[1]))


def _sample_attn_kernel(pt_ref, *refs, past_len, rows_per_step):
    del pt_ref
    n_page_refs = len(refs) - 25
    npg = n_page_refs // rows_per_step
    (q_all, nsakv_all, winkv_all, ng_all, poolu_all, memq_all, kcvc_all, wincache_all, poolstate_all, memcache_all,
     e_ref, pair_ref, fold_ref, unfold_ref, wpool_ref, pscale_ref,
     onsa_all, pooly_all, memo_all, winout_all,
     qbd_all, bias_all, m_all, l_all, acc_all) = refs[n_page_refs:]
    per_row = (q_all, nsakv_all, winkv_all, ng_all, poolu_all, memq_all, kcvc_all, wincache_all, poolstate_all,
               memcache_all, onsa_all, pooly_all, memo_all, winout_all, qbd_all, bias_all, m_all, l_all, acc_all)
    views = [tuple(ref.at[i] for ref in per_row) for i in range(rows_per_step)]
    row_pages = [refs[i * npg:(i + 1) * npg] for i in range(rows_per_step)]
    step = pl.program_id(1)
    n_steps = pl.num_programs(1)
    nsp = pair_ref.shape[1]
    keys_per_step = npg * PAGE_SIZE
    pos = past_len
    row8 = _iota((8, KV_WIDTH), 0)
    lane8 = _iota((8, KV_WIDTH), 1)
    head_lanes = (lane8 // HEAD_DIM) == (row8 // GROUP)

    def prologue(view):
        (q_ref, nsakv_ref, winkv_ref, ng_ref, poolu_ref, memq_ref, kcvc_ref, wincache_ref, poolstate_ref, memcache_ref,
         onsa_ref, pooly_ref, memo_ref, winout_ref, qbd_scr, bias_scr, m_scr, l_scr, acc_scr) = view
        ncs = kcvc_ref.shape[1]
        q8 = jnp.where((_iota((8, NSA_WIDTH), 1) // HEAD_DIM) == _iota((8, NSA_WIDTH), 0),
                       _rows8(q_ref[...].astype(F32)), 0.0)
        qbd = _dot(q8.astype(BF16), fold_ref[...])
        qbd_scr[...] = qbd
        s = _dot(qbd.astype(BF16), kcvc_ref[0:KV_WIDTH, :].astype(BF16))
        cmask = (_iota((8, ncs), 1) * CMP_BLOCK + (CMP_BLOCK - 1)) <= pos
        p = _masked_softmax_rows(s[None], cmask)[0]
        ocmp = _dot_nt(p.astype(BF16), kcvc_ref[KV_WIDTH:2 * KV_WIDTH, :].astype(BF16))
        acc_scr[0] = jnp.where(head_lanes, ocmp, 0.0)
        eye = _iota((nsp, nsp), 0) == _iota((nsp, nsp), 1)
        ii = _iota((nsp, nsp), 1)
        jj = _iota((nsp, nsp), 0)
        cur = pos // SEL_BLOCK
        sel_rows = []
        for k in range(KV_HEADS):
            imp = jnp.sum(p[k * GROUP:(k + 1) * GROUP], axis=0, keepdims=True)
            imp2 = _dot_split_lhs(_rows8(imp), pair_ref[...], 3)[0:1]
            blk = _iota((1, nsp), 1)
            forced = (blk == 0) | (blk == cur) | (blk == cur - 1)
            valid = blk * SEL_BLOCK <= pos
            score = jnp.where(valid, jnp.where(forced, BIG, imp2), -BIG)
            score_i = jnp.broadcast_to(score, (nsp, nsp))
            score_j = jnp.sum(jnp.where(eye, score_i, 0.0), axis=1, keepdims=True)
            beats = jnp.where((score_j > score_i) | ((score_j == score_i) & (jj < ii)), 1.0, 0.0)
            cnt = jnp.sum(beats, axis=0, keepdims=True)
            cnt = cnt + jnp.where(score < BIG, 1.0, 0.0)
            sel_rows.append(jnp.where(cnt < TOP_K, 1.0, 0.0))
        sel8 = jnp.where(_iota((8, nsp), 0) < GROUP, _rows8(sel_rows[0]), _rows8(sel_rows[1])).astype(BF16)
        for st in range(bias_scr.shape[0]):
            selx = _dot(sel8, e_ref[:, st * keys_per_step:(st + 1) * keys_per_step])
            bias_scr[st] = jnp.where(selx > 0.5, 0.0, NEG)
        m_scr[...] = jnp.full(m_scr.shape, NEG, F32)
        l_scr[...] = jnp.zeros(l_scr.shape, F32)
        acc_scr[1] = jnp.zeros((8, KV_WIDTH), F32)

    def stream(view, pages):
        qbd_scr, bias_scr, m_scr, l_scr, acc_scr = view[-5:]
        qbd_bf = qbd_scr[...].astype(BF16)
        k_t = jnp.concatenate([pg[0:KV_WIDTH, :] for pg in pages], axis=1).astype(BF16)
        v_t = jnp.concatenate([pg[KV_WIDTH:2 * KV_WIDTH, :] for pg in pages], axis=1).astype(BF16)
        s_all = _dot(qbd_bf, k_t) + bias_scr[step]
        m_old = m_scr[...]
        m_new = jnp.maximum(m_old, jnp.max(s_all, axis=-1, keepdims=True))
        p_all = jnp.exp2(s_all - m_new) * jnp.where(s_all > 0.5 * NEG, 1.0, 0.0)
        alpha = jnp.exp2(m_old - m_new)
        l_scr[...] = alpha * l_scr[...] + jnp.sum(p_all, axis=-1, keepdims=True)
        acc_scr[1] = alpha * acc_scr[1] + _dot_nt(p_all.astype(BF16), v_t)
        m_scr[...] = m_new

    def epilogue(view):
        (q_ref, nsakv_ref, winkv_ref, ng_ref, poolu_ref, memq_ref, kcvc_ref, wincache_ref, poolstate_ref, memcache_ref,
         onsa_ref, pooly_ref, memo_ref, winout_ref, qbd_scr, bias_scr, m_scr, l_scr, acc_scr) = view
        qbd = qbd_scr[...]
        qbd_bf = qbd.astype(BF16)
        ks_new = _rows8(nsakv_ref[:, 2 * KV_WIDTH:3 * KV_WIDTH])
        vs_new = _rows8(nsakv_ref[:, 3 * KV_WIDTH:4 * KV_WIDTH])
        s_new = jnp.sum(qbd * ks_new, axis=-1, keepdims=True)
        m_old = m_scr[...]
        m_fin = jnp.maximum(m_old, s_new)
        alpha = jnp.exp2(m_old - m_fin)
        p_new = jnp.exp2(s_new - m_fin)
        l_fin = alpha * l_scr[...] + p_new
        osel = (alpha * acc_scr[1] + p_new * vs_new) * (1.0 / l_fin)
        wb = wincache_ref.shape[1]
        kw_new = _rows8(winkv_ref[:, 0:KV_WIDTH])
        vw_new = _rows8(winkv_ref[:, KV_WIDTH:2 * KV_WIDTH])
        s_w = _dot(qbd_bf, wincache_ref[0:KV_WIDTH, :].astype(BF16))
        kpos = pos - wb + _iota((8, wb), 1)
        wmask = (kpos <= pos) & (kpos > pos - WINDOW)
        s_wn = jnp.sum(qbd * kw_new, axis=-1, keepdims=True)
        s_w = jnp.where(wmask, s_w, NEG)
        m_w = jnp.maximum(jnp.max(s_w, axis=-1, keepdims=True), s_wn)
        p_w = jnp.exp2(s_w - m_w) * jnp.where(wmask, 1.0, 0.0)
        p_wn = jnp.exp2(s_wn - m_w)
        l_w = jnp.sum(p_w, axis=-1, keepdims=True) + p_wn
        owin = (_dot_nt(p_w.astype(BF16), wincache_ref[KV_WIDTH:2 * KV_WIDTH, :].astype(BF16)) + p_wn * vw_new) * (1.0 / l_w)
        ng8 = _rows8(ng_ref[...])
        glane = _iota((8, NSAG_PAD), 1)
        grow = _iota((8, NSAG_PAD), 0)
        gate = lambda br: jnp.sum(jnp.where(glane == N_NSA_BRANCH * grow + br, ng8, 0.0), axis=-1, keepdims=True)
        o8 = jnp.where(head_lanes, gate(0) * acc_scr[0] + gate(1) * osel + gate(2) * owin, 0.0)
        o512 = _dot_split_lhs(o8, unfold_ref[...], 3)
        own = (_iota((8, NSA_WIDTH), 1) // HEAD_DIM) == _iota((8, NSA_WIDTH), 0)
        onsa_ref[...] = jnp.sum(jnp.where(own, o512, 0.0), axis=0, keepdims=True)
        mtok = memcache_ref.shape[0] // (2 * MEM_HEADS)
        outs = []
        for h in range(MEM_HEADS):
            k_h = memcache_ref[pl.ds(h, mtok, stride=2 * MEM_HEADS), :].astype(BF16)
            v_h = memcache_ref[pl.ds(MEM_HEADS + h, mtok, stride=2 * MEM_HEADS), :].astype(BF16)
            q_h = _rows8(memq_ref[:, h * MEM_HEAD_DIM:(h + 1) * MEM_HEAD_DIM].astype(F32)).astype(BF16)
            s_m = _dot_nt(q_h, k_h) * MEM_HEAD_DIM ** -0.5
            p_m = jnp.exp(s_m - jnp.max(s_m, axis=-1, keepdims=True))
            p_m = p_m * (1.0 / jnp.sum(p_m, axis=-1, keepdims=True))
            outs.append(_dot(p_m.astype(BF16), v_h)[0:1])
        memo_ref[...] = jnp.concatenate(outs, axis=1)
        u_new = poolu_ref[...]
        state = poolstate_ref[...]
        srow = _iota((POOL_BUF, POOL_GROUP_WIDTH), 0)
        ys = []
        for gi, win in enumerate(POOL_WINDOWS):
            lo = gi * POOL_GROUP_WIDTH
            u_g = u_new[:, lo:lo + POOL_GROUP_WIDTH]
            tail = jnp.sum(jnp.where(srow >= POOL_BUF - (win - 1), state[:, lo:lo + POOL_GROUP_WIDTH], 0.0),
                           axis=0, keepdims=True)
            d = (tail + u_g) / float(min(pos + 1, win)) - u_g
            ys.append(_dot(_rows8(d).astype(BF16), wpool_ref[gi])[0:1])
        pooly_ref[...] = jnp.concatenate(ys, axis=1) * pscale_ref[...]
        nf = 2 * KV_WIDTH
        eye_f = _iota((nf, nf), 0) == _iota((nf, nf), 1)
        new_col = jnp.sum(jnp.where(eye_f, jnp.broadcast_to(winkv_ref[...], (nf, nf)), 0.0), axis=1, keepdims=True)
        shifted = pltpu.roll(wincache_ref[...], wb - 1, axis=1)
        winout_ref[...] = jnp.where(_iota((nf, wb), 1) == wb - 1, new_col, shifted)

    @pl.when(step == 0)
    def _():
        for view in views:
            prologue(view)

    for view, pages in zip(views, row_pages):
        stream(view, pages)

    @pl.when(step == n_steps - 1)
    def _():
        for view in views:
            epilogue(view)


def _sample_attn(l, q, nsakv, winkv, ng, pool_u, memq, kcvc_s, cache_t, wincache_t, poolstate, memcache, page_table,
                 w, consts):
    bd, n_pages = page_table.shape
    npg = min(PAGES_PER_STEP_SEL, n_pages)
    n_steps = n_pages // npg
    past_len = n_pages * PAGE_SIZE
    ncs = past_len // CMP_BLOCK
    nsp = past_len // SEL_BLOCK
    wb = wincache_t.shape[3]
    mrows = memcache.shape[2]
    rb = ROWS_PER_DECODE_STEP if bd % ROWS_PER_DECODE_STEP == 0 else 1
    row = lambda w: pl.BlockSpec((rb, 1, w), lambda b, s, pt: (b, 0, 0))
    const = lambda shape: pl.BlockSpec(shape, lambda b, s, pt: (0,) * len(shape))
    per_b = lambda r, c: pl.BlockSpec((None, rb, r, c), lambda b, s, pt: (l, b, 0, 0))
    page_specs = []
    for i in range(rb):
        page_specs += _page_specs(npg, 1, lambda g: l, lambda g, i=i: g[0] * rb + i, lambda g: g[1])
    in_specs = (
        page_specs
        + [row(NSA_WIDTH), row(4 * KV_WIDTH), row(2 * KV_WIDTH), row(NSAG_PAD), row(POOL_WIDTH), row(MEM_WIDTH),
           per_b(2 * KV_WIDTH, ncs), per_b(2 * KV_WIDTH, wb), per_b(POOL_BUF, POOL_WIDTH), per_b(mrows, MEM_HEAD_DIM),
           const((nsp, past_len)), const((ncs, nsp)), const((NSA_WIDTH, KV_WIDTH)), const((KV_WIDTH, NSA_WIDTH)),
           pl.BlockSpec((None, POOL_GROUPS, POOL_GROUP_WIDTH, POOL_GROUP_WIDTH), lambda b, s, pt: (l, 0, 0, 0)),
           pl.BlockSpec((None, 1, POOL_WIDTH), lambda b, s, pt: (l, 0, 0))])
    grid_spec = pltpu.PrefetchScalarGridSpec(
        num_scalar_prefetch=1,
        grid=(bd // rb, n_steps),
        in_specs=in_specs,
        out_specs=[row(NSA_WIDTH), row(POOL_WIDTH), row(MEM_WIDTH),
                   pl.BlockSpec((rb, 2 * KV_WIDTH, wb), lambda b, s, pt: (b, 0, 0))],
        scratch_shapes=[pltpu.VMEM((rb, 8, KV_WIDTH), F32), pltpu.VMEM((rb, n_steps, 8, npg * PAGE_SIZE), F32),
                        pltpu.VMEM((rb, 8, 1), F32), pltpu.VMEM((rb, 8, 1), F32), pltpu.VMEM((rb, 2, 8, KV_WIDTH), F32)],
    )
    out = jax.ShapeDtypeStruct((bd, 1, NSA_WIDTH), F32)
    return pl.pallas_call(
        functools.partial(_sample_attn_kernel, past_len=past_len, rows_per_step=rb),
        grid_spec=grid_spec,
        out_shape=[out, out, out, jax.ShapeDtypeStruct((bd, 2 * KV_WIDTH, wb), F32)],
        compiler_params=_params("parallel", "arbitrary"),
        name="sample_attn",
    )(page_table, *([cache_t] * (rb * npg)), q, nsakv, winkv, ng, pool_u, memq, kcvc_s, wincache_t, poolstate, memcache,
      consts["sel_expand_s"], consts["pair_s"], consts["fold"], consts["unfold"], w["w_pool"], w["pool_scale"])


def _block_diag_ones(n, seg):
    return (jnp.arange(n)[:, None] // seg == jnp.arange(n)[None, :] // seg).astype(BF16)


def _constants(t, past_len):
    nc, ns = t // CMP_BLOCK, t // SEL_BLOCK
    ratio = SEL_BLOCK // CMP_BLOCK
    pair_t = (jnp.arange(ns)[:, None] == jnp.arange(nc)[None, :] // ratio).astype(BF16)
    ncs, nsp = past_len // CMP_BLOCK, past_len // SEL_BLOCK
    sel_expand_s = (jnp.arange(nsp)[:, None] == (jnp.arange(past_len) // SEL_BLOCK)[None, :]).astype(BF16)
    pair_s = (jnp.arange(ncs)[:, None] // ratio == jnp.arange(nsp)[None, :]).astype(BF16)
    c = jnp.arange(NSA_WIDTH)
    fold_col = (c // (GROUP * HEAD_DIM)) * HEAD_DIM + c % HEAD_DIM
    fold = (fold_col[:, None] == jnp.arange(KV_WIDTH)[None, :]).astype(BF16)
    return {
        "ones64": _block_diag_ones(NSA_WIDTH, HEAD_DIM), "ones128": _block_diag_ones(MEM_WIDTH, MEM_HEAD_DIM),
        "ones64_kv": _block_diag_ones(KV_WIDTH, HEAD_DIM),
        "pair_t": pair_t,
        "sel_expand_s": sel_expand_s, "pair_s": pair_s, "fold": fold, "unfold": fold.T,
    }


def kernel(x_prompt, x_sample, mem_prompt, cache_nsa_kv, cache_win_kv, state_pool, cache_mem_kv, page_table,
           attn_norm_g, w_in, nsa_q_g, nsa_kc_g, nsa_ks_g, nsa_kw_g, cmp_pe_k, cmp_pe_v, cmp_wk, cmp_wv,
           w_pool, pool_scale, mem_norm_g, w_mem_kv, mem_q_g, mem_k_g, w_up_nsa, w_up_pool, w_up_mem, w_out,
           ffn_norm_g, w_gate_up, w_down):
    depth = w_in.shape[0]
    b, t, _ = x_prompt.shape
    bd = x_sample.shape[0]
    n_pages = page_table.shape[1]
    past_len = n_pages * PAGE_SIZE
    mtok = mem_prompt.shape[1]
    wb = cache_win_kv.shape[2]
    assert x_sample.shape[1] == 1 and wb == WINDOW and past_len >= WINDOW and mtok == cache_mem_kv.shape[2]
    consts = _constants(t, past_len)

    nsag_lo = NSA_WIDTH + 6 * KV_WIDTH
    nsag_hi = nsag_lo + N_NSA_BRANCH * N_HEADS
    w_t = w_in.transpose(0, 2, 1)
    w_in_t = jnp.concatenate(
        [w_t[:, :nsag_lo], w_t[:, nsag_hi:], w_t[:, nsag_lo:nsag_hi],
         jnp.zeros((depth, NSAG_PAD - N_NSA_BRANCH * N_HEADS, D_MODEL), w_in.dtype)], axis=1).astype(BF16)
    eye_kv = jnp.eye(KV_HEADS, dtype=F32)
    rows = lambda g, reps: jnp.tile(g, (1, reps))[:, None, :].astype(F32)
    w = {
        "attn_g": attn_norm_g[:, None, :], "w_in_t": w_in_t,
        "q_g": rows(nsa_q_g, N_HEADS), "ks_g": rows(nsa_ks_g, KV_HEADS), "kw_g": rows(nsa_kw_g, KV_HEADS),
        "mq_g": rows(mem_q_g, MEM_HEADS), "kc_g": rows(nsa_kc_g, KV_HEADS), "mk_g": rows(mem_k_g, MEM_HEADS),
        "cmp_pe": jnp.concatenate([jnp.tile(cmp_pe_k, (1, 1, KV_HEADS)), jnp.tile(cmp_pe_v, (1, 1, KV_HEADS))], axis=2),
        "cmp_wk": jax.vmap(lambda m: jnp.kron(eye_kv, m))(cmp_wk), "cmp_wv": jax.vmap(lambda m: jnp.kron(eye_kv, m))(cmp_wv),
        "w_pool": w_pool.astype(BF16), "pool_scale": pool_scale[:, None, :],
        "mem_g": mem_norm_g[:, None, :], "w_mem_kv": w_mem_kv.astype(BF16),
        "w_mem_v_t": w_mem_kv[:, :, MEM_WIDTH:].transpose(0, 2, 1).astype(BF16),
        "w_up_nsa": w_up_nsa.astype(BF16), "w_up_pool": w_up_pool.astype(BF16), "w_up_mem": w_up_mem.astype(BF16),
        "w_out": w_out.astype(BF16), "ffn_g": ffn_norm_g[:, None, :],
        "w_gate_up": w_gate_up.astype(BF16), "w_down": w_down.astype(BF16),
    }
    swap = lambda a: a.transpose(0, 2, 1)
    cmp_w = {"cmp_pe_t": swap(w["cmp_pe"]), "cmp_wk_t": swap(w["cmp_wk"]), "cmp_wv_t": swap(w["cmp_wv"]),
             "kc_g_col": swap(w["kc_g"]), "ones64_kv": consts["ones64_kv"]}

    cache_t = cache_nsa_kv.transpose(0, 1, 3, 4, 5, 2).reshape(depth, cache_nsa_kv.shape[1], 4 * KV_WIDTH, PAGE_SIZE)
    wincache_t = cache_win_kv.transpose(0, 1, 3, 4, 5, 2).reshape(depth, bd, 2 * KV_WIDTH, wb)
    memcache = cache_mem_kv.reshape(depth, bd, cache_mem_kv.shape[2] * 2 * MEM_HEADS, MEM_HEAD_DIM)
    kcvc_s = _compress_pages(cache_t, page_table, cmp_w)

    xp = x_prompt.reshape(b * t, D_MODEL)
    xs = x_sample.reshape(bd, D_MODEL)
    nsa_p, nsa_s, win_p, win_s, pool_p, pool_s, mem_p = [], [], [], [], [], [], []
    for l in range(depth):
        q, nsakv, winkv, kext, kwbf, vt, pool_u, memq, mg, ng = _inproj(xp, l, w, consts, t)
        mkv, mk, mvt = _memkv(mem_prompt, l, w, consts)
        nsakv3 = nsakv.reshape(b, t, 4 * KV_WIDTH)
        kc, vct = _compress_prompt(nsakv3, l, w, consts)
        pool3 = pool_u.reshape(b, t, POOL_WIDTH)
        by_b = lambda a: a.reshape(b, t, a.shape[-1])
        onsa, pooly, memo = _prompt_attn(
            by_b(q), kc, vct, by_b(kext), by_b(kwbf), vt.reshape(b, -1, vt.shape[1], vt.shape[2]), pool3, by_b(memq),
            mk, mvt, by_b(ng), l, w, consts)
        xp = _merge_ffn(xp, onsa.reshape(b * t, NSA_WIDTH), pooly.reshape(b * t, POOL_WIDTH),
                        memo.reshape(b * t, MEM_WIDTH), mg, l, w)
        nsa_p.append(nsakv3.reshape(b, t, 4, KV_HEADS, HEAD_DIM))
        win_rows = winkv.reshape(b, t, 2 * KV_WIDTH)[:, t - min(WINDOW, t):]
        win_p.append(win_rows.reshape(b, win_rows.shape[1], 2, KV_HEADS, HEAD_DIM))
        pool_p.append(pool3[:, t - POOL_BUF:])
        mem_p.append(mkv.reshape(b, mtok, 2, MEM_HEADS, MEM_HEAD_DIM))
        q, nsakv, winkv, _, _, _, pool_u, memq, mg, ng = _inproj(xs, l, w, consts, bd)
        r3 = lambda a: a.reshape(bd, 1, a.shape[-1])
        onsa, pooly, memo, win_t = _sample_attn(l, r3(q), r3(nsakv), r3(winkv), r3(ng), r3(pool_u), r3(memq), kcvc_s,
                                                cache_t, wincache_t, state_pool, memcache, page_table, w, consts)
        xs = _merge_ffn(xs, onsa.reshape(bd, NSA_WIDTH), pooly.reshape(bd, POOL_WIDTH), memo.reshape(bd, MEM_WIDTH), mg, l, w)
        nsa_s.append(nsakv.reshape(bd, 1, 4, KV_HEADS, HEAD_DIM))
        win_s.append(win_t)
        pool_s.append(jnp.concatenate([state_pool[l][:, 1:], pool_u.reshape(bd, 1, POOL_WIDTH)], axis=1))
    win_s_out = jnp.stack(win_s).reshape(depth, bd, 2, KV_HEADS, HEAD_DIM, wb).transpose(0, 1, 5, 2, 3, 4)
    return (xp.reshape(b, t, D_MODEL), xs.reshape(bd, 1, D_MODEL), jnp.stack(nsa_p), jnp.stack(nsa_s),
            jnp.stack(win_p), win_s_out, jnp.stack(pool_p), jnp.stack(pool_s), jnp.stack(mem_p))
```

```python
import functools

import jax
import jax.numpy as jnp
from jax import lax
from jax.experimental import pallas as pl
from jax.experimental.pallas import tpu as pltpu

F32 = jnp.float32
BF16 = jnp.bfloat16

D_MODEL = 1024
PAGE_SIZE = 128
N_HEADS = 8
HEAD_DIM = 64
KV_HEADS = 2
GROUP = N_HEADS // KV_HEADS
NSA_WIDTH = N_HEADS * HEAD_DIM
KV_WIDTH = KV_HEADS * HEAD_DIM
CMP_BLOCK = 32
SEL_BLOCK = 64
TOP_K = 16
WINDOW = 512
N_NSA_BRANCH = 3
POOL_WIDTH = 512
POOL_GROUPS = 4
POOL_GROUP_WIDTH = POOL_WIDTH // POOL_GROUPS
POOL_WINDOWS = (2, 4, 8, 16)
POOL_BUF = 15
MEM_HEADS = 4
MEM_HEAD_DIM = 128
MEM_WIDTH = MEM_HEADS * MEM_HEAD_DIM
N_BRANCH = 3
D_FF = ((8 * D_MODEL // 3 + 255) // 256) * 256
EPS = 1e-6
LOG2_E = 1.4426950408889634
NEG = -1e30
BIG = 1e9

V7X_LANES = 128
V7X_SUBLANES = 8
ONES_ROWS = 2 * V7X_SUBLANES
V7X_VMEM_LIMIT_BYTES = 56 * 1024 * 1024

C_Q = 0
C_NSAKV = C_Q + NSA_WIDTH
C_WINKV = C_NSAKV + 4 * KV_WIDTH
C_POOL = C_WINKV + 2 * KV_WIDTH
C_MEMQ = C_POOL + POOL_WIDTH
C_MERGE = C_MEMQ + MEM_WIDTH
C_NSAG = C_MERGE + N_BRANCH * D_MODEL
NSAG_PAD = V7X_LANES
PROJ_PAD = C_NSAG + NSAG_PAD

TM_PROJ = 512
TQ = 512
SEL_CHUNKS_PER_STEP = 1
FF_CHUNK = D_FF // 2
PAGES_PER_STEP_CMP = 32
PAGES_PER_STEP_SEL = 32
ROWS_PER_DECODE_STEP = 2


def _dot(a, b):
    return jnp.dot(a, b, preferred_element_type=F32)


def _dot_nt(a, b):
    return lax.dot_general(a, b, (((1,), (1,)), ((), ())), preferred_element_type=F32)


def _dot_f32(a, b):
    return jnp.dot(a, b, preferred_element_type=F32, precision=lax.Precision.HIGHEST)


def _split_bf16(a, n):
    parts, r = [], a
    for _ in range(n):
        p = r.astype(BF16)
        parts.append(p)
        r = r - p.astype(F32)
    return parts


def _dot_split_lhs(a, b, n):
    return sum(_dot(p, b) for p in _split_bf16(a, n))


def _dot_split_rhs(a, b, n):
    return sum(_dot(a, p) for p in _split_bf16(b, n))


def _iota(shape, dim):
    return lax.broadcasted_iota(jnp.int32, shape, dim)


def _rms_rows(x, g):
    return x * lax.rsqrt(jnp.mean(x * x, axis=-1, keepdims=True) + EPS) * g


def _seg_rmsnorm(v, ones_bd, gain, seg):
    ssq = _dot((v * v).astype(BF16), ones_bd)
    return v * lax.rsqrt(ssq * (1.0 / seg) + EPS) * gain


def _const_spec(shape):
    return pl.BlockSpec(shape, lambda *_: (0,) * len(shape), pipeline_mode=pl.Buffered(1))


def _layer_spec(l, shape):
    return pl.BlockSpec((None,) + tuple(shape), lambda *_: (l,) + (0,) * len(shape), pipeline_mode=pl.Buffered(1))


def _params(*sem):
    return pltpu.CompilerParams(dimension_semantics=sem, vmem_limit_bytes=V7X_VMEM_LIMIT_BYTES)


def _inproj_kernel(x_ref, g_ref, w_ref, qg_ref, ksg_ref, kwg_ref, mqg_ref, ones64_ref, ones128_ref,
                   q_ref, nsakv_ref, winkv_ref, kext_ref, kwbf_ref, vt_ref, pool_ref, memq_ref, mg_ref, ng_ref,
                   *, seq_len):
    tm = x_ref.shape[0]
    hb = _rms_rows(x_ref[...], g_ref[...]).astype(BF16)

    def proj(lo, width):
        return _dot_nt(hb, w_ref[lo:lo + width, :])

    ones_kv = ones64_ref[0:KV_WIDTH, 0:KV_WIDTH]
    q = _seg_rmsnorm(proj(C_Q, NSA_WIDTH), ones64_ref[...], qg_ref[...], HEAD_DIM)
    q_ref[...] = (q * (HEAD_DIM ** -0.5 * LOG2_E)).astype(BF16)
    kcvc = proj(C_NSAKV, 2 * KV_WIDTH)
    ks = _seg_rmsnorm(proj(C_NSAKV + 2 * KV_WIDTH, KV_WIDTH), ones_kv, ksg_ref[...], HEAD_DIM)
    vs = proj(C_NSAKV + 3 * KV_WIDTH, KV_WIDTH)
    kw = _seg_rmsnorm(proj(C_WINKV, KV_WIDTH), ones_kv, kwg_ref[...], HEAD_DIM)
    vw = proj(C_WINKV + KV_WIDTH, KV_WIDTH)
    nsakv_ref[:, 0:2 * KV_WIDTH] = kcvc
    nsakv_ref[:, 2 * KV_WIDTH:3 * KV_WIDTH] = ks
    nsakv_ref[:, 3 * KV_WIDTH:4 * KV_WIDTH] = vs
    winkv_ref[:, 0:KV_WIDTH] = kw
    winkv_ref[:, KV_WIDTH:2 * KV_WIDTH] = vw
    t_seq = (pl.program_id(0) * tm + _iota((tm, HEAD_DIM), 0)) % seq_len
    onehot = jnp.where(t_seq // SEL_BLOCK == _iota((tm, HEAD_DIM), 1), 1.0, 0.0).astype(BF16)
    for k in range(KV_HEADS):
        kext_ref[:, 2 * k * HEAD_DIM:(2 * k + 1) * HEAD_DIM] = ks[:, k * HEAD_DIM:(k + 1) * HEAD_DIM].astype(BF16)
        kext_ref[:, (2 * k + 1) * HEAD_DIM:(2 * k + 2) * HEAD_DIM] = onehot
    kwbf_ref[...] = kw.astype(BF16)
    vs_lo, vw_lo = C_NSAKV + 3 * KV_WIDTH, C_WINKV + KV_WIDTH
    vt_ref[0:KV_WIDTH, :] = _dot_nt(w_ref[vs_lo:vs_lo + KV_WIDTH, :], hb).astype(BF16)
    vt_ref[KV_WIDTH:2 * KV_WIDTH, :] = _dot_nt(w_ref[vw_lo:vw_lo + KV_WIDTH, :], hb).astype(BF16)
    pool_ref[...] = proj(C_POOL, POOL_WIDTH)
    mq = _seg_rmsnorm(proj(C_MEMQ, MEM_WIDTH), ones128_ref[...], mqg_ref[...], MEM_HEAD_DIM)
    memq_ref[...] = mq.astype(BF16)
    for j in range(N_BRANCH):
        mg_ref[:, j * D_MODEL:(j + 1) * D_MODEL] = jax.nn.sigmoid(proj(C_MERGE + j * D_MODEL, D_MODEL))
    ng_ref[...] = jax.nn.sigmoid(proj(C_NSAG, NSAG_PAD))


def _inproj(x, l, w, consts, seq_len):
    n = x.shape[0]
    lay = functools.partial(_layer_spec, l)
    tm = min(TM_PROJ, n)
    row = lambda w: pl.BlockSpec((tm, w), lambda i: (i, 0))
    widths = (NSA_WIDTH, 4 * KV_WIDTH, 2 * KV_WIDTH, 2 * KV_WIDTH, KV_WIDTH, None, POOL_WIDTH, MEM_WIDTH,
              N_BRANCH * D_MODEL, NSAG_PAD)
    dtypes = (BF16, F32, F32, BF16, BF16, BF16, F32, BF16, F32, F32)
    vt_spec = pl.BlockSpec((None, 2 * KV_WIDTH, tm), lambda i: (i, 0, 0))
    vt_shape = jax.ShapeDtypeStruct((n // tm, 2 * KV_WIDTH, tm), BF16)
    return pl.pallas_call(
        functools.partial(_inproj_kernel, seq_len=seq_len),
        grid=(n // tm,),
        in_specs=[row(D_MODEL), lay((1, D_MODEL)), lay((PROJ_PAD, D_MODEL)),
                  lay((1, NSA_WIDTH)), lay((1, KV_WIDTH)), lay((1, KV_WIDTH)), lay((1, MEM_WIDTH)),
                  _const_spec((NSA_WIDTH, NSA_WIDTH)), _const_spec((MEM_WIDTH, MEM_WIDTH))],
        out_specs=[vt_spec if wd is None else row(wd) for wd in widths],
        out_shape=[vt_shape if wd is None else jax.ShapeDtypeStruct((n, wd), d) for wd, d in zip(widths, dtypes)],
        compiler_params=_params("parallel"),
        name="inproj",
    )(x, w["attn_g"], w["w_in_t"], w["q_g"], w["ks_g"], w["kw_g"], w["mq_g"], consts["ones64"], consts["ones128"])


def _memkv_kernel(m_ref, g_ref, w_ref, wvt_ref, kg_ref, ones128_ref, kv_ref, kbf_ref, vt_ref):
    hb = _rms_rows(m_ref[...], g_ref[...]).astype(BF16)
    k = _seg_rmsnorm(_dot(hb, w_ref[:, 0:MEM_WIDTH]), ones128_ref[...], kg_ref[...], MEM_HEAD_DIM)
    kv_ref[:, 0:MEM_WIDTH] = k
    kv_ref[:, MEM_WIDTH:2 * MEM_WIDTH] = _dot(hb, w_ref[:, MEM_WIDTH:2 * MEM_WIDTH])
    kbf_ref[...] = k.astype(BF16)
    vt_ref[...] = _dot_nt(wvt_ref[...], hb).astype(BF16)


def _memkv(mem, l, w, consts):
    b, mtok, _ = mem.shape
    lay = functools.partial(_layer_spec, l)
    blk = lambda r, c: pl.BlockSpec((None, r, c), lambda i: (i, 0, 0))
    return pl.pallas_call(
        _memkv_kernel,
        grid=(b,),
        in_specs=[blk(mtok, D_MODEL), lay((1, D_MODEL)), lay((D_MODEL, 2 * MEM_WIDTH)),
                  lay((MEM_WIDTH, D_MODEL)), lay((1, MEM_WIDTH)), _const_spec((MEM_WIDTH, MEM_WIDTH))],
        out_specs=[blk(mtok, 2 * MEM_WIDTH), blk(mtok, MEM_WIDTH), blk(MEM_WIDTH, mtok)],
        out_shape=[jax.ShapeDtypeStruct((b, mtok, 2 * MEM_WIDTH), F32), jax.ShapeDtypeStruct((b, mtok, MEM_WIDTH), BF16),
                   jax.ShapeDtypeStruct((b, MEM_WIDTH, mtok), BF16)],
        compiler_params=_params("parallel"),
        name="memkv",
    )(mem, w["mem_g"], w["w_mem_kv"], w["w_mem_v_t"], w["mk_g"], consts["ones128"])


def _compress_prompt_kernel(kv_ref, pe_ref, wk_ref, wv_ref, kcg_ref, ones64_ref, kc_ref, vct_ref):
    t = kv_ref.shape[0]
    nc = t // CMP_BLOCK
    pooled = jnp.sum(kv_ref[...].reshape(nc, CMP_BLOCK, 2 * KV_WIDTH), axis=1) * (1.0 / CMP_BLOCK)
    pooled = pooled + jnp.sum(pe_ref[...], axis=0, keepdims=True) * (1.0 / CMP_BLOCK)
    k = _dot_f32(pooled[:, 0:KV_WIDTH], wk_ref[...])
    v = _dot_f32(pooled[:, KV_WIDTH:2 * KV_WIDTH], wv_ref[...])
    kc_ref[...] = _seg_rmsnorm(k, ones64_ref[...], kcg_ref[...], HEAD_DIM).astype(BF16)
    vct_ref[...] = v.T.astype(BF16)


def _compress_prompt(nsakv, l, w, consts):
    b, t, _ = nsakv.shape
    nc = t // CMP_BLOCK
    return pl.pallas_call(
        _compress_prompt_kernel,
        grid=(b,),
        in_specs=[pl.BlockSpec((None, t, 2 * KV_WIDTH), lambda i: (i, 0, 0)),
                  _layer_spec(l, (CMP_BLOCK, 2 * KV_WIDTH)), _layer_spec(l, (KV_WIDTH, KV_WIDTH)),
                  _layer_spec(l, (KV_WIDTH, KV_WIDTH)), _layer_spec(l, (1, KV_WIDTH)), _const_spec((KV_WIDTH, KV_WIDTH))],
        out_specs=[pl.BlockSpec((None, nc, KV_WIDTH), lambda i: (i, 0, 0)),
                   pl.BlockSpec((None, KV_WIDTH, nc), lambda i: (i, 0, 0))],
        out_shape=[jax.ShapeDtypeStruct((b, nc, KV_WIDTH), BF16), jax.ShapeDtypeStruct((b, KV_WIDTH, nc), BF16)],
        compiler_params=_params("parallel"),
        name="compress_prompt",
    )(nsakv, w["cmp_pe"], w["cmp_wk"], w["cmp_wv"], w["kc_g"], consts["ones64_kv"])


def _masked_softmax_rows(s, mask):
    maskf = jnp.where(mask, 1.0, 0.0)
    s = jnp.where(mask[None], s, NEG)
    m = jnp.max(s, axis=-1, keepdims=True)
    p = jnp.exp2(s - m) * maskf[None]
    den = jnp.maximum(jnp.sum(p, axis=-1, keepdims=True), 1e-30)
    return p * (1.0 / den)


def _stack_heads(q, k):
    return jnp.concatenate([q[:, (k * GROUP + g) * HEAD_DIM:(k * GROUP + g + 1) * HEAD_DIM] for g in range(GROUP)], axis=0)


def _transpose_tiles(x):
    r, c = x.shape
    n = V7X_LANES
    return jnp.concatenate(
        [jnp.concatenate([x[i * n:(i + 1) * n, j * n:(j + 1) * n].T for i in range(r // n)], axis=1) for j in range(c // n)],
        axis=0)


def _topk_select_t(score_t, n_sel):
    ns, tq = score_t.shape
    sub = V7X_SUBLANES
    tiles = [score_t[v * sub:(v + 1) * sub] for v in range(ns // sub)]
    row = _iota((sub, tq), 0)
    cnts = [jnp.zeros((sub, tq), F32) for _ in tiles]
    for j in range(ns):
        r = jnp.broadcast_to(score_t[j:j + 1, :], (sub, tq))
        for v, tile in enumerate(tiles):
            if v * sub > j:
                wins = r >= tile
            elif v * sub + sub - 1 < j:
                wins = r > tile
            else:
                wins = jnp.where(row + v * sub > j, jnp.where(r >= tile, 1.0, 0.0), jnp.where(r > tile, 1.0, 0.0)) > 0.5
            cnts[v] = cnts[v] + jnp.where(wins, 1.0, 0.0)
    return jnp.where(jnp.concatenate(cnts, axis=0) < n_sel, 1.0, 0.0)


def _with_ones_rows(v_t):
    return jnp.concatenate([v_t, jnp.ones((ONES_ROWS, v_t.shape[1]), BF16)], axis=0)


def _pv_and_sum(v_t, p):
    d = v_t.shape[0]
    r = _dot(_with_ones_rows(v_t), p.astype(BF16))
    return r[0:d], r[d:d + 1]


def _online_softmax_step(state, s_t, v_t):
    m, l, acc = state
    m_new = jnp.maximum(m, jnp.max(s_t, axis=0, keepdims=True))
    alpha = jnp.exp2(m - m_new)
    pv, psum = _pv_and_sum(v_t, jnp.exp2(s_t - m_new))
    return m_new, alpha * l + psum, alpha * acc + pv


def _merge_softmax_states(states):
    m = states[0][0]
    for st in states[1:]:
        m = jnp.maximum(m, st[0])
    l, acc = 0.0, 0.0
    for m_i, l_i, acc_i in states:
        w = jnp.exp2(m_i - m)
        l, acc = l + w * l_i, acc + w * acc_i
    return m, l, acc


def _prompt_attn_kernel(q_ref, kc_ref, vct_ref, kext_ref, kw_ref, vt_ref, poolc_ref, poolp_ref, memq_ref, mk_ref, mvt_ref,
                        ng_ref, pairt_ref, wpool_ref, pscale_ref,
                        onsa_ref, pooly_ref, memo_ref, pool_scr, part_scr):
    tq = q_ref.shape[0]
    nc = kc_ref.shape[0]
    ns = pairt_ref.shape[0]
    kc = vt_ref.shape[2]
    rows = GROUP * tq
    n_sel = min(TOP_K, ns)
    qi = pl.program_id(1)
    t0 = qi * tq
    q = q_ref[...]

    eye_tq = jnp.where(_iota((tq, tq), 0) == _iota((tq, tq), 1), 1.0, 0.0).astype(BF16)
    blk_t = _iota((ns, tq), 0)
    tpos_t = t0 + _iota((ns, tq), 1)
    cur_t = tpos_t // SEL_BLOCK
    valid_t = blk_t * SEL_BLOCK <= tpos_t
    forced_t = (blk_t == 0) | (blk_t == cur_t) | (blk_t == cur_t - 1)
    tok_of_row = lambda shape: t0 + (_iota(shape, 1) & (tq - 1))
    cmask = (_iota((nc, rows), 0) * CMP_BLOCK + (CMP_BLOCK - 1)) <= tok_of_row((nc, rows))
    cmaskf = jnp.where(cmask, 1.0, 0.0)
    ng_t = _transpose_tiles(ng_ref[...])
    nwb = WINDOW // tq + 1
    rel = _iota((tq, rows), 0) - (_iota((tq, rows), 1) & (tq - 1))
    win_first = jnp.where(rel > 0, 0.0, NEG)
    win_diag = jnp.where(rel <= 0, 0.0, NEG)
    init = (jnp.full((1, rows), NEG, F32), jnp.zeros((1, rows), F32), jnp.zeros((HEAD_DIM, rows), F32))

    def gate_row(k, br):
        idx = [N_NSA_BRANCH * (k * GROUP + g) + br for g in range(GROUP)]
        return jnp.concatenate([ng_t[i:i + 1, :] for i in idx], axis=1)

    q_exts = []
    for k in range(KV_HEADS):
        qk = _stack_heads(q, k)
        s_c = jnp.where(cmask, _dot_nt(kc_ref[:, k * HEAD_DIM:(k + 1) * HEAD_DIM], qk), NEG)
        p = jnp.exp2(s_c - jnp.max(s_c, axis=0, keepdims=True)) * cmaskf
        p = p * (1.0 / jnp.maximum(jnp.sum(p, axis=0, keepdims=True), 1e-30))
        o_cmp = _dot(vct_ref[k * HEAD_DIM:(k + 1) * HEAD_DIM, :], p.astype(BF16))
        imp_t = p[:, 0:tq]
        for g in range(1, GROUP):
            imp_t = imp_t + p[:, g * tq:(g + 1) * tq]
        imp2_t = _dot_split_rhs(pairt_ref[...], imp_t, 3)
        score_t = jnp.where(valid_t, jnp.where(forced_t, BIG, imp2_t), -BIG)
        sel_t = _topk_select_t(score_t, n_sel)
        sel = _dot_nt(eye_tq, sel_t.astype(BF16))
        selneg = jnp.where(sel > 0.5, 0.0, NEG).astype(BF16)
        if ns < HEAD_DIM:
            selneg = jnp.concatenate([selneg, jnp.zeros((tq, HEAD_DIM - ns), BF16)], axis=1)
        q_exts.append(jnp.concatenate([qk, jnp.concatenate([selneg] * GROUP, axis=0)], axis=1))
        s_blocks, v_blocks = [], []
        for i in range(nwb):
            kb = qi - (nwb - 1) + i
            inside = kb >= 0
            kbc = jnp.maximum(kb, 0)
            start = pl.multiple_of(kbc * tq, tq)
            k_b = kw_ref[pl.ds(start, tq), k * HEAD_DIM:(k + 1) * HEAD_DIM]
            s_b = _dot_nt(jnp.where(inside, k_b, jnp.zeros_like(k_b)), qk)
            if i == nwb - 1:
                s_b = s_b + win_diag
            elif i == 0:
                s_b = s_b + win_first
            v_pair = vt_ref[kbc // (kc // tq), KV_WIDTH + k * HEAD_DIM:KV_WIDTH + (k + 1) * HEAD_DIM, :]
            v_b = v_pair[:, 0:tq]
            for part in range(1, kc // tq):
                v_b = jnp.where(kbc % (kc // tq) == part, v_pair[:, part * tq:(part + 1) * tq], v_b)
            s_blocks.append(s_b)
            v_blocks.append(jnp.where(inside, v_b, jnp.zeros_like(v_b)))
        s_w = jnp.concatenate(s_blocks, axis=0)
        pv_w, l_w = _pv_and_sum(jnp.concatenate(v_blocks, axis=1), jnp.exp2(s_w - jnp.max(s_w, axis=0, keepdims=True)))
        o_win = pv_w * (1.0 / l_w)
        part_scr[k] = gate_row(k, 0) * o_cmp + gate_row(k, 2) * o_win

    mq = memq_ref[...]
    outs = []
    for h in range(MEM_HEADS):
        lo = h * MEM_HEAD_DIM
        s_m = _dot_nt(mk_ref[:, lo:lo + MEM_HEAD_DIM], mq[:, lo:lo + MEM_HEAD_DIM]) * MEM_HEAD_DIM ** -0.5
        pv_m, l_m = _pv_and_sum(mvt_ref[lo:lo + MEM_HEAD_DIM, :], jnp.exp(s_m - jnp.max(s_m, axis=0, keepdims=True)))
        outs.append(_transpose_tiles(pv_m * (1.0 / l_m)))
    memo_ref[...] = jnp.concatenate(outs, axis=1).astype(memo_ref.dtype)

    cur = poolc_ref[...]
    pool_scr[0:16, :] = jnp.where(qi > 0, poolp_ref[...], 0.0)
    pool_scr[16:16 + tq, :] = cur
    tpos1 = t0 + _iota((tq, POOL_GROUP_WIDTH), 0) + 1
    ys = []
    for gi, win in enumerate(POOL_WINDOWS):
        lo = gi * POOL_GROUP_WIDTH
        acc = pool_scr[16:16 + tq, lo:lo + POOL_GROUP_WIDTH]
        for sft in range(1, win):
            acc = acc + pool_scr[16 - sft:16 - sft + tq, lo:lo + POOL_GROUP_WIDTH]
        cnt = jnp.minimum(tpos1, win).astype(F32)
        d = acc / cnt - cur[:, lo:lo + POOL_GROUP_WIDTH]
        ys.append(_dot(d.astype(BF16), wpool_ref[gi]))
    pooly_ref[...] = (jnp.concatenate(ys, axis=1) * pscale_ref[...]).astype(pooly_ref.dtype)

    per = SEL_CHUNKS_PER_STEP
    span = per * kc
    n_full = t0 // span

    def sel_scores(k, sc):
        start = pl.multiple_of(sc * span, span)
        return _dot_nt(kext_ref[pl.ds(start, span), 2 * k * HEAD_DIM:(2 * k + 2) * HEAD_DIM], q_exts[k])

    def sel_values(k, sc):
        return jnp.concatenate([vt_ref[sc * per + j, k * HEAD_DIM:(k + 1) * HEAD_DIM, :] for j in range(per)], axis=1)

    def sel_step(sc, states):
        return tuple(_online_softmax_step(states[k], sel_scores(k, sc), sel_values(k, sc)) for k in range(KV_HEADS))

    states = lax.fori_loop(0, n_full, sel_step, (init,) * KV_HEADS)

    causal = (n_full * span + _iota((span, rows), 0)) <= tok_of_row((span, rows))
    head_out = []
    for k in range(KV_HEADS):
        _, l_k, acc_k = _online_softmax_step(states[k], jnp.where(causal, sel_scores(k, n_full), NEG),
                                             sel_values(k, n_full))
        o_k = part_scr[k] + gate_row(k, 1) * (acc_k * (1.0 / l_k))
        head_out += [o_k[:, g * tq:(g + 1) * tq] for g in range(GROUP)]
    pairs = [_transpose_tiles(jnp.concatenate(head_out[2 * j:2 * j + 2], axis=0)) for j in range(N_HEADS // 2)]
    onsa_ref[...] = jnp.concatenate(pairs, axis=1).astype(onsa_ref.dtype)


def _prompt_attn(q, kc, vct, kext, kw, vt, pool_u, memq, mk, mvt, ng, l, w, consts):
    b, t, _ = q.shape
    nc = t // CMP_BLOCK
    ns = t // SEL_BLOCK
    tq = TQ
    chunk = vt.shape[3]
    m = mk.shape[1]
    span = SEL_CHUNKS_PER_STEP * chunk
    assert WINDOW % tq == 0 and chunk % tq == 0 and t % span == 0 and span % tq == 0
    assert ns <= HEAD_DIM and tq % V7X_LANES == 0 and tq & (tq - 1) == 0
    tok = lambda w: pl.BlockSpec((None, tq, w), lambda i, j: (i, j, 0))
    per_b = lambda r, w: pl.BlockSpec((None, r, w), lambda i, j: (i, 0, 0))
    prev_rows = pl.BlockSpec((None, 16, POOL_WIDTH), lambda i, j: (i, jnp.maximum(j * (tq // 16) - 1, 0), 0))
    out = jax.ShapeDtypeStruct((b, t, NSA_WIDTH), BF16)
    return pl.pallas_call(
        _prompt_attn_kernel,
        grid=(b, t // tq),
        in_specs=[tok(NSA_WIDTH), per_b(nc, KV_WIDTH), per_b(KV_WIDTH, nc), per_b(t, 2 * KV_WIDTH), per_b(t, KV_WIDTH),
                  pl.BlockSpec((None, t // chunk, 2 * KV_WIDTH, chunk), lambda i, j: (i, 0, 0, 0)),
                  tok(POOL_WIDTH), prev_rows, tok(MEM_WIDTH), per_b(m, MEM_WIDTH), per_b(MEM_WIDTH, m), tok(NSAG_PAD),
                  _const_spec((ns, nc)),
                  _layer_spec(l, (POOL_GROUPS, POOL_GROUP_WIDTH, POOL_GROUP_WIDTH)), _layer_spec(l, (1, POOL_WIDTH))],
        out_specs=[tok(NSA_WIDTH), tok(POOL_WIDTH), tok(MEM_WIDTH)],
        out_shape=[out, out, out],
        scratch_shapes=[pltpu.VMEM((16 + tq, POOL_WIDTH), F32), pltpu.VMEM((KV_HEADS, HEAD_DIM, GROUP * tq), F32)],
        compiler_params=_params("parallel", "arbitrary"),
        name="prompt_attn",
    )(q, kc, vct, kext, kw, vt, pool_u, pool_u, memq, mk, mvt, ng, consts["pair_t"], w["w_pool"], w["pool_scale"])


def _merge_ffn_kernel(x_ref, onsa_ref, pooly_ref, memo_ref, mg_ref, wn_ref, wp_ref, wm_ref, wo_ref, fg_ref,
                      wgu_ref, wd_ref, y_ref):
    h = (mg_ref[:, 0:D_MODEL] * _dot(onsa_ref[...].astype(BF16), wn_ref[...])
         + mg_ref[:, D_MODEL:2 * D_MODEL] * _dot(pooly_ref[...].astype(BF16), wp_ref[...])
         + mg_ref[:, 2 * D_MODEL:3 * D_MODEL] * _dot(memo_ref[...].astype(BF16), wm_ref[...]))
    x1 = x_ref[...] + _dot(h.astype(BF16), wo_ref[...])
    hn = _rms_rows(x1, fg_ref[...]).astype(BF16)
    acc = x1
    for j in range(D_FF // FF_CHUNK):
        lo = j * FF_CHUNK
        gate = _dot(hn, wgu_ref[:, lo:lo + FF_CHUNK])
        up = _dot(hn, wgu_ref[:, D_FF + lo:D_FF + lo + FF_CHUNK])
        act = gate * jax.nn.sigmoid(gate) * up
        acc = acc + _dot(act.astype(BF16), wd_ref[lo:lo + FF_CHUNK, :])
    y_ref[...] = acc


def _merge_ffn(x, onsa, pooly, memo, mg, l, w):
    n = x.shape[0]
    lay = functools.partial(_layer_spec, l)
    tm = min(TM_PROJ, n)
    row = lambda w: pl.BlockSpec((tm, w), lambda i: (i, 0))
    return pl.pallas_call(
        _merge_ffn_kernel,
        grid=(n // tm,),
        in_specs=[row(D_MODEL), row(NSA_WIDTH), row(POOL_WIDTH), row(MEM_WIDTH), row(N_BRANCH * D_MODEL),
                  lay((NSA_WIDTH, D_MODEL)), lay((POOL_WIDTH, D_MODEL)), lay((MEM_WIDTH, D_MODEL)),
                  lay((D_MODEL, D_MODEL)), lay((1, D_MODEL)), lay((D_MODEL, 2 * D_FF)), lay((D_FF, D_MODEL))],
        out_specs=row(D_MODEL),
        out_shape=jax.ShapeDtypeStruct((n, D_MODEL), F32),
        compiler_params=_params("parallel"),
        name="merge_ffn",
    )(x, onsa, pooly, memo, mg, w["w_up_nsa"], w["w_up_pool"], w["w_up_mem"], w["w_out"], w["ffn_g"],
      w["w_gate_up"], w["w_down"])


def _page_specs(npg, half, layer_of, batch_of, step_of):
    def spec(i):
        def index_map(*args):
            pt = args[-1]
            grid = args[:-1]
            return (layer_of(grid), pt[batch_of(grid), step_of(grid) * npg + i], half, 0)
        return pl.BlockSpec((None, None, 2 * KV_WIDTH, PAGE_SIZE), index_map)
    return [spec(i) for i in range(npg)]


def _compress_pages_kernel(pt_ref, *refs):
    del pt_ref
    npg = len(refs) - 7
    pages = refs[:npg]
    pe_ref, wkt_ref, wvt_ref, kcg_ref, ones64_ref, poolm_ref, out_ref = refs[npg:]
    x = jnp.concatenate([pg[...] for pg in pages], axis=1)
    pooled = _dot(x.astype(BF16), poolm_ref[...])
    pooled = pooled + jnp.sum(pe_ref[...], axis=1, keepdims=True) * (1.0 / CMP_BLOCK)
    k = _dot_f32(wkt_ref[...], pooled[0:KV_WIDTH])
    v = _dot_f32(wvt_ref[...], pooled[KV_WIDTH:2 * KV_WIDTH])
    ssq = _dot_split_rhs(ones64_ref[...], k * k, 2)
    out_ref[0:KV_WIDTH, :] = k * lax.rsqrt(ssq * (1.0 / HEAD_DIM) + EPS) * kcg_ref[...]
    out_ref[KV_WIDTH:2 * KV_WIDTH, :] = v


def _compress_pages(cache_t, page_table, w):
    depth = cache_t.shape[0]
    bd, n_pages = page_table.shape
    npg = min(PAGES_PER_STEP_CMP, n_pages)
    per_page = PAGE_SIZE // CMP_BLOCK
    per_step = npg * per_page
    ncs = n_pages * per_page
    lconst = lambda r, c: pl.BlockSpec((None, r, c), lambda l, b, s, pt: (l, 0, 0))
    const = lambda r, c: pl.BlockSpec((r, c), lambda l, b, s, pt: (0, 0))
    grid_spec = pltpu.PrefetchScalarGridSpec(
        num_scalar_prefetch=1,
        grid=(depth, bd, n_pages // npg),
        in_specs=_page_specs(npg, 0, lambda g: g[0], lambda g: g[1], lambda g: g[2])
        + [lconst(2 * KV_WIDTH, CMP_BLOCK), lconst(KV_WIDTH, KV_WIDTH), lconst(KV_WIDTH, KV_WIDTH), lconst(KV_WIDTH, 1),
           const(KV_WIDTH, KV_WIDTH), const(npg * PAGE_SIZE, per_step)],
        out_specs=pl.BlockSpec((None, None, 2 * KV_WIDTH, per_step), lambda l, b, s, pt: (l, b, 0, s)),
    )
    rows = jnp.arange(npg * PAGE_SIZE)[:, None] // CMP_BLOCK
    poolm = jnp.where(rows == jnp.arange(per_step)[None, :], 1.0 / CMP_BLOCK, 0.0).astype(BF16)
    return pl.pallas_call(
        _compress_pages_kernel,
        grid_spec=grid_spec,
        out_shape=jax.ShapeDtypeStruct((depth, bd, 2 * KV_WIDTH, ncs), F32),
        compiler_params=_params("parallel", "parallel", "arbitrary"),
        name="compress_pages",
    )(page_table, *([cache_t] * npg), w["cmp_pe_t"], w["cmp_wk_t"], w["cmp_wv_t"], w["kc_g_col"], w["ones64_kv"], poolm)


def _rows8(row):
    return jnp.broadcast_to(row, (V7X_SUBLANES, row.shape[1]))


def _sample_attn_kernel(pt_ref, *refs, past_len, rows_per_step):
    del pt_ref
    n_page_refs = len(refs) - 25
    npg = n_page_refs // rows_per_step
    (q_all, nsakv_all, winkv_all, ng_all, poolu_all, memq_all, kcvc_all, wincache_all, poolstate_all, memcache_all,
     e_ref, pair_ref, fold_ref, unfold_ref, wpool_ref, pscale_ref,
     onsa_all, pooly_all, memo_all, winout_all,
     qbd_all, bias_all, m_all, l_all, acc_all) = refs[n_page_refs:]
    per_row = (q_all, nsakv_all, winkv_all, ng_all, poolu_all, memq_all, kcvc_all, wincache_all, poolstate_all,
               memcache_all, onsa_all, pooly_all, memo_all, winout_all, qbd_all, bias_all, m_all, l_all, acc_all)
    views = [tuple(ref.at[i] for ref in per_row) for i in range(rows_per_step)]
    row_pages = [refs[i * npg:(i + 1) * npg] for i in range(rows_per_step)]
    step = pl.program_id(1)
    n_steps = pl.num_programs(1)
    nsp = pair_ref.shape[1]
    keys_per_step = npg * PAGE_SIZE
    pos = past_len
    row8 = _iota((8, KV_WIDTH), 0)
    lane8 = _iota((8, KV_WIDTH), 1)
    head_lanes = (lane8 // HEAD_DIM) == (row8 // GROUP)

    def prologue(view):
        (q_ref, nsakv_ref, winkv_ref, ng_ref, poolu_ref, memq_ref, kcvc_ref, wincache_ref, poolstate_ref, memcache_ref,
         onsa_ref, pooly_ref, memo_ref, winout_ref, qbd_scr, bias_scr, m_scr, l_scr, acc_scr) = view
        ncs = kcvc_ref.shape[1]
        q8 = jnp.where((_iota((8, NSA_WIDTH), 1) // HEAD_DIM) == _iota((8, NSA_WIDTH), 0),
                       _rows8(q_ref[...].astype(F32)), 0.0)
        qbd = _dot(q8.astype(BF16), fold_ref[...])
        qbd_scr[...] = qbd
        s = _dot(qbd.astype(BF16), kcvc_ref[0:KV_WIDTH, :].astype(BF16))
        cmask = (_iota((8, ncs), 1) * CMP_BLOCK + (CMP_BLOCK - 1)) <= pos
        p = _masked_softmax_rows(s[None], cmask)[0]
        ocmp = _dot_nt(p.astype(BF16), kcvc_ref[KV_WIDTH:2 * KV_WIDTH, :].astype(BF16))
        acc_scr[0] = jnp.where(head_lanes, ocmp, 0.0)
        eye = _iota((nsp, nsp), 0) == _iota((nsp, nsp), 1)
        ii = _iota((nsp, nsp), 1)
        jj = _iota((nsp, nsp), 0)
        cur = pos // SEL_BLOCK
        sel_rows = []
        for k in range(KV_HEADS):
            imp = jnp.sum(p[k * GROUP:(k + 1) * GROUP], axis=0, keepdims=True)
            imp2 = _dot_split_lhs(_rows8(imp), pair_ref[...], 3)[0:1]
            blk = _iota((1, nsp), 1)
            forced = (blk == 0) | (blk == cur) | (blk == cur - 1)
            valid = blk * SEL_BLOCK <= pos
            score = jnp.where(valid, jnp.where(forced, BIG, imp2), -BIG)
            score_i = jnp.broadcast_to(score, (nsp, nsp))
            score_j = jnp.sum(jnp.where(eye, score_i, 0.0), axis=1, keepdims=True)
            beats = jnp.where((score_j > score_i) | ((score_j == score_i) & (jj < ii)), 1.0, 0.0)
            cnt = jnp.sum(beats, axis=0, keepdims=True)
            cnt = cnt + jnp.where(score < BIG, 1.0, 0.0)
            sel_rows.append(jnp.where(cnt < TOP_K, 1.0, 0.0))
        sel8 = jnp.where(_iota((8, nsp), 0) < GROUP, _rows8(sel_rows[0]), _rows8(sel_rows[1])).astype(BF16)
        for st in range(bias_scr.shape[0]):
            selx = _dot(sel8, e_ref[:, st * keys_per_step:(st + 1) * keys_per_step])
            bias_scr[st] = jnp.where(selx > 0.5, 0.0, NEG)
        m_scr[...] = jnp.full(m_scr.shape, NEG, F32)
        l_scr[...] = jnp.zeros(l_scr.shape, F32)
        acc_scr[1] = jnp.zeros((8, KV_WIDTH), F32)

    def stream(view, pages):
        qbd_scr, bias_scr, m_scr, l_scr, acc_scr = view[-5:]
        qbd_bf = qbd_scr[...].astype(BF16)
        k_t = jnp.concatenate([pg[0:KV_WIDTH, :] for pg in pages], axis=1).astype(BF16)
        v_t = jnp.concatenate([pg[KV_WIDTH:2 * KV_WIDTH, :] for pg in pages], axis=1).astype(BF16)
        s_all = _dot(qbd_bf, k_t) + bias_scr[step]
        m_old = m_scr[...]
        m_new = jnp.maximum(m_old, jnp.max(s_all, axis=-1, keepdims=True))
        p_all = jnp.exp2(s_all - m_new) * jnp.where(s_all > 0.5 * NEG, 1.0, 0.0)
        alpha = jnp.exp2(m_old - m_new)
        l_scr[...] = alpha * l_scr[...] + jnp.sum(p_all, axis=-1, keepdims=True)
        acc_scr[1] = alpha * acc_scr[1] + _dot_nt(p_all.astype(BF16), v_t)
        m_scr[...] = m_new

    def epilogue(view):
        (q_ref, nsakv_ref, winkv_ref, ng_ref, poolu_ref, memq_ref, kcvc_ref, wincache_ref, poolstate_ref, memcache_ref,
         onsa_ref, pooly_ref, memo_ref, winout_ref, qbd_scr, bias_scr, m_scr, l_scr, acc_scr) = view
        qbd = qbd_scr[...]
        qbd_bf = qbd.astype(BF16)
        ks_new = _rows8(nsakv_ref[:, 2 * KV_WIDTH:3 * KV_WIDTH])
        vs_new = _rows8(nsakv_ref[:, 3 * KV_WIDTH:4 * KV_WIDTH])
        s_new = jnp.sum(qbd * ks_new, axis=-1, keepdims=True)
        m_old = m_scr[...]
        m_fin = jnp.maximum(m_old, s_new)
        alpha = jnp.exp2(m_old - m_fin)
        p_new = jnp.exp2(s_new - m_fin)
        l_fin = alpha * l_scr[...] + p_new
        osel = (alpha * acc_scr[1] + p_new * vs_new) * (1.0 / l_fin)
        wb = wincache_ref.shape[1]
        kw_new = _rows8(winkv_ref[:, 0:KV_WIDTH])
        vw_new = _rows8(winkv_ref[:, KV_WIDTH:2 * KV_WIDTH])
        s_w = _dot(qbd_bf, wincache_ref[0:KV_WIDTH, :].astype(BF16))
        kpos = pos - wb + _iota((8, wb), 1)
        wmask = (kpos <= pos) & (kpos > pos - WINDOW)
        s_wn = jnp.sum(qbd * kw_new, axis=-1, keepdims=True)
        s_w = jnp.where(wmask, s_w, NEG)
        m_w = jnp.maximum(jnp.max(s_w, axis=-1, keepdims=True), s_wn)
        p_w = jnp.exp2(s_w - m_w) * jnp.where(wmask, 1.0, 0.0)
        p_wn = jnp.exp2(s_wn - m_w)
        l_w = jnp.sum(p_w, axis=-1, keepdims=True) + p_wn
        owin = (_dot_nt(p_w.astype(BF16), wincache_ref[KV_WIDTH:2 * KV_WIDTH, :].astype(BF16)) + p_wn * vw_new) * (1.0 / l_w)
        ng8 = _rows8(ng_ref[...])
        glane = _iota((8, NSAG_PAD), 1)
        grow = _iota((8, NSAG_PAD), 0)
        gate = lambda br: jnp.sum(jnp.where(glane == N_NSA_BRANCH * grow + br, ng8, 0.0), axis=-1, keepdims=True)
        o8 = jnp.where(head_lanes, gate(0) * acc_scr[0] + gate(1) * osel + gate(2) * owin, 0.0)
        o512 = _dot_split_lhs(o8, unfold_ref[...], 3)
        own = (_iota((8, NSA_WIDTH), 1) // HEAD_DIM) == _iota((8, NSA_WIDTH), 0)
        onsa_ref[...] = jnp.sum(jnp.where(own, o512, 0.0), axis=0, keepdims=True)
        mtok = memcache_ref.shape[0] // (2 * MEM_HEADS)
        outs = []
        for h in range(MEM_HEADS):
            k_h = memcache_ref[pl.ds(h, mtok, stride=2 * MEM_HEADS), :].astype(BF16)
            v_h = memcache_ref[pl.ds(MEM_HEADS + h, mtok, stride=2 * MEM_HEADS), :].astype(BF16)
            q_h = _rows8(memq_ref[:, h * MEM_HEAD_DIM:(h + 1) * MEM_HEAD_DIM].astype(F32)).astype(BF16)
            s_m = _dot_nt(q_h, k_h) * MEM_HEAD_DIM ** -0.5
            p_m = jnp.exp(s_m - jnp.max(s_m, axis=-1, keepdims=True))
            p_m = p_m * (1.0 / jnp.sum(p_m, axis=-1, keepdims=True))
            outs.append(_dot(p_m.astype(BF16), v_h)[0:1])
        memo_ref[...] = jnp.concatenate(outs, axis=1)
        u_new = poolu_ref[...]
        state = poolstate_ref[...]
        srow = _iota((POOL_BUF, POOL_GROUP_WIDTH), 0)
        ys = []
        for gi, win in enumerate(POOL_WINDOWS):
            lo = gi * POOL_GROUP_WIDTH
            u_g = u_new[:, lo:lo + POOL_GROUP_WIDTH]
            tail = jnp.sum(jnp.where(srow >= POOL_BUF - (win - 1), state[:, lo:lo + POOL_GROUP_WIDTH], 0.0),
                           axis=0, keepdims=True)
            d = (tail + u_g) / float(min(pos + 1, win)) - u_g
            ys.append(_dot(_rows8(d).astype(BF16), wpool_ref[gi])[0:1])
        pooly_ref[...] = jnp.concatenate(ys, axis=1) * pscale_ref[...]
        nf = 2 * KV_WIDTH
        eye_f = _iota((nf, nf), 0) == _iota((nf, nf), 1)
        new_col = jnp.sum(jnp.where(eye_f, jnp.broadcast_to(winkv_ref[...], (nf, nf)), 0.0), axis=1, keepdims=True)
        shifted = pltpu.roll(wincache_ref[...], wb - 1, axis=1)
        winout_ref[...] = jnp.where(_iota((nf, wb), 1) == wb - 1, new_col, shifted)

    @pl.when(step == 0)
    def _():
        for view in views:
            prologue(view)

    for view, pages in zip(views, row_pages):
        stream(view, pages)

    @pl.when(step == n_steps - 1)
    def _():
        for view in views:
            epilogue(view)


def _sample_attn(l, q, nsakv, winkv, ng, pool_u, memq, kcvc_s, cache_t, wincache_t, poolstate, memcache, page_table,
                 w, consts):
    bd, n_pages = page_table.shape
    npg = min(PAGES_PER_STEP_SEL, n_pages)
    n_steps = n_pages // npg
    past_len = n_pages * PAGE_SIZE
    ncs = past_len // CMP_BLOCK
    nsp = past_len // SEL_BLOCK
    wb = wincache_t.shape[3]
    mrows = memcache.shape[2]
    rb = ROWS_PER_DECODE_STEP if bd % ROWS_PER_DECODE_STEP == 0 else 1
    row = lambda w: pl.BlockSpec((rb, 1, w), lambda b, s, pt: (b, 0, 0))
    const = lambda shape: pl.BlockSpec(shape, lambda b, s, pt: (0,) * len(shape))
    per_b = lambda r, c: pl.BlockSpec((None, rb, r, c), lambda b, s, pt: (l, b, 0, 0))
    page_specs = []
    for i in range(rb):
        page_specs += _page_specs(npg, 1, lambda g: l, lambda g, i=i: g[0] * rb + i, lambda g: g[1])
    in_specs = (
        page_specs
        + [row(NSA_WIDTH), row(4 * KV_WIDTH), row(2 * KV_WIDTH), row(NSAG_PAD), row(POOL_WIDTH), row(MEM_WIDTH),
           per_b(2 * KV_WIDTH, ncs), per_b(2 * KV_WIDTH, wb), per_b(POOL_BUF, POOL_WIDTH), per_b(mrows, MEM_HEAD_DIM),
           const((nsp, past_len)), const((ncs, nsp)), const((NSA_WIDTH, KV_WIDTH)), const((KV_WIDTH, NSA_WIDTH)),
           pl.BlockSpec((None, POOL_GROUPS, POOL_GROUP_WIDTH, POOL_GROUP_WIDTH), lambda b, s, pt: (l, 0, 0, 0)),
           pl.BlockSpec((None, 1, POOL_WIDTH), lambda b, s, pt: (l, 0, 0))])
    grid_spec = pltpu.PrefetchScalarGridSpec(
        num_scalar_prefetch=1,
        grid=(bd // rb, n_steps),
        in_specs=in_specs,
        out_specs=[row(NSA_WIDTH), row(POOL_WIDTH), row(MEM_WIDTH),
                   pl.BlockSpec((rb, 2 * KV_WIDTH, wb), lambda b, s, pt: (b, 0, 0))],
        scratch_shapes=[pltpu.VMEM((rb, 8, KV_WIDTH), F32), pltpu.VMEM((rb, n_steps, 8, npg * PAGE_SIZE), F32),
                        pltpu.VMEM((rb, 8, 1), F32), pltpu.VMEM((rb, 8, 1), F32), pltpu.VMEM((rb, 2, 8, KV_WIDTH), F32)],
    )
    out = jax.ShapeDtypeStruct((bd, 1, NSA_WIDTH), F32)
    return pl.pallas_call(
        functools.partial(_sample_attn_kernel, past_len=past_len, rows_per_step=rb),
        grid_spec=grid_spec,
        out_shape=[out, out, out, jax.ShapeDtypeStruct((bd, 2 * KV_WIDTH, wb), F32)],
        compiler_params=_params("parallel", "arbitrary"),
        name="sample_attn",
    )(page_table, *([cache_t] * (rb * npg)), q, nsakv, winkv, ng, pool_u, memq, kcvc_s, wincache_t, poolstate, memcache,
      consts["sel_expand_s"], consts["pair_s"], consts["fold"], consts["unfold"], w["w_pool"], w["pool_scale"])


def _block_diag_ones(n, seg):
    return (jnp.arange(n)[:, None] // seg == jnp.arange(n)[None, :] // seg).astype(BF16)


def _constants(t, past_len):
    nc, ns = t // CMP_BLOCK, t // SEL_BLOCK
    ratio = SEL_BLOCK // CMP_BLOCK
    pair_t = (jnp.arange(ns)[:, None] == jnp.arange(nc)[None, :] // ratio).astype(BF16)
    ncs, nsp = past_len // CMP_BLOCK, past_len // SEL_BLOCK
    sel_expand_s = (jnp.arange(nsp)[:, None] == (jnp.arange(past_len) // SEL_BLOCK)[None, :]).astype(BF16)
    pair_s = (jnp.arange(ncs)[:, None] // ratio == jnp.arange(nsp)[None, :]).astype(BF16)
    c = jnp.arange(NSA_WIDTH)
    fold_col = (c // (GROUP * HEAD_DIM)) * HEAD_DIM + c % HEAD_DIM
    fold = (fold_col[:, None] == jnp.arange(KV_WIDTH)[None, :]).astype(BF16)
    return {
        "ones64": _block_diag_ones(NSA_WIDTH, HEAD_DIM), "ones128": _block_diag_ones(MEM_WIDTH, MEM_HEAD_DIM),
        "ones64_kv": _block_diag_ones(KV_WIDTH, HEAD_DIM),
        "pair_t": pair_t,
        "sel_expand_s": sel_expand_s, "pair_s": pair_s, "fold": fold, "unfold": fold.T,
    }


def kernel(x_prompt, x_sample, mem_prompt, cache_nsa_kv, cache_win_kv, state_pool, cache_mem_kv, page_table,
           attn_norm_g, w_in, nsa_q_g, nsa_kc_g, nsa_ks_g, nsa_kw_g, cmp_pe_k, cmp_pe_v, cmp_wk, cmp_wv,
           w_pool, pool_scale, mem_norm_g, w_mem_kv, mem_q_g, mem_k_g, w_up_nsa, w_up_pool, w_up_mem, w_out,
           ffn_norm_g, w_gate_up, w_down):
    depth = w_in.shape[0]
    b, t, _ = x_prompt.shape
    bd = x_sample.shape[0]
    n_pages = page_table.shape[1]
    past_len = n_pages * PAGE_SIZE
    mtok = mem_prompt.shape[1]
    wb = cache_win_kv.shape[2]
    assert x_sample.shape[1] == 1 and wb == WINDOW and past_len >= WINDOW and mtok == cache_mem_kv.shape[2]
    consts = _constants(t, past_len)

    nsag_lo = NSA_WIDTH + 6 * KV_WIDTH
    nsag_hi = nsag_lo + N_NSA_BRANCH * N_HEADS
    w_t = w_in.transpose(0, 2, 1)
    w_in_t = jnp.concatenate(
        [w_t[:, :nsag_lo], w_t[:, nsag_hi:], w_t[:, nsag_lo:nsag_hi],
         jnp.zeros((depth, NSAG_PAD - N_NSA_BRANCH * N_HEADS, D_MODEL), w_in.dtype)], axis=1).astype(BF16)
    eye_kv = jnp.eye(KV_HEADS, dtype=F32)
    rows = lambda g, reps: jnp.tile(g, (1, reps))[:, None, :].astype(F32)
    w = {
        "attn_g": attn_norm_g[:, None, :], "w_in_t": w_in_t,
        "q_g": rows(nsa_q_g, N_HEADS), "ks_g": rows(nsa_ks_g, KV_HEADS), "kw_g": rows(nsa_kw_g, KV_HEADS),
        "mq_g": rows(mem_q_g, MEM_HEADS), "kc_g": rows(nsa_kc_g, KV_HEADS), "mk_g": rows(mem_k_g, MEM_HEADS),
        "cmp_pe": jnp.concatenate([jnp.tile(cmp_pe_k, (1, 1, KV_HEADS)), jnp.tile(cmp_pe_v, (1, 1, KV_HEADS))], axis=2),
        "cmp_wk": jax.vmap(lambda m: jnp.kron(eye_kv, m))(cmp_wk), "cmp_wv": jax.vmap(lambda m: jnp.kron(eye_kv, m))(cmp_wv),
        "w_pool": w_pool.astype(BF16), "pool_scale": pool_scale[:, None, :],
        "mem_g": mem_norm_g[:, None, :], "w_mem_kv": w_mem_kv.astype(BF16),
        "w_mem_v_t": w_mem_kv[:, :, MEM_WIDTH:].transpose(0, 2, 1).astype(BF16),
        "w_up_nsa": w_up_nsa.astype(BF16), "w_up_pool": w_up_pool.astype(BF16), "w_up_mem": w_up_mem.astype(BF16),
        "w_out": w_out.astype(BF16), "ffn_g": ffn_norm_g[:, None, :],
        "w_gate_up": w_gate_up.astype(BF16), "w_down": w_down.astype(BF16),
    }
    swap = lambda a: a.transpose(0, 2, 1)
    cmp_w = {"cmp_pe_t": swap(w["cmp_pe"]), "cmp_wk_t": swap(w["cmp_wk"]), "cmp_wv_t": swap(w["cmp_wv"]),
             "kc_g_col": swap(w["kc_g"]), "ones64_kv": consts["ones64_kv"]}

    cache_t = cache_nsa_kv.transpose(0, 1, 3, 4, 5, 2).reshape(depth, cache_nsa_kv.shape[1], 4 * KV_WIDTH, PAGE_SIZE)
    wincache_t = cache_win_kv.transpose(0, 1, 3, 4, 5, 2).reshape(depth, bd, 2 * KV_WIDTH, wb)
    memcache = cache_mem_kv.reshape(depth, bd, cache_mem_kv.shape[2] * 2 * MEM_HEADS, MEM_HEAD_DIM)
    kcvc_s = _compress_pages(cache_t, page_table, cmp_w)

    xp = x_prompt.reshape(b * t, D_MODEL)
    xs = x_sample.reshape(bd, D_MODEL)
    nsa_p, nsa_s, win_p, win_s, pool_p, pool_s, mem_p = [], [], [], [], [], [], []
    for l in range(depth):
        q, nsakv, winkv, kext, kwbf, vt, pool_u, memq, mg, ng = _inproj(xp, l, w, consts, t)
        mkv, mk, mvt = _memkv(mem_prompt, l, w, consts)
        nsakv3 = nsakv.reshape(b, t, 4 * KV_WIDTH)
        kc, vct = _compress_prompt(nsakv3, l, w, consts)
        pool3 = pool_u.reshape(b, t, POOL_WIDTH)
        by_b = lambda a: a.reshape(b, t, a.shape[-1])
        onsa, pooly, memo = _prompt_attn(
            by_b(q), kc, vct, by_b(kext), by_b(kwbf), vt.reshape(b, -1, vt.shape[1], vt.shape[2]), pool3, by_b(memq),
            mk, mvt, by_b(ng), l, w, consts)
        xp = _merge_ffn(xp, onsa.reshape(b * t, NSA_WIDTH), pooly.reshape(b * t, POOL_WIDTH),
                        memo.reshape(b * t, MEM_WIDTH), mg, l, w)
        nsa_p.append(nsakv3.reshape(b, t, 4, KV_HEADS, HEAD_DIM))
        win_rows = winkv.reshape(b, t, 2 * KV_WIDTH)[:, t - min(WINDOW, t):]
        win_p.append(win_rows.reshape(b, win_rows.shape[1], 2, KV_HEADS, HEAD_DIM))
        pool_p.append(pool3[:, t - POOL_BUF:])
        mem_p.append(mkv.reshape(b, mtok, 2, MEM_HEADS, MEM_HEAD_DIM))
        q, nsakv, winkv, _, _, _, pool_u, memq, mg, ng = _inproj(xs, l, w, consts, bd)
        r3 = lambda a: a.reshape(bd, 1, a.shape[-1])
        onsa, pooly, memo, win_t = _sample_attn(l, r3(q), r3(nsakv), r3(winkv), r3(ng), r3(pool_u), r3(memq), kcvc_s,
                                                cache_t, wincache_t, state_pool, memcache, page_table, w, consts)
        xs = _merge_ffn(xs, onsa.reshape(bd, NSA_WIDTH), pooly.reshape(bd, POOL_WIDTH), memo.reshape(bd, MEM_WIDTH), mg, l, w)
        nsa_s.append(nsakv.reshape(bd, 1, 4, KV_HEADS, HEAD_DIM))
        win_s.append(win_t)
        pool_s.append(jnp.concatenate([state_pool[l][:, 1:], pool_u.reshape(bd, 1, POOL_WIDTH)], axis=1))
    win_s_out = jnp.stack(win_s).reshape(depth, bd, 2, KV_HEADS, HEAD_DIM, wb).transpose(0, 1, 5, 2, 3, 4)
    return (xp.reshape(b, t, D_MODEL), xs.reshape(bd, 1, D_MODEL), jnp.stack(nsa_p), jnp.stack(nsa_s),
            jnp.stack(win_p), win_s_out, jnp.stack(pool_p), jnp.stack(pool_s), jnp.stack(mem_p))
```

```python
import functools

import jax
import jax.numpy as jnp
from jax import lax
from jax.experimental import pallas as pl
from jax.experimental.pallas import tpu as pltpu

F32 = jnp.float32
BF16 = jnp.bfloat16

D_MODEL = 1024
PAGE_SIZE = 128
N_HEADS = 8
HEAD_DIM = 64
KV_HEADS = 2
GROUP = N_HEADS // KV_HEADS
NSA_WIDTH = N_HEADS * HEAD_DIM
KV_WIDTH = KV_HEADS * HEAD_DIM
CMP_BLOCK = 32
SEL_BLOCK = 64
TOP_K = 16
WINDOW = 512
N_NSA_BRANCH = 3
POOL_WIDTH = 512
POOL_GROUPS = 4
POOL_GROUP_WIDTH = POOL_WIDTH // POOL_GROUPS
POOL_WINDOWS = (2, 4, 8, 16)
POOL_BUF = 15
MEM_HEADS = 4
MEM_HEAD_DIM = 128
MEM_WIDTH = MEM_HEADS * MEM_HEAD_DIM
N_BRANCH = 3
D_FF = ((8 * D_MODEL // 3 + 255) // 256) * 256
EPS = 1e-6
LOG2_E = 1.4426950408889634
NEG = -1e30
BIG = 1e9

V7X_LANES = 128
V7X_SUBLANES = 8
ONES_ROWS = 2 * V7X_SUBLANES
V7X_VMEM_LIMIT_BYTES = 56 * 1024 * 1024

C_Q = 0
C_NSAKV = C_Q + NSA_WIDTH
C_WINKV = C_NSAKV + 4 * KV_WIDTH
C_POOL = C_WINKV + 2 * KV_WIDTH
C_MEMQ = C_POOL + POOL_WIDTH
C_MERGE = C_MEMQ + MEM_WIDTH
C_NSAG = C_MERGE + N_BRANCH * D_MODEL
NSAG_PAD = V7X_LANES
PROJ_PAD = C_NSAG + NSAG_PAD

TM_PROJ = 512
TQ = 512
SEL_CHUNKS_PER_STEP = 1
V7X_MXU_DIM = 256
FF_CHUNKS = (6 * V7X_MXU_DIM, D_FF - 6 * V7X_MXU_DIM)
PAGES_PER_STEP_CMP = 32
PAGES_PER_STEP_SEL = 32
ROWS_PER_DECODE_STEP = 2


def _dot(a, b):
    return jnp.dot(a, b, preferred_element_type=F32)


def _dot_nt(a, b):
    return lax.dot_general(a, b, (((1,), (1,)), ((), ())), preferred_element_type=F32)


def _dot_f32(a, b):
    return jnp.dot(a, b, preferred_element_type=F32, precision=lax.Precision.HIGHEST)


def _split_bf16(a, n):
    parts, r = [], a
    for _ in range(n):
        p = r.astype(BF16)
        parts.append(p)
        r = r - p.astype(F32)
    return parts


def _dot_split_lhs(a, b, n):
    return sum(_dot(p, b) for p in _split_bf16(a, n))


def _dot_split_rhs(a, b, n):
    return sum(_dot(a, p) for p in _split_bf16(b, n))


def _iota(shape, dim):
    return lax.broadcasted_iota(jnp.int32, shape, dim)


def _rms_rows(x, g):
    return x * lax.rsqrt(jnp.mean(x * x, axis=-1, keepdims=True) + EPS) * g


def _seg_rmsnorm(v, ones_bd, gain, seg):
    n = ones_bd.shape[0]
    sq = (v * v).astype(BF16)
    ssq = jnp.concatenate([_dot(sq[:, lo:lo + n], ones_bd) for lo in range(0, v.shape[1], n)], axis=1)
    return v * lax.rsqrt(ssq * (1.0 / seg) + EPS) * gain


def _const_spec(shape):
    return pl.BlockSpec(shape, lambda *_: (0,) * len(shape), pipeline_mode=pl.Buffered(1))


def _layer_spec(l, shape):
    return pl.BlockSpec((None,) + tuple(shape), lambda *_: (l,) + (0,) * len(shape), pipeline_mode=pl.Buffered(1))


def _params(*sem):
    return pltpu.CompilerParams(dimension_semantics=sem, vmem_limit_bytes=V7X_VMEM_LIMIT_BYTES)


def _inproj_kernel(x_ref, g_ref, w_ref, qg_ref, ksg_ref, kwg_ref, mqg_ref, ones64_ref, ones128_ref,
                   q_ref, nsakv_ref, winkv_ref, kext_ref, kwbf_ref, vt_ref, pool_ref, memq_ref, mg_ref, ng_ref,
                   *, seq_len):
    tm = x_ref.shape[0]
    hb = _rms_rows(x_ref[...], g_ref[...]).astype(BF16)

    def proj(lo, width):
        return _dot_nt(hb, w_ref[lo:lo + width, :])

    ones_kv = ones64_ref[0:KV_WIDTH, 0:KV_WIDTH]
    q = _seg_rmsnorm(proj(C_Q, NSA_WIDTH), ones64_ref[...], qg_ref[...], HEAD_DIM)
    q_ref[...] = (q * (HEAD_DIM ** -0.5 * LOG2_E)).astype(BF16)
    kcvc = proj(C_NSAKV, 2 * KV_WIDTH)
    ksvs = proj(C_NSAKV + 2 * KV_WIDTH, 2 * KV_WIDTH)
    kwvw = proj(C_WINKV, 2 * KV_WIDTH)
    ks = _seg_rmsnorm(ksvs[:, 0:KV_WIDTH], ones_kv, ksg_ref[...], HEAD_DIM)
    vs = ksvs[:, KV_WIDTH:2 * KV_WIDTH]
    kw = _seg_rmsnorm(kwvw[:, 0:KV_WIDTH], ones_kv, kwg_ref[...], HEAD_DIM)
    vw = kwvw[:, KV_WIDTH:2 * KV_WIDTH]
    nsakv_ref[:, 0:2 * KV_WIDTH] = kcvc
    nsakv_ref[:, 2 * KV_WIDTH:3 * KV_WIDTH] = ks
    nsakv_ref[:, 3 * KV_WIDTH:4 * KV_WIDTH] = vs
    winkv_ref[:, 0:KV_WIDTH] = kw
    winkv_ref[:, KV_WIDTH:2 * KV_WIDTH] = vw
    t_seq = (pl.program_id(0) * tm + _iota((tm, HEAD_DIM), 0)) % seq_len
    onehot = jnp.where(t_seq // SEL_BLOCK == _iota((tm, HEAD_DIM), 1), 1.0, 0.0).astype(BF16)
    for k in range(KV_HEADS):
        kext_ref[:, 2 * k * HEAD_DIM:(2 * k + 1) * HEAD_DIM] = ks[:, k * HEAD_DIM:(k + 1) * HEAD_DIM].astype(BF16)
        kext_ref[:, (2 * k + 1) * HEAD_DIM:(2 * k + 2) * HEAD_DIM] = onehot
    kwbf_ref[...] = kw.astype(BF16)
    vs_lo, vw_lo = C_NSAKV + 3 * KV_WIDTH, C_WINKV + KV_WIDTH
    vt_ref[0:KV_WIDTH, :] = _dot_nt(w_ref[vs_lo:vs_lo + KV_WIDTH, :], hb).astype(BF16)
    vt_ref[KV_WIDTH:2 * KV_WIDTH, :] = _dot_nt(w_ref[vw_lo:vw_lo + KV_WIDTH, :], hb).astype(BF16)
    pool_ref[...] = proj(C_POOL, POOL_WIDTH)
    mq = _seg_rmsnorm(proj(C_MEMQ, MEM_WIDTH), ones128_ref[...], mqg_ref[...], MEM_HEAD_DIM)
    memq_ref[...] = mq.astype(BF16)
    for j in range(N_BRANCH):
        mg_ref[:, j * D_MODEL:(j + 1) * D_MODEL] = jax.nn.sigmoid(proj(C_MERGE + j * D_MODEL, D_MODEL))
    ng_ref[...] = jax.nn.sigmoid(proj(C_NSAG, NSAG_PAD))


def _inproj(x, l, w, consts, seq_len):
    n = x.shape[0]
    lay = functools.partial(_layer_spec, l)
    tm = min(TM_PROJ, n)
    row = lambda w: pl.BlockSpec((tm, w), lambda i: (i, 0))
    widths = (NSA_WIDTH, 4 * KV_WIDTH, 2 * KV_WIDTH, 2 * KV_WIDTH, KV_WIDTH, None, POOL_WIDTH, MEM_WIDTH,
              N_BRANCH * D_MODEL, NSAG_PAD)
    dtypes = (BF16, F32, F32, BF16, BF16, BF16, F32, BF16, F32, F32)
    vt_spec = pl.BlockSpec((None, 2 * KV_WIDTH, tm), lambda i: (i, 0, 0))
    vt_shape = jax.ShapeDtypeStruct((n // tm, 2 * KV_WIDTH, tm), BF16)
    return pl.pallas_call(
        functools.partial(_inproj_kernel, seq_len=seq_len),
        grid=(n // tm,),
        in_specs=[row(D_MODEL), lay((1, D_MODEL)), lay((PROJ_PAD, D_MODEL)),
                  lay((1, NSA_WIDTH)), lay((1, KV_WIDTH)), lay((1, KV_WIDTH)), lay((1, MEM_WIDTH)),
                  _const_spec((V7X_MXU_DIM, V7X_MXU_DIM)), _const_spec((V7X_MXU_DIM, V7X_MXU_DIM))],
        out_specs=[vt_spec if wd is None else row(wd) for wd in widths],
        out_shape=[vt_shape if wd is None else jax.ShapeDtypeStruct((n, wd), d) for wd, d in zip(widths, dtypes)],
        compiler_params=_params("parallel"),
        name="inproj",
    )(x, w["attn_g"], w["w_in_t"], w["q_g"], w["ks_g"], w["kw_g"], w["mq_g"], consts["ones64"], consts["ones128"])


def _memkv_kernel(m_ref, g_ref, w_ref, wvt_ref, kg_ref, ones128_ref, kv_ref, kbf_ref, vt_ref):
    hb = _rms_rows(m_ref[...], g_ref[...]).astype(BF16)
    k = _seg_rmsnorm(_dot(hb, w_ref[:, 0:MEM_WIDTH]), ones128_ref[...], kg_ref[...], MEM_HEAD_DIM)
    kv_ref[:, 0:MEM_WIDTH] = k
    kv_ref[:, MEM_WIDTH:2 * MEM_WIDTH] = _dot(hb, w_ref[:, MEM_WIDTH:2 * MEM_WIDTH])
    kbf_ref[...] = k.astype(BF16)
    vt_ref[...] = _dot_nt(wvt_ref[...], hb).astype(BF16)


def _memkv(mem, l, w, consts):
    b, mtok, _ = mem.shape
    lay = functools.partial(_layer_spec, l)
    blk = lambda r, c: pl.BlockSpec((None, r, c), lambda i: (i, 0, 0))
    return pl.pallas_call(
        _memkv_kernel,
        grid=(b,),
        in_specs=[blk(mtok, D_MODEL), lay((1, D_MODEL)), lay((D_MODEL, 2 * MEM_WIDTH)),
                  lay((MEM_WIDTH, D_MODEL)), lay((1, MEM_WIDTH)), _const_spec((V7X_MXU_DIM, V7X_MXU_DIM))],
        out_specs=[blk(mtok, 2 * MEM_WIDTH), blk(mtok, MEM_WIDTH), blk(MEM_WIDTH, mtok)],
        out_shape=[jax.ShapeDtypeStruct((b, mtok, 2 * MEM_WIDTH), F32), jax.ShapeDtypeStruct((b, mtok, MEM_WIDTH), BF16),
                   jax.ShapeDtypeStruct((b, MEM_WIDTH, mtok), BF16)],
        compiler_params=_params("parallel"),
        name="memkv",
    )(mem, w["mem_g"], w["w_mem_kv"], w["w_mem_v_t"], w["mk_g"], consts["ones128"])


def _compress_prompt_kernel(kv_ref, pe_ref, wk_ref, wv_ref, kcg_ref, ones64_ref, kc_ref, vct_ref):
    t = kv_ref.shape[0]
    nc = t // CMP_BLOCK
    pooled = jnp.sum(kv_ref[...].reshape(nc, CMP_BLOCK, 2 * KV_WIDTH), axis=1) * (1.0 / CMP_BLOCK)
    pooled = pooled + jnp.sum(pe_ref[...], axis=0, keepdims=True) * (1.0 / CMP_BLOCK)
    k = _dot_f32(pooled[:, 0:KV_WIDTH], wk_ref[...])
    v = _dot_f32(pooled[:, KV_WIDTH:2 * KV_WIDTH], wv_ref[...])
    kc_ref[...] = _seg_rmsnorm(k, ones64_ref[...], kcg_ref[...], HEAD_DIM).astype(BF16)
    vct_ref[...] = v.T.astype(BF16)


def _compress_prompt(nsakv, l, w, consts):
    b, t, _ = nsakv.shape
    nc = t // CMP_BLOCK
    return pl.pallas_call(
        _compress_prompt_kernel,
        grid=(b,),
        in_specs=[pl.BlockSpec((None, t, 2 * KV_WIDTH), lambda i: (i, 0, 0)),
                  _layer_spec(l, (CMP_BLOCK, 2 * KV_WIDTH)), _layer_spec(l, (KV_WIDTH, KV_WIDTH)),
                  _layer_spec(l, (KV_WIDTH, KV_WIDTH)), _layer_spec(l, (1, KV_WIDTH)), _const_spec((KV_WIDTH, KV_WIDTH))],
        out_specs=[pl.BlockSpec((None, nc, KV_WIDTH), lambda i: (i, 0, 0)),
                   pl.BlockSpec((None, KV_WIDTH, nc), lambda i: (i, 0, 0))],
        out_shape=[jax.ShapeDtypeStruct((b, nc, KV_WIDTH), BF16), jax.ShapeDtypeStruct((b, KV_WIDTH, nc), BF16)],
        compiler_params=_params("parallel"),
        name="compress_prompt",
    )(nsakv, w["cmp_pe"], w["cmp_wk"], w["cmp_wv"], w["kc_g"], consts["ones64_kv"])


def _masked_softmax_rows(s, mask):
    maskf = jnp.where(mask, 1.0, 0.0)
    s = jnp.where(mask[None], s, NEG)
    m = jnp.max(s, axis=-1, keepdims=True)
    p = jnp.exp2(s - m) * maskf[None]
    den = jnp.maximum(jnp.sum(p, axis=-1, keepdims=True), 1e-30)
    return p * (1.0 / den)


def _stack_heads(q, k):
    return jnp.concatenate([q[:, (k * GROUP + g) * HEAD_DIM:(k * GROUP + g + 1) * HEAD_DIM] for g in range(GROUP)], axis=0)


def _transpose_tiles(x):
    r, c = x.shape
    n = V7X_LANES
    return jnp.concatenate(
        [jnp.concatenate([x[i * n:(i + 1) * n, j * n:(j + 1) * n].T for i in range(r // n)], axis=1) for j in range(c // n)],
        axis=0)


def _topk_select_t(score_t, n_sel):
    ns, tq = score_t.shape
    sub = V7X_SUBLANES
    tiles = [score_t[v * sub:(v + 1) * sub] for v in range(ns // sub)]
    row = _iota((sub, tq), 0)
    cnts = [jnp.zeros((sub, tq), F32) for _ in tiles]
    for j in range(ns):
        r = jnp.broadcast_to(score_t[j:j + 1, :], (sub, tq))
        for v, tile in enumerate(tiles):
            if v * sub > j:
                wins = r >= tile
            elif v * sub + sub - 1 < j:
                wins = r > tile
            else:
                wins = jnp.where(row + v * sub > j, jnp.where(r >= tile, 1.0, 0.0), jnp.where(r > tile, 1.0, 0.0)) > 0.5
            cnts[v] = cnts[v] + jnp.where(wins, 1.0, 0.0)
    return jnp.where(jnp.concatenate(cnts, axis=0) < n_sel, 1.0, 0.0)


def _with_ones_rows(v_t):
    return jnp.concatenate([v_t, jnp.ones((ONES_ROWS, v_t.shape[1]), BF16)], axis=0)


def _pv_and_sum(v_t, p):
    d = v_t.shape[0]
    r = _dot(_with_ones_rows(v_t), p.astype(BF16))
    return r[0:d], r[d:d + 1]


def _online_softmax_step(state, s_t, v_t):
    m, l, acc = state
    m_new = jnp.maximum(m, jnp.max(s_t, axis=0, keepdims=True))
    alpha = jnp.exp2(m - m_new)
    pv, psum = _pv_and_sum(v_t, jnp.exp2(s_t - m_new))
    return m_new, alpha * l + psum, alpha * acc + pv


def _merge_softmax_states(states):
    m = states[0][0]
    for st in states[1:]:
        m = jnp.maximum(m, st[0])
    l, acc = 0.0, 0.0
    for m_i, l_i, acc_i in states:
        w = jnp.exp2(m_i - m)
        l, acc = l + w * l_i, acc + w * acc_i
    return m, l, acc


def _prompt_attn_kernel(q_ref, kc_ref, vct_ref, kext_ref, kw_ref, vt_ref, poolc_ref, poolp_ref, memq_ref, mk_ref, mvt_ref,
                        ng_ref, pairt_ref, wpool_ref, pscale_ref,
                        onsa_ref, pooly_ref, memo_ref, pool_scr, part_scr):
    tq = q_ref.shape[0]
    nc = kc_ref.shape[0]
    ns = pairt_ref.shape[0]
    kc = vt_ref.shape[2]
    rows = GROUP * tq
    n_sel = min(TOP_K, ns)
    qi = pl.program_id(1)
    t0 = qi * tq
    q = q_ref[...]

    eye_tq = jnp.where(_iota((tq, tq), 0) == _iota((tq, tq), 1), 1.0, 0.0).astype(BF16)
    blk_t = _iota((ns, tq), 0)
    tpos_t = t0 + _iota((ns, tq), 1)
    cur_t = tpos_t // SEL_BLOCK
    valid_t = blk_t * SEL_BLOCK <= tpos_t
    forced_t = (blk_t == 0) | (blk_t == cur_t) | (blk_t == cur_t - 1)
    tok_of_row = lambda shape: t0 + (_iota(shape, 1) & (tq - 1))
    cmask = (_iota((nc, rows), 0) * CMP_BLOCK + (CMP_BLOCK - 1)) <= tok_of_row((nc, rows))
    cmaskf = jnp.where(cmask, 1.0, 0.0)
    ng_t = _transpose_tiles(ng_ref[...])
    nwb = WINDOW // tq + 1
    rel = _iota((tq, rows), 0) - (_iota((tq, rows), 1) & (tq - 1))
    win_first = jnp.where(rel > 0, 0.0, NEG)
    win_diag = jnp.where(rel <= 0, 0.0, NEG)
    init = (jnp.full((1, rows), NEG, F32), jnp.zeros((1, rows), F32), jnp.zeros((HEAD_DIM, rows), F32))

    def gate_row(k, br):
        idx = [N_NSA_BRANCH * (k * GROUP + g) + br for g in range(GROUP)]
        return jnp.concatenate([ng_t[i:i + 1, :] for i in idx], axis=1)

    q_exts = []
    for k in range(KV_HEADS):
        qk = _stack_heads(q, k)
        s_c = jnp.where(cmask, _dot_nt(kc_ref[:, k * HEAD_DIM:(k + 1) * HEAD_DIM], qk), NEG)
        p = jnp.exp2(s_c - jnp.max(s_c, axis=0, keepdims=True)) * cmaskf
        p = p * (1.0 / jnp.maximum(jnp.sum(p, axis=0, keepdims=True), 1e-30))
        o_cmp = _dot(vct_ref[k * HEAD_DIM:(k + 1) * HEAD_DIM, :], p.astype(BF16))
        imp_t = p[:, 0:tq]
        for g in range(1, GROUP):
            imp_t = imp_t + p[:, g * tq:(g + 1) * tq]
        imp2_t = _dot_split_rhs(pairt_ref[...], imp_t, 3)
        score_t = jnp.where(valid_t, jnp.where(forced_t, BIG, imp2_t), -BIG)
        sel_t = _topk_select_t(score_t, n_sel)
        sel = _dot_nt(eye_tq, sel_t.astype(BF16))
        selneg = jnp.where(sel > 0.5, 0.0, NEG).astype(BF16)
        if ns < HEAD_DIM:
            selneg = jnp.concatenate([selneg, jnp.zeros((tq, HEAD_DIM - ns), BF16)], axis=1)
        q_exts.append(jnp.concatenate([qk, jnp.concatenate([selneg] * GROUP, axis=0)], axis=1))
        s_blocks, v_blocks = [], []
        for i in range(nwb):
            kb = qi - (nwb - 1) + i
            inside = kb >= 0
            kbc = jnp.maximum(kb, 0)
            start = pl.multiple_of(kbc * tq, tq)
            k_b = kw_ref[pl.ds(start, tq), k * HEAD_DIM:(k + 1) * HEAD_DIM]
            s_b = _dot_nt(jnp.where(inside, k_b, jnp.zeros_like(k_b)), qk)
            if i == nwb - 1:
                s_b = s_b + win_diag
            elif i == 0:
                s_b = s_b + win_first
            v_pair = vt_ref[kbc // (kc // tq), KV_WIDTH + k * HEAD_DIM:KV_WIDTH + (k + 1) * HEAD_DIM, :]
            v_b = v_pair[:, 0:tq]
            for part in range(1, kc // tq):
                v_b = jnp.where(kbc % (kc // tq) == part, v_pair[:, part * tq:(part + 1) * tq], v_b)
            s_blocks.append(s_b)
            v_blocks.append(jnp.where(inside, v_b, jnp.zeros_like(v_b)))
        s_w = jnp.concatenate(s_blocks, axis=0)
        pv_w, l_w = _pv_and_sum(jnp.concatenate(v_blocks, axis=1), jnp.exp2(s_w - jnp.max(s_w, axis=0, keepdims=True)))
        o_win = pv_w * (1.0 / l_w)
        part_scr[k] = gate_row(k, 0) * o_cmp + gate_row(k, 2) * o_win

    mq = memq_ref[...]
    outs = []
    for h in range(MEM_HEADS):
        lo = h * MEM_HEAD_DIM
        s_m = _dot_nt(mk_ref[:, lo:lo + MEM_HEAD_DIM], mq[:, lo:lo + MEM_HEAD_DIM]) * MEM_HEAD_DIM ** -0.5
        pv_m, l_m = _pv_and_sum(mvt_ref[lo:lo + MEM_HEAD_DIM, :], jnp.exp(s_m - jnp.max(s_m, axis=0, keepdims=True)))
        outs.append(_transpose_tiles(pv_m * (1.0 / l_m)))
    memo_ref[...] = jnp.concatenate(outs, axis=1).astype(memo_ref.dtype)

    cur = poolc_ref[...]
    pool_scr[0:16, :] = jnp.where(qi > 0, poolp_ref[...], 0.0)
    pool_scr[16:16 + tq, :] = cur
    tpos1 = t0 + _iota((tq, POOL_GROUP_WIDTH), 0) + 1
    ys = []
    for gi, win in enumerate(POOL_WINDOWS):
        lo = gi * POOL_GROUP_WIDTH
        acc = pool_scr[16:16 + tq, lo:lo + POOL_GROUP_WIDTH]
        for sft in range(1, win):
            acc = acc + pool_scr[16 - sft:16 - sft + tq, lo:lo + POOL_GROUP_WIDTH]
        cnt = jnp.minimum(tpos1, win).astype(F32)
        d = acc / cnt - cur[:, lo:lo + POOL_GROUP_WIDTH]
        ys.append(_dot(d.astype(BF16), wpool_ref[gi]))
    pooly_ref[...] = (jnp.concatenate(ys, axis=1) * pscale_ref[...]).astype(pooly_ref.dtype)

    per = SEL_CHUNKS_PER_STEP
    span = per * kc
    n_full = t0 // span

    def sel_scores(k, sc):
        start = pl.multiple_of(sc * span, span)
        return _dot_nt(kext_ref[pl.ds(start, span), 2 * k * HEAD_DIM:(2 * k + 2) * HEAD_DIM], q_exts[k])

    def sel_values(k, sc):
        return jnp.concatenate([vt_ref[sc * per + j, k * HEAD_DIM:(k + 1) * HEAD_DIM, :] for j in range(per)], axis=1)

    def sel_step(sc, states):
        return tuple(_online_softmax_step(states[k], sel_scores(k, sc), sel_values(k, sc)) for k in range(KV_HEADS))

    states = lax.fori_loop(0, n_full, sel_step, (init,) * KV_HEADS)

    causal = (n_full * span + _iota((span, rows), 0)) <= tok_of_row((span, rows))
    head_out = []
    for k in range(KV_HEADS):
        _, l_k, acc_k = _online_softmax_step(states[k], jnp.where(causal, sel_scores(k, n_full), NEG),
                                             sel_values(k, n_full))
        o_k = part_scr[k] + gate_row(k, 1) * (acc_k * (1.0 / l_k))
        head_out += [o_k[:, g * tq:(g + 1) * tq] for g in range(GROUP)]
    pairs = [_transpose_tiles(jnp.concatenate(head_out[2 * j:2 * j + 2], axis=0)) for j in range(N_HEADS // 2)]
    onsa_ref[...] = jnp.concatenate(pairs, axis=1).astype(onsa_ref.dtype)


def _prompt_attn(q, kc, vct, kext, kw, vt, pool_u, memq, mk, mvt, ng, l, w, consts):
    b, t, _ = q.shape
    nc = t // CMP_BLOCK
    ns = t // SEL_BLOCK
    tq = TQ
    chunk = vt.shape[3]
    m = mk.shape[1]
    span = SEL_CHUNKS_PER_STEP * chunk
    assert WINDOW % tq == 0 and chunk % tq == 0 and t % span == 0 and span % tq == 0
    assert ns <= HEAD_DIM and tq % V7X_LANES == 0 and tq & (tq - 1) == 0
    tok = lambda w: pl.BlockSpec((None, tq, w), lambda i, j: (i, j, 0))
    per_b = lambda r, w: pl.BlockSpec((None, r, w), lambda i, j: (i, 0, 0))
    prev_rows = pl.BlockSpec((None, 16, POOL_WIDTH), lambda i, j: (i, jnp.maximum(j * (tq // 16) - 1, 0), 0))
    out = jax.ShapeDtypeStruct((b, t, NSA_WIDTH), BF16)
    return pl.pallas_call(
        _prompt_attn_kernel,
        grid=(b, t // tq),
        in_specs=[tok(NSA_WIDTH), per_b(nc, KV_WIDTH), per_b(KV_WIDTH, nc), per_b(t, 2 * KV_WIDTH), per_b(t, KV_WIDTH),
                  pl.BlockSpec((None, t // chunk, 2 * KV_WIDTH, chunk), lambda i, j: (i, 0, 0, 0)),
                  tok(POOL_WIDTH), prev_rows, tok(MEM_WIDTH), per_b(m, MEM_WIDTH), per_b(MEM_WIDTH, m), tok(NSAG_PAD),
                  _const_spec((ns, nc)),
                  _layer_spec(l, (POOL_GROUPS, POOL_GROUP_WIDTH, POOL_GROUP_WIDTH)), _layer_spec(l, (1, POOL_WIDTH))],
        out_specs=[tok(NSA_WIDTH), tok(POOL_WIDTH), tok(MEM_WIDTH)],
        out_shape=[out, out, out],
        scratch_shapes=[pltpu.VMEM((16 + tq, POOL_WIDTH), F32), pltpu.VMEM((KV_HEADS, HEAD_DIM, GROUP * tq), F32)],
        compiler_params=_params("parallel", "arbitrary"),
        name="prompt_attn",
    )(q, kc, vct, kext, kw, vt, pool_u, pool_u, memq, mk, mvt, ng, consts["pair_t"], w["w_pool"], w["pool_scale"])


def _merge_ffn_kernel(x_ref, onsa_ref, pooly_ref, memo_ref, mg_ref, wn_ref, wp_ref, wm_ref, wo_ref, fg_ref,
                      wgu_ref, wd_ref, y_ref):
    h = (mg_ref[:, 0:D_MODEL] * _dot(onsa_ref[...].astype(BF16), wn_ref[...])
         + mg_ref[:, D_MODEL:2 * D_MODEL] * _dot(pooly_ref[...].astype(BF16), wp_ref[...])
         + mg_ref[:, 2 * D_MODEL:3 * D_MODEL] * _dot(memo_ref[...].astype(BF16), wm_ref[...]))
    x1 = x_ref[...] + _dot(h.astype(BF16), wo_ref[...])
    hn = _rms_rows(x1, fg_ref[...]).astype(BF16)
    acc = x1
    lo = 0
    for width in FF_CHUNKS:
        gate = _dot(hn, wgu_ref[:, lo:lo + width])
        up = _dot(hn, wgu_ref[:, D_FF + lo:D_FF + lo + width])
        act = gate * jax.nn.sigmoid(gate) * up
        acc = acc + _dot(act.astype(BF16), wd_ref[lo:lo + width, :])
        lo += width
    y_ref[...] = acc


def _merge_ffn(x, onsa, pooly, memo, mg, l, w):
    n = x.shape[0]
    lay = functools.partial(_layer_spec, l)
    tm = min(TM_PROJ, n)
    row = lambda w: pl.BlockSpec((tm, w), lambda i: (i, 0))
    return pl.pallas_call(
        _merge_ffn_kernel,
        grid=(n // tm,),
        in_specs=[row(D_MODEL), row(NSA_WIDTH), row(POOL_WIDTH), row(MEM_WIDTH), row(N_BRANCH * D_MODEL),
                  lay((NSA_WIDTH, D_MODEL)), lay((POOL_WIDTH, D_MODEL)), lay((MEM_WIDTH, D_MODEL)),
                  lay((D_MODEL, D_MODEL)), lay((1, D_MODEL)), lay((D_MODEL, 2 * D_FF)), lay((D_FF, D_MODEL))],
        out_specs=row(D_MODEL),
        out_shape=jax.ShapeDtypeStruct((n, D_MODEL), F32),
        compiler_params=_params("parallel"),
        name="merge_ffn",
    )(x, onsa, pooly, memo, mg, w["w_up_nsa"], w["w_up_pool"], w["w_up_mem"], w["w_out"], w["ffn_g"],
      w["w_gate_up"], w["w_down"])


def _page_specs(npg, half, layer_of, batch_of, step_of):
    def spec(i):
        def index_map(*args):
            pt = args[-1]
            grid = args[:-1]
            return (layer_of(grid), pt[batch_of(grid), step_of(grid) * npg + i], half, 0)
        return pl.BlockSpec((None, None, 2 * KV_WIDTH, PAGE_SIZE), index_map)
    return [spec(i) for i in range(npg)]


def _compress_pages_kernel(pt_ref, *refs):
    del pt_ref
    npg = len(refs) - 7
    pages = refs[:npg]
    pe_ref, wkt_ref, wvt_ref, kcg_ref, ones64_ref, poolm_ref, out_ref = refs[npg:]
    x = jnp.concatenate([pg[...] for pg in pages], axis=1)
    pooled = _dot(x.astype(BF16), poolm_ref[...])
    pooled = pooled + jnp.sum(pe_ref[...], axis=1, keepdims=True) * (1.0 / CMP_BLOCK)
    k = _dot_f32(wkt_ref[...], pooled[0:KV_WIDTH])
    v = _dot_f32(wvt_ref[...], pooled[KV_WIDTH:2 * KV_WIDTH])
    ssq = _dot_split_rhs(ones64_ref[...], k * k, 2)
    out_ref[0:KV_WIDTH, :] = k * lax.rsqrt(ssq * (1.0 / HEAD_DIM) + EPS) * kcg_ref[...]
    out_ref[KV_WIDTH:2 * KV_WIDTH, :] = v


def _compress_pages(cache_t, page_table, w):
    depth = cache_t.shape[0]
    bd, n_pages = page_table.shape
    npg = min(PAGES_PER_STEP_CMP, n_pages)
    per_page = PAGE_SIZE // CMP_BLOCK
    per_step = npg * per_page
    ncs = n_pages * per_page
    lconst = lambda r, c: pl.BlockSpec((None, r, c), lambda l, b, s, pt: (l, 0, 0))
    const = lambda r, c: pl.BlockSpec((r, c), lambda l, b, s, pt: (0, 0))
    grid_spec = pltpu.PrefetchScalarGridSpec(
        num_scalar_prefetch=1,
        grid=(depth, bd, n_pages // npg),
        in_specs=_page_specs(npg, 0, lambda g: g[0], lambda g: g[1], lambda g: g[2])
        + [lconst(2 * KV_WIDTH, CMP_BLOCK), lconst(KV_WIDTH, KV_WIDTH), lconst(KV_WIDTH, KV_WIDTH), lconst(KV_WIDTH, 1),
           const(KV_WIDTH, KV_WIDTH), const(npg * PAGE_SIZE, per_step)],
        out_specs=pl.BlockSpec((None, None, 2 * KV_WIDTH, per_step), lambda l, b, s, pt: (l, b, 0, s)),
    )
    rows = jnp.arange(npg * PAGE_SIZE)[:, None] // CMP_BLOCK
    poolm = jnp.where(rows == jnp.arange(per_step)[None, :], 1.0 / CMP_BLOCK, 0.0).astype(BF16)
    return pl.pallas_call(
        _compress_pages_kernel,
        grid_spec=grid_spec,
        out_shape=jax.ShapeDtypeStruct((depth, bd, 2 * KV_WIDTH, ncs), F32),
        compiler_params=_params("parallel", "parallel", "arbitrary"),
        name="compress_pages",
    )(page_table, *([cache_t] * npg), w["cmp_pe_t"], w["cmp_wk_t"], w["cmp_wv_t"], w["kc_g_col"], w["ones64_kv"], poolm)


def _rows8(row):
    return jnp.broadcast_to(row, (V7X_SUBLANES, row.shape[1]))


def _sample_attn_kernel(pt_ref, *refs, past_len, rows_per_step):
    del pt_ref
    n_page_refs = len(refs) - 25
    npg = n_page_refs // rows_per_step
    (q_all, nsakv_all, winkv_all, ng_all, poolu_all, memq_all, kcvc_all, wincache_all, poolstate_all, memcache_all,
     e_ref, pair_ref, fold_ref, unfold_ref, wpool_ref, pscale_ref,
     onsa_all, pooly_all, memo_all, winout_all,
     qbd_all, bias_all, m_all, l_all, acc_all) = refs[n_page_refs:]
    per_row = (q_all, nsakv_all, winkv_all, ng_all, poolu_all, memq_all, kcvc_all, wincache_all, poolstate_all,
               memcache_all, onsa_all, pooly_all, memo_all, winout_all, qbd_all, bias_all, m_all, l_all, acc_all)
    views = [tuple(ref.at[i] for ref in per_row) for i in range(rows_per_step)]
    row_pages = [refs[i * npg:(i + 1) * npg] for i in range(rows_per_step)]
    step = pl.program_id(1)
    n_steps = pl.num_programs(1)
    nsp = pair_ref.shape[1]
    keys_per_step = npg * PAGE_SIZE
    pos = past_len
    row8 = _iota((8, KV_WIDTH), 0)
    lane8 = _iota((8, KV_WIDTH), 1)
    head_lanes = (lane8 // HEAD_DIM) == (row8 // GROUP)

    def prologue(view):
        (q_ref, nsakv_ref, winkv_ref, ng_ref, poolu_ref, memq_ref, kcvc_ref, wincache_ref, poolstate_ref, memcache_ref,
         onsa_ref, pooly_ref, memo_ref, winout_ref, qbd_scr, bias_scr, m_scr, l_scr, acc_scr) = view
        ncs = kcvc_ref.shape[1]
        q8 = jnp.where((_iota((8, NSA_WIDTH), 1) // HEAD_DIM) == _iota((8, NSA_WIDTH), 0),
                       _rows8(q_ref[...].astype(F32)), 0.0)
        qbd = _dot(q8.astype(BF16), fold_ref[...])
        qbd_scr[...] = qbd
        s = _dot(qbd.astype(BF16), kcvc_ref[0:KV_WIDTH, :].astype(BF16))
        cmask = (_iota((8, ncs), 1) * CMP_BLOCK + (CMP_BLOCK - 1)) <= pos
        p = _masked_softmax_rows(s[None], cmask)[0]
        ocmp = _dot_nt(p.astype(BF16), kcvc_ref[KV_WIDTH:2 * KV_WIDTH, :].astype(BF16))
        acc_scr[0] = jnp.where(head_lanes, ocmp, 0.0)
        eye = _iota((nsp, nsp), 0) == _iota((nsp, nsp), 1)
        ii = _iota((nsp, nsp), 1)
        jj = _iota((nsp, nsp), 0)
        cur = pos // SEL_BLOCK
        sel_rows = []
        for k in range(KV_HEADS):
            imp = jnp.sum(p[k * GROUP:(k + 1) * GROUP], axis=0, keepdims=True)
            imp2 = _dot_split_lhs(_rows8(imp), pair_ref[...], 3)[0:1]
            blk = _iota((1, nsp), 1)
            forced = (blk == 0) | (blk == cur) | (blk == cur - 1)
            valid = blk * SEL_BLOCK <= pos
            score = jnp.where(valid, jnp.where(forced, BIG, imp2), -BIG)
            score_i = jnp.broadcast_to(score, (nsp, nsp))
            score_j = jnp.sum(jnp.where(eye, score_i, 0.0), axis=1, keepdims=True)
            beats = jnp.where((score_j > score_i) | ((score_j == score_i) & (jj < ii)), 1.0, 0.0)
            cnt = jnp.sum(beats, axis=0, keepdims=True)
            cnt = cnt + jnp.where(score < BIG, 1.0, 0.0)
            sel_rows.append(jnp.where(cnt < TOP_K, 1.0, 0.0))
        sel8 = jnp.where(_iota((8, nsp), 0) < GROUP, _rows8(sel_rows[0]), _rows8(sel_rows[1])).astype(BF16)
        for st in range(bias_scr.shape[0]):
            selx = _dot(sel8, e_ref[:, st * keys_per_step:(st + 1) * keys_per_step])
            bias_scr[st] = jnp.where(selx > 0.5, 0.0, NEG)
        m_scr[...] = jnp.full(m_scr.shape, NEG, F32)
        l_scr[...] = jnp.zeros(l_scr.shape, F32)
        acc_scr[1] = jnp.zeros((8, KV_WIDTH), F32)

    def stream(view, pages):
        qbd_scr, bias_scr, m_scr, l_scr, acc_scr = view[-5:]
        qbd_bf = qbd_scr[...].astype(BF16)
        k_t = jnp.concatenate([pg[0:KV_WIDTH, :] for pg in pages], axis=1).astype(BF16)
        v_t = jnp.concatenate([pg[KV_WIDTH:2 * KV_WIDTH, :] for pg in pages], axis=1).astype(BF16)
        s_all = _dot(qbd_bf, k_t) + bias_scr[step]
        m_old = m_scr[...]
        m_new = jnp.maximum(m_old, jnp.max(s_all, axis=-1, keepdims=True))
        p_all = jnp.exp2(s_all - m_new) * jnp.where(s_all > 0.5 * NEG, 1.0, 0.0)
        alpha = jnp.exp2(m_old - m_new)
        l_scr[...] = alpha * l_scr[...] + jnp.sum(p_all, axis=-1, keepdims=True)
        acc_scr[1] = alpha * acc_scr[1] + _dot_nt(p_all.astype(BF16), v_t)
        m_scr[...] = m_new

    def epilogue(view):
        (q_ref, nsakv_ref, winkv_ref, ng_ref, poolu_ref, memq_ref, kcvc_ref, wincache_ref, poolstate_ref, memcache_ref,
         onsa_ref, pooly_ref, memo_ref, winout_ref, qbd_scr, bias_scr, m_scr, l_scr, acc_scr) = view
        qbd = qbd_scr[...]
        qbd_bf = qbd.astype(BF16)
        ks_new = _rows8(nsakv_ref[:, 2 * KV_WIDTH:3 * KV_WIDTH])
        vs_new = _rows8(nsakv_ref[:, 3 * KV_WIDTH:4 * KV_WIDTH])
        s_new = jnp.sum(qbd * ks_new, axis=-1, keepdims=True)
        m_old = m_scr[...]
        m_fin = jnp.maximum(m_old, s_new)
        alpha = jnp.exp2(m_old - m_fin)
        p_new = jnp.exp2(s_new - m_fin)
        l_fin = alpha * l_scr[...] + p_new
        osel = (alpha * acc_scr[1] + p_new * vs_new) * (1.0 / l_fin)
        wb = wincache_ref.shape[1]
        kw_new = _rows8(winkv_ref[:, 0:KV_WIDTH])
        vw_new = _rows8(winkv_ref[:, KV_WIDTH:2 * KV_WIDTH])
        s_w = _dot(qbd_bf, wincache_ref[0:KV_WIDTH, :].astype(BF16))
        kpos = pos - wb + _iota((8, wb), 1)
        wmask = (kpos <= pos) & (kpos > pos - WINDOW)
        s_wn = jnp.sum(qbd * kw_new, axis=-1, keepdims=True)
        s_w = jnp.where(wmask, s_w, NEG)
        m_w = jnp.maximum(jnp.max(s_w, axis=-1, keepdims=True), s_wn)
        p_w = jnp.exp2(s_w - m_w) * jnp.where(wmask, 1.0, 0.0)
        p_wn = jnp.exp2(s_wn - m_w)
        l_w = jnp.sum(p_w, axis=-1, keepdims=True) + p_wn
        owin = (_dot_nt(p_w.astype(BF16), wincache_ref[KV_WIDTH:2 * KV_WIDTH, :].astype(BF16)) + p_wn * vw_new) * (1.0 / l_w)
        ng8 = _rows8(ng_ref[...])
        glane = _iota((8, NSAG_PAD), 1)
        grow = _iota((8, NSAG_PAD), 0)
        gate = lambda br: jnp.sum(jnp.where(glane == N_NSA_BRANCH * grow + br, ng8, 0.0), axis=-1, keepdims=True)
        o8 = jnp.where(head_lanes, gate(0) * acc_scr[0] + gate(1) * osel + gate(2) * owin, 0.0)
        o512 = _dot_split_lhs(o8, unfold_ref[...], 3)
        own = (_iota((8, NSA_WIDTH), 1) // HEAD_DIM) == _iota((8, NSA_WIDTH), 0)
        onsa_ref[...] = jnp.sum(jnp.where(own, o512, 0.0), axis=0, keepdims=True)
        mtok = memcache_ref.shape[0] // (2 * MEM_HEADS)
        outs = []
        for h in range(MEM_HEADS):
            k_h = memcache_ref[pl.ds(h, mtok, stride=2 * MEM_HEADS), :].astype(BF16)
            v_h = memcache_ref[pl.ds(MEM_HEADS + h, mtok, stride=2 * MEM_HEADS), :].astype(BF16)
            q_h = _rows8(memq_ref[:, h * MEM_HEAD_DIM:(h + 1) * MEM_HEAD_DIM].astype(F32)).astype(BF16)
            s_m = _dot_nt(q_h, k_h) * MEM_HEAD_DIM ** -0.5
            p_m = jnp.exp(s_m - jnp.max(s_m, axis=-1, keepdims=True))
            p_m = p_m * (1.0 / jnp.sum(p_m, axis=-1, keepdims=True))
            outs.append(_dot(p_m.astype(BF16), v_h)[0:1])
        memo_ref[...] = jnp.concatenate(outs, axis=1)
        u_new = poolu_ref[...]
        state = poolstate_ref[...]
        srow = _iota((POOL_BUF, POOL_GROUP_WIDTH), 0)
        ys = []
        for gi, win in enumerate(POOL_WINDOWS):
            lo = gi * POOL_GROUP_WIDTH
            u_g = u_new[:, lo:lo + POOL_GROUP_WIDTH]
            tail = jnp.sum(jnp.where(srow >= POOL_BUF - (win - 1), state[:, lo:lo + POOL_GROUP_WIDTH], 0.0),
                           axis=0, keepdims=True)
            d = (tail + u_g) / float(min(pos + 1, win)) - u_g
            ys.append(_dot(_rows8(d).astype(BF16), wpool_ref[gi])[0:1])
        pooly_ref[...] = jnp.concatenate(ys, axis=1) * pscale_ref[...]
        nf = 2 * KV_WIDTH
        eye_f = _iota((nf, nf), 0) == _iota((nf, nf), 1)
        new_col = jnp.sum(jnp.where(eye_f, jnp.broadcast_to(winkv_ref[...], (nf, nf)), 0.0), axis=1, keepdims=True)
        shifted = pltpu.roll(wincache_ref[...], wb - 1, axis=1)
        winout_ref[...] = jnp.where(_iota((nf, wb), 1) == wb - 1, new_col, shifted)

    @pl.when(step == 0)
    def _():
        for view in views:
            prologue(view)

    for view, pages in zip(views, row_pages):
        stream(view, pages)

    @pl.when(step == n_steps - 1)
    def _():
        for view in views:
            epilogue(view)


def _sample_attn(l, q, nsakv, winkv, ng, pool_u, memq, kcvc_s, cache_t, wincache_t, poolstate, memcache, page_table,
                 w, consts):
    bd, n_pages = page_table.shape
    npg = min(PAGES_PER_STEP_SEL, n_pages)
    n_steps = n_pages // npg
    past_len = n_pages * PAGE_SIZE
    ncs = past_len // CMP_BLOCK
    nsp = past_len // SEL_BLOCK
    wb = wincache_t.shape[3]
    mrows = memcache.shape[2]
    rb = ROWS_PER_DECODE_STEP if bd % ROWS_PER_DECODE_STEP == 0 else 1
    row = lambda w: pl.BlockSpec((rb, 1, w), lambda b, s, pt: (b, 0, 0))
    const = lambda shape: pl.BlockSpec(shape, lambda b, s, pt: (0,) * len(shape))
    per_b = lambda r, c: pl.BlockSpec((None, rb, r, c), lambda b, s, pt: (l, b, 0, 0))
    page_specs = []
    for i in range(rb):
        page_specs += _page_specs(npg, 1, lambda g: l, lambda g, i=i: g[0] * rb + i, lambda g: g[1])
    in_specs = (
        page_specs
        + [row(NSA_WIDTH), row(4 * KV_WIDTH), row(2 * KV_WIDTH), row(NSAG_PAD), row(POOL_WIDTH), row(MEM_WIDTH),
           per_b(2 * KV_WIDTH, ncs), per_b(2 * KV_WIDTH, wb), per_b(POOL_BUF, POOL_WIDTH), per_b(mrows, MEM_HEAD_DIM),
           const((nsp, past_len)), const((ncs, nsp)), const((NSA_WIDTH, KV_WIDTH)), const((KV_WIDTH, NSA_WIDTH)),
           pl.BlockSpec((None, POOL_GROUPS, POOL_GROUP_WIDTH, POOL_GROUP_WIDTH), lambda b, s, pt: (l, 0, 0, 0)),
           pl.BlockSpec((None, 1, POOL_WIDTH), lambda b, s, pt: (l, 0, 0))])
    grid_spec = pltpu.PrefetchScalarGridSpec(
        num_scalar_prefetch=1,
        grid=(bd // rb, n_steps),
        in_specs=in_specs,
        out_specs=[row(NSA_WIDTH), row(POOL_WIDTH), row(MEM_WIDTH),
                   pl.BlockSpec((rb, 2 * KV_WIDTH, wb), lambda b, s, pt: (b, 0, 0))],
        scratch_shapes=[pltpu.VMEM((rb, 8, KV_WIDTH), F32), pltpu.VMEM((rb, n_steps, 8, npg * PAGE_SIZE), F32),
                        pltpu.VMEM((rb, 8, 1), F32), pltpu.VMEM((rb, 8, 1), F32), pltpu.VMEM((rb, 2, 8, KV_WIDTH), F32)],
    )
    out = jax.ShapeDtypeStruct((bd, 1, NSA_WIDTH), F32)
    return pl.pallas_call(
        functools.partial(_sample_attn_kernel, past_len=past_len, rows_per_step=rb),
        grid_spec=grid_spec,
        out_shape=[out, out, out, jax.ShapeDtypeStruct((bd, 2 * KV_WIDTH, wb), F32)],
        compiler_params=_params("parallel", "arbitrary"),
        name="sample_attn",
    )(page_table, *([cache_t] * (rb * npg)), q, nsakv, winkv, ng, pool_u, memq, kcvc_s, wincache_t, poolstate, memcache,
      consts["sel_expand_s"], consts["pair_s"], consts["fold"], consts["unfold"], w["w_pool"], w["pool_scale"])


def _block_diag_ones(n, seg):
    return (jnp.arange(n)[:, None] // seg == jnp.arange(n)[None, :] // seg).astype(BF16)


def _constants(t, past_len):
    nc, ns = t // CMP_BLOCK, t // SEL_BLOCK
    ratio = SEL_BLOCK // CMP_BLOCK
    pair_t = (jnp.arange(ns)[:, None] == jnp.arange(nc)[None, :] // ratio).astype(BF16)
    ncs, nsp = past_len // CMP_BLOCK, past_len // SEL_BLOCK
    sel_expand_s = (jnp.arange(nsp)[:, None] == (jnp.arange(past_len) // SEL_BLOCK)[None, :]).astype(BF16)
    pair_s = (jnp.arange(ncs)[:, None] // ratio == jnp.arange(nsp)[None, :]).astype(BF16)
    c = jnp.arange(NSA_WIDTH)
    fold_col = (c // (GROUP * HEAD_DIM)) * HEAD_DIM + c % HEAD_DIM
    fold = (fold_col[:, None] == jnp.arange(KV_WIDTH)[None, :]).astype(BF16)
    return {
        "ones64": _block_diag_ones(V7X_MXU_DIM, HEAD_DIM), "ones128": _block_diag_ones(V7X_MXU_DIM, MEM_HEAD_DIM),
        "ones64_kv": _block_diag_ones(KV_WIDTH, HEAD_DIM),
        "pair_t": pair_t,
        "sel_expand_s": sel_expand_s, "pair_s": pair_s, "fold": fold, "unfold": fold.T,
    }


def kernel(x_prompt, x_sample, mem_prompt, cache_nsa_kv, cache_win_kv, state_pool, cache_mem_kv, page_table,
           attn_norm_g, w_in, nsa_q_g, nsa_kc_g, nsa_ks_g, nsa_kw_g, cmp_pe_k, cmp_pe_v, cmp_wk, cmp_wv,
           w_pool, pool_scale, mem_norm_g, w_mem_kv, mem_q_g, mem_k_g, w_up_nsa, w_up_pool, w_up_mem, w_out,
           ffn_norm_g, w_gate_up, w_down):
    depth = w_in.shape[0]
    b, t, _ = x_prompt.shape
    bd = x_sample.shape[0]
    n_pages = page_table.shape[1]
    past_len = n_pages * PAGE_SIZE
    mtok = mem_prompt.shape[1]
    wb = cache_win_kv.shape[2]
    assert x_sample.shape[1] == 1 and wb == WINDOW and past_len >= WINDOW and mtok == cache_mem_kv.shape[2]
    consts = _constants(t, past_len)

    nsag_lo = NSA_WIDTH + 6 * KV_WIDTH
    nsag_hi = nsag_lo + N_NSA_BRANCH * N_HEADS
    w_t = w_in.transpose(0, 2, 1)
    w_in_t = jnp.concatenate(
        [w_t[:, :nsag_lo], w_t[:, nsag_hi:], w_t[:, nsag_lo:nsag_hi],
         jnp.zeros((depth, NSAG_PAD - N_NSA_BRANCH * N_HEADS, D_MODEL), w_in.dtype)], axis=1).astype(BF16)
    eye_kv = jnp.eye(KV_HEADS, dtype=F32)
    rows = lambda g, reps: jnp.tile(g, (1, reps))[:, None, :].astype(F32)
    w = {
        "attn_g": attn_norm_g[:, None, :], "w_in_t": w_in_t,
        "q_g": rows(nsa_q_g, N_HEADS), "ks_g": rows(nsa_ks_g, KV_HEADS), "kw_g": rows(nsa_kw_g, KV_HEADS),
        "mq_g": rows(mem_q_g, MEM_HEADS), "kc_g": rows(nsa_kc_g, KV_HEADS), "mk_g": rows(mem_k_g, MEM_HEADS),
        "cmp_pe": jnp.concatenate([jnp.tile(cmp_pe_k, (1, 1, KV_HEADS)), jnp.tile(cmp_pe_v, (1, 1, KV_HEADS))], axis=2),
        "cmp_wk": jax.vmap(lambda m: jnp.kron(eye_kv, m))(cmp_wk), "cmp_wv": jax.vmap(lambda m: jnp.kron(eye_kv, m))(cmp_wv),
        "w_pool": w_pool.astype(BF16), "pool_scale": pool_scale[:, None, :],
        "mem_g": mem_norm_g[:, None, :], "w_mem_kv": w_mem_kv.astype(BF16),
        "w_mem_v_t": w_mem_kv[:, :, MEM_WIDTH:].transpose(0, 2, 1).astype(BF16),
        "w_up_nsa": w_up_nsa.astype(BF16), "w_up_pool": w_up_pool.astype(BF16), "w_up_mem": w_up_mem.astype(BF16),
        "w_out": w_out.astype(BF16), "ffn_g": ffn_norm_g[:, None, :],
        "w_gate_up": w_gate_up.astype(BF16), "w_down": w_down.astype(BF16),
    }
    swap = lambda a: a.transpose(0, 2, 1)
    cmp_w = {"cmp_pe_t": swap(w["cmp_pe"]), "cmp_wk_t": swap(w["cmp_wk"]), "cmp_wv_t": swap(w["cmp_wv"]),
             "kc_g_col": swap(w["kc_g"]), "ones64_kv": consts["ones64_kv"]}

    cache_t = cache_nsa_kv.transpose(0, 1, 3, 4, 5, 2).reshape(depth, cache_nsa_kv.shape[1], 4 * KV_WIDTH, PAGE_SIZE)
    wincache_t = cache_win_kv.transpose(0, 1, 3, 4, 5, 2).reshape(depth, bd, 2 * KV_WIDTH, wb)
    memcache = cache_mem_kv.reshape(depth, bd, cache_mem_kv.shape[2] * 2 * MEM_HEADS, MEM_HEAD_DIM)
    kcvc_s = _compress_pages(cache_t, page_table, cmp_w)

    xp = x_prompt.reshape(b * t, D_MODEL)
    xs = x_sample.reshape(bd, D_MODEL)
    nsa_p, nsa_s, win_p, win_s, pool_p, pool_s, mem_p = [], [], [], [], [], [], []
    for l in range(depth):
        q, nsakv, winkv, kext, kwbf, vt, pool_u, memq, mg, ng = _inproj(xp, l, w, consts, t)
        mkv, mk, mvt = _memkv(mem_prompt, l, w, consts)
        nsakv3 = nsakv.reshape(b, t, 4 * KV_WIDTH)
        kc, vct = _compress_prompt(nsakv3, l, w, consts)
        pool3 = pool_u.reshape(b, t, POOL_WIDTH)
        by_b = lambda a: a.reshape(b, t, a.shape[-1])
        onsa, pooly, memo = _prompt_attn(
            by_b(q), kc, vct, by_b(kext), by_b(kwbf), vt.reshape(b, -1, vt.shape[1], vt.shape[2]), pool3, by_b(memq),
            mk, mvt, by_b(ng), l, w, consts)
        xp = _merge_ffn(xp, onsa.reshape(b * t, NSA_WIDTH), pooly.reshape(b * t, POOL_WIDTH),
                        memo.reshape(b * t, MEM_WIDTH), mg, l, w)
        nsa_p.append(nsakv3.reshape(b, t, 4, KV_HEADS, HEAD_DIM))
        win_rows = winkv.reshape(b, t, 2 * KV_WIDTH)[:, t - min(WINDOW, t):]
        win_p.append(win_rows.reshape(b, win_rows.shape[1], 2, KV_HEADS, HEAD_DIM))
        pool_p.append(pool3[:, t - POOL_BUF:])
        mem_p.append(mkv.reshape(b, mtok, 2, MEM_HEADS, MEM_HEAD_DIM))
        q, nsakv, winkv, _, _, _, pool_u, memq, mg, ng = _inproj(xs, l, w, consts, bd)
        r3 = lambda a: a.reshape(bd, 1, a.shape[-1])
        onsa, pooly, memo, win_t = _sample_attn(l, r3(q), r3(nsakv), r3(winkv), r3(ng), r3(pool_u), r3(memq), kcvc_s,
                                                cache_t, wincache_t, state_pool, memcache, page_table, w, consts)
        xs = _merge_ffn(xs, onsa.reshape(bd, NSA_WIDTH), pooly.reshape(bd, POOL_WIDTH), memo.reshape(bd, MEM_WIDTH), mg, l, w)
        nsa_s.append(nsakv.reshape(bd, 1, 4, KV_HEADS, HEAD_DIM))
        win_s.append(win_t)
        pool_s.append(jnp.concatenate([state_pool[l][:, 1:], pool_u.reshape(bd, 1, POOL_WIDTH)], axis=1))
    win_s_out = jnp.stack(win_s).reshape(depth, bd, 2, KV_HEADS, HEAD_DIM, wb).transpose(0, 1, 5, 2, 3, 4)
    return (xp.reshape(b, t, D_MODEL), xs.reshape(bd, 1, D_MODEL), jnp.stack(nsa_p), jnp.stack(nsa_s),
            jnp.stack(win_p), win_s_out, jnp.stack(pool_p), jnp.stack(pool_s), jnp.stack(mem_p))
```

```python
import functools

import jax
import jax.numpy as jnp
from jax import lax
from jax.experimental import pallas as pl
from jax.experimental.pallas import tpu as pltpu

F32 = jnp.float32
BF16 = jnp.bfloat16

D_MODEL = 1024
PAGE_SIZE = 128
N_HEADS = 8
HEAD_DIM = 64
KV_HEADS = 2
GROUP = N_HEADS // KV_HEADS
NSA_WIDTH = N_HEADS * HEAD_DIM
KV_WIDTH = KV_HEADS * HEAD_DIM
CMP_BLOCK = 32
SEL_BLOCK = 64
TOP_K = 16
WINDOW = 512
N_NSA_BRANCH = 3
POOL_WIDTH = 512
POOL_GROUPS = 4
POOL_GROUP_WIDTH = POOL_WIDTH // POOL_GROUPS
POOL_WINDOWS = (2, 4, 8, 16)
POOL_BUF = 15
MEM_HEADS = 4
MEM_HEAD_DIM = 128
MEM_WIDTH = MEM_HEADS * MEM_HEAD_DIM
N_BRANCH = 3
D_FF = ((8 * D_MODEL // 3 + 255) // 256) * 256
EPS = 1e-6
LOG2_E = 1.4426950408889634
NEG = -1e30
BIG = 1e9

V7X_LANES = 128
V7X_SUBLANES = 8
ONES_ROWS = 2 * V7X_SUBLANES
V7X_VMEM_LIMIT_BYTES = 56 * 1024 * 1024

C_Q = 0
C_NSAKV = C_Q + NSA_WIDTH
C_WINKV = C_NSAKV + 4 * KV_WIDTH
C_POOL = C_WINKV + 2 * KV_WIDTH
C_MEMQ = C_POOL + POOL_WIDTH
C_MERGE = C_MEMQ + MEM_WIDTH
C_NSAG = C_MERGE + N_BRANCH * D_MODEL
NSAG_PAD = V7X_LANES
PROJ_PAD = C_NSAG + NSAG_PAD

TM_PROJ = 512
TQ = 512
SEL_CHUNKS_PER_STEP = 1
V7X_MXU_DIM = 256
FF_CHUNKS = (6 * V7X_MXU_DIM, D_FF - 6 * V7X_MXU_DIM)
PAGES_PER_STEP_CMP = 64
PAGES_PER_STEP_SEL = 64
ROWS_PER_DECODE_STEP = 2


def _dot(a, b):
    return jnp.dot(a, b, preferred_element_type=F32)


def _dot_nt(a, b):
    return lax.dot_general(a, b, (((1,), (1,)), ((), ())), preferred_element_type=F32)


def _dot_f32(a, b):
    return jnp.dot(a, b, preferred_element_type=F32, precision=lax.Precision.HIGHEST)


def _split_bf16(a, n):
    parts, r = [], a
    for _ in range(n):
        p = r.astype(BF16)
        parts.append(p)
        r = r - p.astype(F32)
    return parts


def _dot_split_lhs(a, b, n):
    return sum(_dot(p, b) for p in _split_bf16(a, n))


def _dot_split_rhs(a, b, n):
    return sum(_dot(a, p) for p in _split_bf16(b, n))


def _iota(shape, dim):
    return lax.broadcasted_iota(jnp.int32, shape, dim)


def _rms_rows(x, g):
    return x * lax.rsqrt(jnp.mean(x * x, axis=-1, keepdims=True) + EPS) * g


def _seg_rmsnorm(v, ones_bd, gain, seg):
    n = ones_bd.shape[0]
    sq = (v * v).astype(BF16)
    ssq = jnp.concatenate([_dot(sq[:, lo:lo + n], ones_bd) for lo in range(0, v.shape[1], n)], axis=1)
    return v * lax.rsqrt(ssq * (1.0 / seg) + EPS) * gain


def _const_spec(shape):
    return pl.BlockSpec(shape, lambda *_: (0,) * len(shape), pipeline_mode=pl.Buffered(1))


def _layer_spec(l, shape):
    return pl.BlockSpec((None,) + tuple(shape), lambda *_: (l,) + (0,) * len(shape), pipeline_mode=pl.Buffered(1))


def _params(*sem):
    return pltpu.CompilerParams(dimension_semantics=sem, vmem_limit_bytes=V7X_VMEM_LIMIT_BYTES)


def _inproj_kernel(x_ref, g_ref, w_ref, qg_ref, ksg_ref, kwg_ref, mqg_ref, ones64_ref, ones128_ref,
                   q_ref, nsakv_ref, winkv_ref, kext_ref, kwbf_ref, vt_ref, pool_ref, memq_ref, mg_ref, ng_ref,
                   *, seq_len):
    tm = x_ref.shape[0]
    hb = _rms_rows(x_ref[...], g_ref[...]).astype(BF16)

    def proj(lo, width):
        return _dot_nt(hb, w_ref[lo:lo + width, :])

    ones_kv = ones64_ref[0:KV_WIDTH, 0:KV_WIDTH]
    q = _seg_rmsnorm(proj(C_Q, NSA_WIDTH), ones64_ref[...], qg_ref[...], HEAD_DIM)
    q_ref[...] = (q * (HEAD_DIM ** -0.5 * LOG2_E)).astype(BF16)
    kcvc = proj(C_NSAKV, 2 * KV_WIDTH)
    ksvs = proj(C_NSAKV + 2 * KV_WIDTH, 2 * KV_WIDTH)
    kwvw = proj(C_WINKV, 2 * KV_WIDTH)
    ks = _seg_rmsnorm(ksvs[:, 0:KV_WIDTH], ones_kv, ksg_ref[...], HEAD_DIM)
    vs = ksvs[:, KV_WIDTH:2 * KV_WIDTH]
    kw = _seg_rmsnorm(kwvw[:, 0:KV_WIDTH], ones_kv, kwg_ref[...], HEAD_DIM)
    vw = kwvw[:, KV_WIDTH:2 * KV_WIDTH]
    nsakv_ref[:, 0:2 * KV_WIDTH] = kcvc
    nsakv_ref[:, 2 * KV_WIDTH:3 * KV_WIDTH] = ks
    nsakv_ref[:, 3 * KV_WIDTH:4 * KV_WIDTH] = vs
    winkv_ref[:, 0:KV_WIDTH] = kw
    winkv_ref[:, KV_WIDTH:2 * KV_WIDTH] = vw
    t_seq = (pl.program_id(0) * tm + _iota((tm, HEAD_DIM), 0)) % seq_len
    onehot = jnp.where(t_seq // SEL_BLOCK == _iota((tm, HEAD_DIM), 1), 1.0, 0.0).astype(BF16)
    for k in range(KV_HEADS):
        kext_ref[:, 2 * k * HEAD_DIM:(2 * k + 1) * HEAD_DIM] = ks[:, k * HEAD_DIM:(k + 1) * HEAD_DIM].astype(BF16)
        kext_ref[:, (2 * k + 1) * HEAD_DIM:(2 * k + 2) * HEAD_DIM] = onehot
    kwbf_ref[...] = kw.astype(BF16)
    vs_lo, vw_lo = C_NSAKV + 3 * KV_WIDTH, C_WINKV + KV_WIDTH
    vt_ref[0:KV_WIDTH, :] = _dot_nt(w_ref[vs_lo:vs_lo + KV_WIDTH, :], hb).astype(BF16)
    vt_ref[KV_WIDTH:2 * KV_WIDTH, :] = _dot_nt(w_ref[vw_lo:vw_lo + KV_WIDTH, :], hb).astype(BF16)
    pool_ref[...] = proj(C_POOL, POOL_WIDTH)
    mq = _seg_rmsnorm(proj(C_MEMQ, MEM_WIDTH), ones128_ref[...], mqg_ref[...], MEM_HEAD_DIM)
    memq_ref[...] = mq.astype(BF16)
    for j in range(N_BRANCH):
        mg_ref[:, j * D_MODEL:(j + 1) * D_MODEL] = jax.nn.sigmoid(proj(C_MERGE + j * D_MODEL, D_MODEL))
    ng_ref[...] = jax.nn.sigmoid(proj(C_NSAG, NSAG_PAD))


def _inproj(x, l, w, consts, seq_len):
    n = x.shape[0]
    lay = functools.partial(_layer_spec, l)
    tm = min(TM_PROJ, n)
    row = lambda w: pl.BlockSpec((tm, w), lambda i: (i, 0))
    widths = (NSA_WIDTH, 4 * KV_WIDTH, 2 * KV_WIDTH, 2 * KV_WIDTH, KV_WIDTH, None, POOL_WIDTH, MEM_WIDTH,
              N_BRANCH * D_MODEL, NSAG_PAD)
    dtypes = (BF16, F32, F32, BF16, BF16, BF16, F32, BF16, F32, F32)
    vt_spec = pl.BlockSpec((None, 2 * KV_WIDTH, tm), lambda i: (i, 0, 0))
    vt_shape = jax.ShapeDtypeStruct((n // tm, 2 * KV_WIDTH, tm), BF16)
    return pl.pallas_call(
        functools.partial(_inproj_kernel, seq_len=seq_len),
        grid=(n // tm,),
        in_specs=[row(D_MODEL), lay((1, D_MODEL)), lay((PROJ_PAD, D_MODEL)),
                  lay((1, NSA_WIDTH)), lay((1, KV_WIDTH)), lay((1, KV_WIDTH)), lay((1, MEM_WIDTH)),
                  _const_spec((V7X_MXU_DIM, V7X_MXU_DIM)), _const_spec((V7X_MXU_DIM, V7X_MXU_DIM))],
        out_specs=[vt_spec if wd is None else row(wd) for wd in widths],
        out_shape=[vt_shape if wd is None else jax.ShapeDtypeStruct((n, wd), d) for wd, d in zip(widths, dtypes)],
        compiler_params=_params("parallel"),
        name="inproj",
    )(x, w["attn_g"], w["w_in_t"], w["q_g"], w["ks_g"], w["kw_g"], w["mq_g"], consts["ones64"], consts["ones128"])


def _memkv_kernel(m_ref, g_ref, w_ref, wvt_ref, kg_ref, ones128_ref, kv_ref, kbf_ref, vt_ref):
    hb = _rms_rows(m_ref[...], g_ref[...]).astype(BF16)
    k = _seg_rmsnorm(_dot(hb, w_ref[:, 0:MEM_WIDTH]), ones128_ref[...], kg_ref[...], MEM_HEAD_DIM)
    kv_ref[:, 0:MEM_WIDTH] = k
    kv_ref[:, MEM_WIDTH:2 * MEM_WIDTH] = _dot(hb, w_ref[:, MEM_WIDTH:2 * MEM_WIDTH])
    kbf_ref[...] = k.astype(BF16)
    vt_ref[...] = _dot_nt(wvt_ref[...], hb).astype(BF16)


def _memkv(mem, l, w, consts):
    b, mtok, _ = mem.shape
    lay = functools.partial(_layer_spec, l)
    blk = lambda r, c: pl.BlockSpec((None, r, c), lambda i: (i, 0, 0))
    return pl.pallas_call(
        _memkv_kernel,
        grid=(b,),
        in_specs=[blk(mtok, D_MODEL), lay((1, D_MODEL)), lay((D_MODEL, 2 * MEM_WIDTH)),
                  lay((MEM_WIDTH, D_MODEL)), lay((1, MEM_WIDTH)), _const_spec((V7X_MXU_DIM, V7X_MXU_DIM))],
        out_specs=[blk(mtok, 2 * MEM_WIDTH), blk(mtok, MEM_WIDTH), blk(MEM_WIDTH, mtok)],
        out_shape=[jax.ShapeDtypeStruct((b, mtok, 2 * MEM_WIDTH), F32), jax.ShapeDtypeStruct((b, mtok, MEM_WIDTH), BF16),
                   jax.ShapeDtypeStruct((b, MEM_WIDTH, mtok), BF16)],
        compiler_params=_params("parallel"),
        name="memkv",
    )(mem, w["mem_g"], w["w_mem_kv"], w["w_mem_v_t"], w["mk_g"], consts["ones128"])


def _compress_prompt_kernel(kv_ref, pe_ref, wk_ref, wv_ref, kcg_ref, ones64_ref, kc_ref, vct_ref):
    t = kv_ref.shape[0]
    nc = t // CMP_BLOCK
    pooled = jnp.sum(kv_ref[...].reshape(nc, CMP_BLOCK, 2 * KV_WIDTH), axis=1) * (1.0 / CMP_BLOCK)
    pooled = pooled + jnp.sum(pe_ref[...], axis=0, keepdims=True) * (1.0 / CMP_BLOCK)
    k = _dot_f32(pooled[:, 0:KV_WIDTH], wk_ref[...])
    v = _dot_f32(pooled[:, KV_WIDTH:2 * KV_WIDTH], wv_ref[...])
    kc_ref[...] = _seg_rmsnorm(k, ones64_ref[...], kcg_ref[...], HEAD_DIM).astype(BF16)
    vct_ref[...] = v.T.astype(BF16)


def _compress_prompt(nsakv, l, w, consts):
    b, t, _ = nsakv.shape
    nc = t // CMP_BLOCK
    return pl.pallas_call(
        _compress_prompt_kernel,
        grid=(b,),
        in_specs=[pl.BlockSpec((None, t, 2 * KV_WIDTH), lambda i: (i, 0, 0)),
                  _layer_spec(l, (CMP_BLOCK, 2 * KV_WIDTH)), _layer_spec(l, (KV_WIDTH, KV_WIDTH)),
                  _layer_spec(l, (KV_WIDTH, KV_WIDTH)), _layer_spec(l, (1, KV_WIDTH)), _const_spec((KV_WIDTH, KV_WIDTH))],
        out_specs=[pl.BlockSpec((None, nc, KV_WIDTH), lambda i: (i, 0, 0)),
                   pl.BlockSpec((None, KV_WIDTH, nc), lambda i: (i, 0, 0))],
        out_shape=[jax.ShapeDtypeStruct((b, nc, KV_WIDTH), BF16), jax.ShapeDtypeStruct((b, KV_WIDTH, nc), BF16)],
        compiler_params=_params("parallel"),
        name="compress_prompt",
    )(nsakv, w["cmp_pe"], w["cmp_wk"], w["cmp_wv"], w["kc_g"], consts["ones64_kv"])


def _masked_softmax_rows(s, mask):
    maskf = jnp.where(mask, 1.0, 0.0)
    s = jnp.where(mask[None], s, NEG)
    m = jnp.max(s, axis=-1, keepdims=True)
    p = jnp.exp2(s - m) * maskf[None]
    den = jnp.maximum(jnp.sum(p, axis=-1, keepdims=True), 1e-30)
    return p * (1.0 / den)


def _stack_heads(q, k):
    return jnp.concatenate([q[:, (k * GROUP + g) * HEAD_DIM:(k * GROUP + g + 1) * HEAD_DIM] for g in range(GROUP)], axis=0)


def _transpose_tiles(x):
    r, c = x.shape
    n = V7X_LANES
    return jnp.concatenate(
        [jnp.concatenate([x[i * n:(i + 1) * n, j * n:(j + 1) * n].T for i in range(r // n)], axis=1) for j in range(c // n)],
        axis=0)


def _topk_select_t(score_t, n_sel):
    ns, tq = score_t.shape
    sub = V7X_SUBLANES
    tiles = [score_t[v * sub:(v + 1) * sub] for v in range(ns // sub)]
    row = _iota((sub, tq), 0)
    cnts = [jnp.zeros((sub, tq), F32) for _ in tiles]
    for j in range(ns):
        r = jnp.broadcast_to(score_t[j:j + 1, :], (sub, tq))
        for v, tile in enumerate(tiles):
            if v * sub > j:
                wins = r >= tile
            elif v * sub + sub - 1 < j:
                wins = r > tile
            else:
                wins = jnp.where(row + v * sub > j, jnp.where(r >= tile, 1.0, 0.0), jnp.where(r > tile, 1.0, 0.0)) > 0.5
            cnts[v] = cnts[v] + jnp.where(wins, 1.0, 0.0)
    return jnp.where(jnp.concatenate(cnts, axis=0) < n_sel, 1.0, 0.0)


def _with_ones_rows(v_t):
    return jnp.concatenate([v_t, jnp.ones((ONES_ROWS, v_t.shape[1]), BF16)], axis=0)


def _pv_and_sum(v_t, p):
    d = v_t.shape[0]
    r = _dot(_with_ones_rows(v_t), p.astype(BF16))
    return r[0:d], r[d:d + 1]


def _online_softmax_step(state, s_t, v_t):
    m, l, acc = state
    m_new = jnp.maximum(m, jnp.max(s_t, axis=0, keepdims=True))
    alpha = jnp.exp2(m - m_new)
    pv, psum = _pv_and_sum(v_t, jnp.exp2(s_t - m_new))
    return m_new, alpha * l + psum, alpha * acc + pv


def _merge_softmax_states(states):
    m = states[0][0]
    for st in states[1:]:
        m = jnp.maximum(m, st[0])
    l, acc = 0.0, 0.0
    for m_i, l_i, acc_i in states:
        w = jnp.exp2(m_i - m)
        l, acc = l + w * l_i, acc + w * acc_i
    return m, l, acc


def _prompt_attn_kernel(q_ref, kc_ref, vct_ref, kext_ref, kw_ref, vt_ref, poolc_ref, poolp_ref, memq_ref, mk_ref, mvt_ref,
                        ng_ref, pairt_ref, wpool_ref, pscale_ref,
                        onsa_ref, pooly_ref, memo_ref, pool_scr, part_scr):
    tq = q_ref.shape[0]
    nc = kc_ref.shape[0]
    ns = pairt_ref.shape[0]
    kc = vt_ref.shape[2]
    rows = GROUP * tq
    n_sel = min(TOP_K, ns)
    qi = pl.program_id(1)
    t0 = qi * tq
    q = q_ref[...]

    eye_tq = jnp.where(_iota((tq, tq), 0) == _iota((tq, tq), 1), 1.0, 0.0).astype(BF16)
    blk_t = _iota((ns, tq), 0)
    tpos_t = t0 + _iota((ns, tq), 1)
    cur_t = tpos_t // SEL_BLOCK
    valid_t = blk_t * SEL_BLOCK <= tpos_t
    forced_t = (blk_t == 0) | (blk_t == cur_t) | (blk_t == cur_t - 1)
    tok_of_row = lambda shape: t0 + (_iota(shape, 1) & (tq - 1))
    cmask = (_iota((nc, rows), 0) * CMP_BLOCK + (CMP_BLOCK - 1)) <= tok_of_row((nc, rows))
    cmaskf = jnp.where(cmask, 1.0, 0.0)
    ng_t = _transpose_tiles(ng_ref[...])
    nwb = WINDOW // tq + 1
    rel = _iota((tq, rows), 0) - (_iota((tq, rows), 1) & (tq - 1))
    win_first = jnp.where(rel > 0, 0.0, NEG)
    win_diag = jnp.where(rel <= 0, 0.0, NEG)
    init = (jnp.full((1, rows), NEG, F32), jnp.zeros((1, rows), F32), jnp.zeros((HEAD_DIM, rows), F32))

    def gate_row(k, br):
        idx = [N_NSA_BRANCH * (k * GROUP + g) + br for g in range(GROUP)]
        return jnp.concatenate([ng_t[i:i + 1, :] for i in idx], axis=1)

    q_exts = []
    for k in range(KV_HEADS):
        qk = _stack_heads(q, k)
        s_c = jnp.where(cmask, _dot_nt(kc_ref[:, k * HEAD_DIM:(k + 1) * HEAD_DIM], qk), NEG)
        p = jnp.exp2(s_c - jnp.max(s_c, axis=0, keepdims=True)) * cmaskf
        p = p * (1.0 / jnp.maximum(jnp.sum(p, axis=0, keepdims=True), 1e-30))
        o_cmp = _dot(vct_ref[k * HEAD_DIM:(k + 1) * HEAD_DIM, :], p.astype(BF16))
        imp_t = p[:, 0:tq]
        for g in range(1, GROUP):
            imp_t = imp_t + p[:, g * tq:(g + 1) * tq]
        imp2_t = _dot_split_rhs(pairt_ref[...], imp_t, 3)
        score_t = jnp.where(valid_t, jnp.where(forced_t, BIG, imp2_t), -BIG)
        sel_t = _topk_select_t(score_t, n_sel)
        sel = _dot_nt(eye_tq, sel_t.astype(BF16))
        selneg = jnp.where(sel > 0.5, 0.0, NEG).astype(BF16)
        if ns < HEAD_DIM:
            selneg = jnp.concatenate([selneg, jnp.zeros((tq, HEAD_DIM - ns), BF16)], axis=1)
        q_exts.append(jnp.concatenate([qk, jnp.concatenate([selneg] * GROUP, axis=0)], axis=1))
        s_blocks, v_blocks = [], []
        for i in range(nwb):
            kb = qi - (nwb - 1) + i
            inside = kb >= 0
            kbc = jnp.maximum(kb, 0)
            start = pl.multiple_of(kbc * tq, tq)
            k_b = kw_ref[pl.ds(start, tq), k * HEAD_DIM:(k + 1) * HEAD_DIM]
            s_b = _dot_nt(jnp.where(inside, k_b, jnp.zeros_like(k_b)), qk)
            if i == nwb - 1:
                s_b = s_b + win_diag
            elif i == 0:
                s_b = s_b + win_first
            v_pair = vt_ref[kbc // (kc // tq), KV_WIDTH + k * HEAD_DIM:KV_WIDTH + (k + 1) * HEAD_DIM, :]
            v_b = v_pair[:, 0:tq]
            for part in range(1, kc // tq):
                v_b = jnp.where(kbc % (kc // tq) == part, v_pair[:, part * tq:(part + 1) * tq], v_b)
            s_blocks.append(s_b)
            v_blocks.append(jnp.where(inside, v_b, jnp.zeros_like(v_b)))
        s_w = jnp.concatenate(s_blocks, axis=0)
        pv_w, l_w = _pv_and_sum(jnp.concatenate(v_blocks, axis=1), jnp.exp2(s_w - jnp.max(s_w, axis=0, keepdims=True)))
        o_win = pv_w * (1.0 / l_w)
        part_scr[k] = gate_row(k, 0) * o_cmp + gate_row(k, 2) * o_win

    mq = memq_ref[...]
    outs = []
    for h in range(MEM_HEADS):
        lo = h * MEM_HEAD_DIM
        s_m = _dot_nt(mk_ref[:, lo:lo + MEM_HEAD_DIM], mq[:, lo:lo + MEM_HEAD_DIM]) * MEM_HEAD_DIM ** -0.5
        pv_m, l_m = _pv_and_sum(mvt_ref[lo:lo + MEM_HEAD_DIM, :], jnp.exp(s_m - jnp.max(s_m, axis=0, keepdims=True)))
        outs.append(_transpose_tiles(pv_m * (1.0 / l_m)))
    memo_ref[...] = jnp.concatenate(outs, axis=1).astype(memo_ref.dtype)

    cur = poolc_ref[...]
    pool_scr[0:16, :] = jnp.where(qi > 0, poolp_ref[...], 0.0)
    pool_scr[16:16 + tq, :] = cur
    tpos1 = t0 + _iota((tq, POOL_GROUP_WIDTH), 0) + 1
    ys = []
    for gi, win in enumerate(POOL_WINDOWS):
        lo = gi * POOL_GROUP_WIDTH
        acc = pool_scr[16:16 + tq, lo:lo + POOL_GROUP_WIDTH]
        for sft in range(1, win):
            acc = acc + pool_scr[16 - sft:16 - sft + tq, lo:lo + POOL_GROUP_WIDTH]
        cnt = jnp.minimum(tpos1, win).astype(F32)
        d = acc / cnt - cur[:, lo:lo + POOL_GROUP_WIDTH]
        ys.append(_dot(d.astype(BF16), wpool_ref[gi]))
    pooly_ref[...] = (jnp.concatenate(ys, axis=1) * pscale_ref[...]).astype(pooly_ref.dtype)

    per = SEL_CHUNKS_PER_STEP
    span = per * kc
    n_full = t0 // span

    def sel_scores(k, sc):
        start = pl.multiple_of(sc * span, span)
        return _dot_nt(kext_ref[pl.ds(start, span), 2 * k * HEAD_DIM:(2 * k + 2) * HEAD_DIM], q_exts[k])

    def sel_values(k, sc):
        return jnp.concatenate([vt_ref[sc * per + j, k * HEAD_DIM:(k + 1) * HEAD_DIM, :] for j in range(per)], axis=1)

    def sel_step(sc, states):
        return tuple(_online_softmax_step(states[k], sel_scores(k, sc), sel_values(k, sc)) for k in range(KV_HEADS))

    states = lax.fori_loop(0, n_full, sel_step, (init,) * KV_HEADS)

    causal = (n_full * span + _iota((span, rows), 0)) <= tok_of_row((span, rows))
    head_out = []
    for k in range(KV_HEADS):
        _, l_k, acc_k = _online_softmax_step(states[k], jnp.where(causal, sel_scores(k, n_full), NEG),
                                             sel_values(k, n_full))
        o_k = part_scr[k] + gate_row(k, 1) * (acc_k * (1.0 / l_k))
        head_out += [o_k[:, g * tq:(g + 1) * tq] for g in range(GROUP)]
    pairs = [_transpose_tiles(jnp.concatenate(head_out[2 * j:2 * j + 2], axis=0)) for j in range(N_HEADS // 2)]
    onsa_ref[...] = jnp.concatenate(pairs, axis=1).astype(onsa_ref.dtype)


def _prompt_attn(q, kc, vct, kext, kw, vt, pool_u, memq, mk, mvt, ng, l, w, consts):
    b, t, _ = q.shape
    nc = t // CMP_BLOCK
    ns = t // SEL_BLOCK
    tq = TQ
    chunk = vt.shape[3]
    m = mk.shape[1]
    span = SEL_CHUNKS_PER_STEP * chunk
    assert WINDOW % tq == 0 and chunk % tq == 0 and t % span == 0 and span % tq == 0
    assert ns <= HEAD_DIM and tq % V7X_LANES == 0 and tq & (tq - 1) == 0
    tok = lambda w: pl.BlockSpec((None, tq, w), lambda i, j: (i, j, 0))
    per_b = lambda r, w: pl.BlockSpec((None, r, w), lambda i, j: (i, 0, 0))
    prev_rows = pl.BlockSpec((None, 16, POOL_WIDTH), lambda i, j: (i, jnp.maximum(j * (tq // 16) - 1, 0), 0))
    out = jax.ShapeDtypeStruct((b, t, NSA_WIDTH), BF16)
    return pl.pallas_call(
        _prompt_attn_kernel,
        grid=(b, t // tq),
        in_specs=[tok(NSA_WIDTH), per_b(nc, KV_WIDTH), per_b(KV_WIDTH, nc), per_b(t, 2 * KV_WIDTH), per_b(t, KV_WIDTH),
                  pl.BlockSpec((None, t // chunk, 2 * KV_WIDTH, chunk), lambda i, j: (i, 0, 0, 0)),
                  tok(POOL_WIDTH), prev_rows, tok(MEM_WIDTH), per_b(m, MEM_WIDTH), per_b(MEM_WIDTH, m), tok(NSAG_PAD),
                  _const_spec((ns, nc)),
                  _layer_spec(l, (POOL_GROUPS, POOL_GROUP_WIDTH, POOL_GROUP_WIDTH)), _layer_spec(l, (1, POOL_WIDTH))],
        out_specs=[tok(NSA_WIDTH), tok(POOL_WIDTH), tok(MEM_WIDTH)],
        out_shape=[out, out, out],
        scratch_shapes=[pltpu.VMEM((16 + tq, POOL_WIDTH), F32), pltpu.VMEM((KV_HEADS, HEAD_DIM, GROUP * tq), F32)],
        compiler_params=_params("parallel", "arbitrary"),
        name="prompt_attn",
    )(q, kc, vct, kext, kw, vt, pool_u, pool_u, memq, mk, mvt, ng, consts["pair_t"], w["w_pool"], w["pool_scale"])


def _merge_ffn_kernel(x_ref, onsa_ref, pooly_ref, memo_ref, mg_ref, wn_ref, wp_ref, wm_ref, wo_ref, fg_ref,
                      wgu_ref, wd_ref, y_ref):
    h = (mg_ref[:, 0:D_MODEL] * _dot(onsa_ref[...].astype(BF16), wn_ref[...])
         + mg_ref[:, D_MODEL:2 * D_MODEL] * _dot(pooly_ref[...].astype(BF16), wp_ref[...])
         + mg_ref[:, 2 * D_MODEL:3 * D_MODEL] * _dot(memo_ref[...].astype(BF16), wm_ref[...]))
    x1 = x_ref[...] + _dot(h.astype(BF16), wo_ref[...])
    hn = _rms_rows(x1, fg_ref[...]).astype(BF16)
    acc = x1
    lo = 0
    for width in FF_CHUNKS:
        gate = _dot(hn, wgu_ref[:, lo:lo + width])
        up = _dot(hn, wgu_ref[:, D_FF + lo:D_FF + lo + width])
        act = gate * jax.nn.sigmoid(gate) * up
        acc = acc + _dot(act.astype(BF16), wd_ref[lo:lo + width, :])
        lo += width
    y_ref[...] = acc


def _merge_ffn(x, onsa, pooly, memo, mg, l, w):
    n = x.shape[0]
    lay = functools.partial(_layer_spec, l)
    tm = min(TM_PROJ, n)
    row = lambda w: pl.BlockSpec((tm, w), lambda i: (i, 0))
    return pl.pallas_call(
        _merge_ffn_kernel,
        grid=(n // tm,),
        in_specs=[row(D_MODEL), row(NSA_WIDTH), row(POOL_WIDTH), row(MEM_WIDTH), row(N_BRANCH * D_MODEL),
                  lay((NSA_WIDTH, D_MODEL)), lay((POOL_WIDTH, D_MODEL)), lay((MEM_WIDTH, D_MODEL)),
                  lay((D_MODEL, D_MODEL)), lay((1, D_MODEL)), lay((D_MODEL, 2 * D_FF)), lay((D_FF, D_MODEL))],
        out_specs=row(D_MODEL),
        out_shape=jax.ShapeDtypeStruct((n, D_MODEL), F32),
        compiler_params=_params("parallel"),
        name="merge_ffn",
    )(x, onsa, pooly, memo, mg, w["w_up_nsa"], w["w_up_pool"], w["w_up_mem"], w["w_out"], w["ffn_g"],
      w["w_gate_up"], w["w_down"])


def _page_specs(npg, half, layer_of, batch_of, step_of):
    def spec(i):
        def index_map(*args):
            pt = args[-1]
            grid = args[:-1]
            return (layer_of(grid), pt[batch_of(grid), step_of(grid) * npg + i], half, 0)
        return pl.BlockSpec((None, None, 2 * KV_WIDTH, PAGE_SIZE), index_map)
    return [spec(i) for i in range(npg)]


def _compress_pages_kernel(pt_ref, *refs):
    del pt_ref
    npg = len(refs) - 7
    pages = refs[:npg]
    pe_ref, wkt_ref, wvt_ref, kcg_ref, ones64_ref, poolm_ref, out_ref = refs[npg:]
    x = jnp.concatenate([pg[...] for pg in pages], axis=1)
    pooled = _dot(x.astype(BF16), poolm_ref[...])
    pooled = pooled + jnp.sum(pe_ref[...], axis=1, keepdims=True) * (1.0 / CMP_BLOCK)
    k = _dot_f32(wkt_ref[...], pooled[0:KV_WIDTH])
    v = _dot_f32(wvt_ref[...], pooled[KV_WIDTH:2 * KV_WIDTH])
    ssq = _dot_split_rhs(ones64_ref[...], k * k, 2)
    out_ref[0:KV_WIDTH, :] = k * lax.rsqrt(ssq * (1.0 / HEAD_DIM) + EPS) * kcg_ref[...]
    out_ref[KV_WIDTH:2 * KV_WIDTH, :] = v


def _compress_pages(cache_t, page_table, w):
    depth = cache_t.shape[0]
    bd, n_pages = page_table.shape
    npg = min(PAGES_PER_STEP_CMP, n_pages)
    per_page = PAGE_SIZE // CMP_BLOCK
    per_step = npg * per_page
    ncs = n_pages * per_page
    lconst = lambda r, c: pl.BlockSpec((None, r, c), lambda l, b, s, pt: (l, 0, 0))
    const = lambda r, c: pl.BlockSpec((r, c), lambda l, b, s, pt: (0, 0))
    grid_spec = pltpu.PrefetchScalarGridSpec(
        num_scalar_prefetch=1,
        grid=(depth, bd, n_pages // npg),
        in_specs=_page_specs(npg, 0, lambda g: g[0], lambda g: g[1], lambda g: g[2])
        + [lconst(2 * KV_WIDTH, CMP_BLOCK), lconst(KV_WIDTH, KV_WIDTH), lconst(KV_WIDTH, KV_WIDTH), lconst(KV_WIDTH, 1),
           const(KV_WIDTH, KV_WIDTH), const(npg * PAGE_SIZE, per_step)],
        out_specs=pl.BlockSpec((None, None, 2 * KV_WIDTH, per_step), lambda l, b, s, pt: (l, b, 0, s)),
    )
    rows = jnp.arange(npg * PAGE_SIZE)[:, None] // CMP_BLOCK
    poolm = jnp.where(rows == jnp.arange(per_step)[None, :], 1.0 / CMP_BLOCK, 0.0).astype(BF16)
    return pl.pallas_call(
        _compress_pages_kernel,
        grid_spec=grid_spec,
        out_shape=jax.ShapeDtypeStruct((depth, bd, 2 * KV_WIDTH, ncs), F32),
        compiler_params=_params("parallel", "parallel", "arbitrary"),
        name="compress_pages",
    )(page_table, *([cache_t] * npg), w["cmp_pe_t"], w["cmp_wk_t"], w["cmp_wv_t"], w["kc_g_col"], w["ones64_kv"], poolm)


def _rows8(row):
    return jnp.broadcast_to(row, (V7X_SUBLANES, row.shape[1]))


def _sample_attn_kernel(pt_ref, *refs, past_len, rows_per_step):
    del pt_ref
    n_page_refs = len(refs) - 25
    npg = n_page_refs // rows_per_step
    (q_all, nsakv_all, winkv_all, ng_all, poolu_all, memq_all, kcvc_all, wincache_all, poolstate_all, memcache_all,
     e_ref, pair_ref, fold_ref, unfold_ref, wpool_ref, pscale_ref,
     onsa_all, pooly_all, memo_all, winout_all,
     qbd_all, bias_all, m_all, l_all, acc_all) = refs[n_page_refs:]
    per_row = (q_all, nsakv_all, winkv_all, ng_all, poolu_all, memq_all, kcvc_all, wincache_all, poolstate_all,
               memcache_all, onsa_all, pooly_all, memo_all, winout_all, qbd_all, bias_all, m_all, l_all, acc_all)
    views = [tuple(ref.at[i] for ref in per_row) for i in range(rows_per_step)]
    row_pages = [refs[i * npg:(i + 1) * npg] for i in range(rows_per_step)]
    step = pl.program_id(1)
    n_steps = pl.num_programs(1)
    nsp = pair_ref.shape[1]
    keys_per_step = npg * PAGE_SIZE
    pos = past_len
    row8 = _iota((8, KV_WIDTH), 0)
    lane8 = _iota((8, KV_WIDTH), 1)
    head_lanes = (lane8 // HEAD_DIM) == (row8 // GROUP)

    def prologue(view):
        (q_ref, nsakv_ref, winkv_ref, ng_ref, poolu_ref, memq_ref, kcvc_ref, wincache_ref, poolstate_ref, memcache_ref,
         onsa_ref, pooly_ref, memo_ref, winout_ref, qbd_scr, bias_scr, m_scr, l_scr, acc_scr) = view
        ncs = kcvc_ref.shape[1]
        q8 = jnp.where((_iota((8, NSA_WIDTH), 1) // HEAD_DIM) == _iota((8, NSA_WIDTH), 0),
                       _rows8(q_ref[...].astype(F32)), 0.0)
        qbd = _dot(q8.astype(BF16), fold_ref[...])
        qbd_scr[...] = qbd
        s = _dot(qbd.astype(BF16), kcvc_ref[0:KV_WIDTH, :].astype(BF16))
        cmask = (_iota((8, ncs), 1) * CMP_BLOCK + (CMP_BLOCK - 1)) <= pos
        p = _masked_softmax_rows(s[None], cmask)[0]
        ocmp = _dot_nt(p.astype(BF16), kcvc_ref[KV_WIDTH:2 * KV_WIDTH, :].astype(BF16))
        acc_scr[0] = jnp.where(head_lanes, ocmp, 0.0)
        eye = _iota((nsp, nsp), 0) == _iota((nsp, nsp), 1)
        ii = _iota((nsp, nsp), 1)
        jj = _iota((nsp, nsp), 0)
        cur = pos // SEL_BLOCK
        sel_rows = []
        for k in range(KV_HEADS):
            imp = jnp.sum(p[k * GROUP:(k + 1) * GROUP], axis=0, keepdims=True)
            imp2 = _dot_split_lhs(_rows8(imp), pair_ref[...], 3)[0:1]
            blk = _iota((1, nsp), 1)
            forced = (blk == 0) | (blk == cur) | (blk == cur - 1)
            valid = blk * SEL_BLOCK <= pos
            score = jnp.where(valid, jnp.where(forced, BIG, imp2), -BIG)
            score_i = jnp.broadcast_to(score, (nsp, nsp))
            score_j = jnp.sum(jnp.where(eye, score_i, 0.0), axis=1, keepdims=True)
            beats = jnp.where((score_j > score_i) | ((score_j == score_i) & (jj < ii)), 1.0, 0.0)
            cnt = jnp.sum(beats, axis=0, keepdims=True)
            cnt = cnt + jnp.where(score < BIG, 1.0, 0.0)
            sel_rows.append(jnp.where(cnt < TOP_K, 1.0, 0.0))
        sel8 = jnp.where(_iota((8, nsp), 0) < GROUP, _rows8(sel_rows[0]), _rows8(sel_rows[1])).astype(BF16)
        for st in range(bias_scr.shape[0]):
            selx = _dot(sel8, e_ref[:, st * keys_per_step:(st + 1) * keys_per_step])
            bias_scr[st] = jnp.where(selx > 0.5, 0.0, NEG)
        m_scr[...] = jnp.full(m_scr.shape, NEG, F32)
        l_scr[...] = jnp.zeros(l_scr.shape, F32)
        acc_scr[1] = jnp.zeros((8, KV_WIDTH), F32)

    def stream(view, pages):
        qbd_scr, bias_scr, m_scr, l_scr, acc_scr = view[-5:]
        qbd_bf = qbd_scr[...].astype(BF16)
        k_t = jnp.concatenate([pg[0:KV_WIDTH, :] for pg in pages], axis=1).astype(BF16)
        v_t = jnp.concatenate([pg[KV_WIDTH:2 * KV_WIDTH, :] for pg in pages], axis=1).astype(BF16)
        s_all = _dot(qbd_bf, k_t) + bias_scr[step]
        m_old = m_scr[...]
        m_new = jnp.maximum(m_old, jnp.max(s_all, axis=-1, keepdims=True))
        p_all = jnp.exp2(s_all - m_new) * jnp.where(s_all > 0.5 * NEG, 1.0, 0.0)
        alpha = jnp.exp2(m_old - m_new)
        l_scr[...] = alpha * l_scr[...] + jnp.sum(p_all, axis=-1, keepdims=True)
        acc_scr[1] = alpha * acc_scr[1] + _dot_nt(p_all.astype(BF16), v_t)
        m_scr[...] = m_new

    def epilogue(view):
        (q_ref, nsakv_ref, winkv_ref, ng_ref, poolu_ref, memq_ref, kcvc_ref, wincache_ref, poolstate_ref, memcache_ref,
         onsa_ref, pooly_ref, memo_ref, winout_ref, qbd_scr, bias_scr, m_scr, l_scr, acc_scr) = view
        qbd = qbd_scr[...]
        qbd_bf = qbd.astype(BF16)
        ks_new = _rows8(nsakv_ref[:, 2 * KV_WIDTH:3 * KV_WIDTH])
        vs_new = _rows8(nsakv_ref[:, 3 * KV_WIDTH:4 * KV_WIDTH])
        s_new = jnp.sum(qbd * ks_new, axis=-1, keepdims=True)
        m_old = m_scr[...]
        m_fin = jnp.maximum(m_old, s_new)
        alpha = jnp.exp2(m_old - m_fin)
        p_new = jnp.exp2(s_new - m_fin)
        l_fin = alpha * l_scr[...] + p_new
        osel = (alpha * acc_scr[1] + p_new * vs_new) * (1.0 / l_fin)
        wb = wincache_ref.shape[1]
        kw_new = _rows8(winkv_ref[:, 0:KV_WIDTH])
        vw_new = _rows8(winkv_ref[:, KV_WIDTH:2 * KV_WIDTH])
        s_w = _dot(qbd_bf, wincache_ref[0:KV_WIDTH, :].astype(BF16))
        kpos = pos - wb + _iota((8, wb), 1)
        wmask = (kpos <= pos) & (kpos > pos - WINDOW)
        s_wn = jnp.sum(qbd * kw_new, axis=-1, keepdims=True)
        s_w = jnp.where(wmask, s_w, NEG)
        m_w = jnp.maximum(jnp.max(s_w, axis=-1, keepdims=True), s_wn)
        p_w = jnp.exp2(s_w - m_w) * jnp.where(wmask, 1.0, 0.0)
        p_wn = jnp.exp2(s_wn - m_w)
        l_w = jnp.sum(p_w, axis=-1, keepdims=True) + p_wn
        owin = (_dot_nt(p_w.astype(BF16), wincache_ref[KV_WIDTH:2 * KV_WIDTH, :].astype(BF16)) + p_wn * vw_new) * (1.0 / l_w)
        ng8 = _rows8(ng_ref[...])
        glane = _iota((8, NSAG_PAD), 1)
        grow = _iota((8, NSAG_PAD), 0)
        gate = lambda br: jnp.sum(jnp.where(glane == N_NSA_BRANCH * grow + br, ng8, 0.0), axis=-1, keepdims=True)
        o8 = jnp.where(head_lanes, gate(0) * acc_scr[0] + gate(1) * osel + gate(2) * owin, 0.0)
        o512 = _dot_split_lhs(o8, unfold_ref[...], 3)
        own = (_iota((8, NSA_WIDTH), 1) // HEAD_DIM) == _iota((8, NSA_WIDTH), 0)
        onsa_ref[...] = jnp.sum(jnp.where(own, o512, 0.0), axis=0, keepdims=True)
        mtok = memcache_ref.shape[0] // (2 * MEM_HEADS)
        outs = []
        for h in range(MEM_HEADS):
            k_h = memcache_ref[pl.ds(h, mtok, stride=2 * MEM_HEADS), :].astype(BF16)
            v_h = memcache_ref[pl.ds(MEM_HEADS + h, mtok, stride=2 * MEM_HEADS), :].astype(BF16)
            q_h = _rows8(memq_ref[:, h * MEM_HEAD_DIM:(h + 1) * MEM_HEAD_DIM].astype(F32)).astype(BF16)
            s_m = _dot_nt(q_h, k_h) * MEM_HEAD_DIM ** -0.5
            p_m = jnp.exp(s_m - jnp.max(s_m, axis=-1, keepdims=True))
            p_m = p_m * (1.0 / jnp.sum(p_m, axis=-1, keepdims=True))
            outs.append(_dot(p_m.astype(BF16), v_h)[0:1])
        memo_ref[...] = jnp.concatenate(outs, axis=1)
        u_new = poolu_ref[...]
        state = poolstate_ref[...]
        srow = _iota((POOL_BUF, POOL_GROUP_WIDTH), 0)
        ys = []
        for gi, win in enumerate(POOL_WINDOWS):
            lo = gi * POOL_GROUP_WIDTH
            u_g = u_new[:, lo:lo + POOL_GROUP_WIDTH]
            tail = jnp.sum(jnp.where(srow >= POOL_BUF - (win - 1), state[:, lo:lo + POOL_GROUP_WIDTH], 0.0),
                           axis=0, keepdims=True)
            d = (tail + u_g) / float(min(pos + 1, win)) - u_g
            ys.append(_dot(_rows8(d).astype(BF16), wpool_ref[gi])[0:1])
        pooly_ref[...] = jnp.concatenate(ys, axis=1) * pscale_ref[...]
        nf = 2 * KV_WIDTH
        eye_f = _iota((nf, nf), 0) == _iota((nf, nf), 1)
        new_col = jnp.sum(jnp.where(eye_f, jnp.broadcast_to(winkv_ref[...], (nf, nf)), 0.0), axis=1, keepdims=True)
        shifted = pltpu.roll(wincache_ref[...], wb - 1, axis=1)
        winout_ref[...] = jnp.where(_iota((nf, wb), 1) == wb - 1, new_col, shifted)

    @pl.when(step == 0)
    def _():
        for view in views:
            prologue(view)

    for view, pages in zip(views, row_pages):
        stream(view, pages)

    @pl.when(step == n_steps - 1)
    def _():
        for view in views:
            epilogue(view)


def _sample_attn(l, q, nsakv, winkv, ng, pool_u, memq, kcvc_s, cache_t, wincache_t, poolstate, memcache, page_table,
                 w, consts):
    bd, n_pages = page_table.shape
    npg = min(PAGES_PER_STEP_SEL, n_pages)
    n_steps = n_pages // npg
    past_len = n_pages * PAGE_SIZE
    ncs = past_len // CMP_BLOCK
    nsp = past_len // SEL_BLOCK
    wb = wincache_t.shape[3]
    mrows = memcache.shape[2]
    rb = ROWS_PER_DECODE_STEP if bd % ROWS_PER_DECODE_STEP == 0 else 1
    row = lambda w: pl.BlockSpec((rb, 1, w), lambda b, s, pt: (b, 0, 0))
    const = lambda shape: pl.BlockSpec(shape, lambda b, s, pt: (0,) * len(shape))
    per_b = lambda r, c: pl.BlockSpec((None, rb, r, c), lambda b, s, pt: (l, b, 0, 0))
    page_specs = []
    for i in range(rb):
        page_specs += _page_specs(npg, 1, lambda g: l, lambda g, i=i: g[0] * rb + i, lambda g: g[1])
    in_specs = (
        page_specs
        + [row(NSA_WIDTH), row(4 * KV_WIDTH), row(2 * KV_WIDTH), row(NSAG_PAD), row(POOL_WIDTH), row(MEM_WIDTH),
           per_b(2 * KV_WIDTH, ncs), per_b(2 * KV_WIDTH, wb), per_b(POOL_BUF, POOL_WIDTH), per_b(mrows, MEM_HEAD_DIM),
           const((nsp, past_len)), const((ncs, nsp)), const((NSA_WIDTH, KV_WIDTH)), const((KV_WIDTH, NSA_WIDTH)),
           pl.BlockSpec((None, POOL_GROUPS, POOL_GROUP_WIDTH, POOL_GROUP_WIDTH), lambda b, s, pt: (l, 0, 0, 0)),
           pl.BlockSpec((None, 1, POOL_WIDTH), lambda b, s, pt: (l, 0, 0))])
    grid_spec = pltpu.PrefetchScalarGridSpec(
        num_scalar_prefetch=1,
        grid=(bd // rb, n_steps),
        in_specs=in_specs,
        out_specs=[row(NSA_WIDTH), row(POOL_WIDTH), row(MEM_WIDTH),
                   pl.BlockSpec((rb, 2 * KV_WIDTH, wb), lambda b, s, pt: (b, 0, 0))],
        scratch_shapes=[pltpu.VMEM((rb, 8, KV_WIDTH), F32), pltpu.VMEM((rb, n_steps, 8, npg * PAGE_SIZE), F32),
                        pltpu.VMEM((rb, 8, 1), F32), pltpu.VMEM((rb, 8, 1), F32), pltpu.VMEM((rb, 2, 8, KV_WIDTH), F32)],
    )
    out = jax.ShapeDtypeStruct((bd, 1, NSA_WIDTH), F32)
    return pl.pallas_call(
        functools.partial(_sample_attn_kernel, past_len=past_len, rows_per_step=rb),
        grid_spec=grid_spec,
        out_shape=[out, out, out, jax.ShapeDtypeStruct((bd, 2 * KV_WIDTH, wb), F32)],
        compiler_params=_params("parallel", "arbitrary"),
        name="sample_attn",
    )(page_table, *([cache_t] * (rb * npg)), q, nsakv, winkv, ng, pool_u, memq, kcvc_s, wincache_t, poolstate, memcache,
      consts["sel_expand_s"], consts["pair_s"], consts["fold"], consts["unfold"], w["w_pool"], w["pool_scale"])


def _block_diag_ones(n, seg):
    return (jnp.arange(n)[:, None] // seg == jnp.arange(n)[None, :] // seg).astype(BF16)


def _constants(t, past_len):
    nc, ns = t // CMP_BLOCK, t // SEL_BLOCK
    ratio = SEL_BLOCK // CMP_BLOCK
    pair_t = (jnp.arange(ns)[:, None] == jnp.arange(nc)[None, :] // ratio).astype(BF16)
    ncs, nsp = past_len // CMP_BLOCK, past_len // SEL_BLOCK
    sel_expand_s = (jnp.arange(nsp)[:, None] == (jnp.arange(past_len) // SEL_BLOCK)[None, :]).astype(BF16)
    pair_s = (jnp.arange(ncs)[:, None] // ratio == jnp.arange(nsp)[None, :]).astype(BF16)
    c = jnp.arange(NSA_WIDTH)
    fold_col = (c // (GROUP * HEAD_DIM)) * HEAD_DIM + c % HEAD_DIM
    fold = (fold_col[:, None] == jnp.arange(KV_WIDTH)[None, :]).astype(BF16)
    return {
        "ones64": _block_diag_ones(V7X_MXU_DIM, HEAD_DIM), "ones128": _block_diag_ones(V7X_MXU_DIM, MEM_HEAD_DIM),
        "ones64_kv": _block_diag_ones(KV_WIDTH, HEAD_DIM),
        "pair_t": pair_t,
        "sel_expand_s": sel_expand_s, "pair_s": pair_s, "fold": fold, "unfold": fold.T,
    }


def kernel(x_prompt, x_sample, mem_prompt, cache_nsa_kv, cache_win_kv, state_pool, cache_mem_kv, page_table,
           attn_norm_g, w_in, nsa_q_g, nsa_kc_g, nsa_ks_g, nsa_kw_g, cmp_pe_k, cmp_pe_v, cmp_wk, cmp_wv,
           w_pool, pool_scale, mem_norm_g, w_mem_kv, mem_q_g, mem_k_g, w_up_nsa, w_up_pool, w_up_mem, w_out,
           ffn_norm_g, w_gate_up, w_down):
    depth = w_in.shape[0]
    b, t, _ = x_prompt.shape
    bd = x_sample.shape[0]
    n_pages = page_table.shape[1]
    past_len = n_pages * PAGE_SIZE
    mtok = mem_prompt.shape[1]
    wb = cache_win_kv.shape[2]
    assert x_sample.shape[1] == 1 and wb == WINDOW and past_len >= WINDOW and mtok == cache_mem_kv.shape[2]
    consts = _constants(t, past_len)

    nsag_lo = NSA_WIDTH + 6 * KV_WIDTH
    nsag_hi = nsag_lo + N_NSA_BRANCH * N_HEADS
    w_t = w_in.transpose(0, 2, 1)
    w_in_t = jnp.concatenate(
        [w_t[:, :nsag_lo], w_t[:, nsag_hi:], w_t[:, nsag_lo:nsag_hi],
         jnp.zeros((depth, NSAG_PAD - N_NSA_BRANCH * N_HEADS, D_MODEL), w_in.dtype)], axis=1).astype(BF16)
    eye_kv = jnp.eye(KV_HEADS, dtype=F32)
    rows = lambda g, reps: jnp.tile(g, (1, reps))[:, None, :].astype(F32)
    w = {
        "attn_g": attn_norm_g[:, None, :], "w_in_t": w_in_t,
        "q_g": rows(nsa_q_g, N_HEADS), "ks_g": rows(nsa_ks_g, KV_HEADS), "kw_g": rows(nsa_kw_g, KV_HEADS),
        "mq_g": rows(mem_q_g, MEM_HEADS), "kc_g": rows(nsa_kc_g, KV_HEADS), "mk_g": rows(mem_k_g, MEM_HEADS),
        "cmp_pe": jnp.concatenate([jnp.tile(cmp_pe_k, (1, 1, KV_HEADS)), jnp.tile(cmp_pe_v, (1, 1, KV_HEADS))], axis=2),
        "cmp_wk": jax.vmap(lambda m: jnp.kron(eye_kv, m))(cmp_wk), "cmp_wv": jax.vmap(lambda m: jnp.kron(eye_kv, m))(cmp_wv),
        "w_pool": w_pool.astype(BF16), "pool_scale": pool_scale[:, None, :],
        "mem_g": mem_norm_g[:, None, :], "w_mem_kv": w_mem_kv.astype(BF16),
        "w_mem_v_t": w_mem_kv[:, :, MEM_WIDTH:].transpose(0, 2, 1).astype(BF16),
        "w_up_nsa": w_up_nsa.astype(BF16), "w_up_pool": w_up_pool.astype(BF16), "w_up_mem": w_up_mem.astype(BF16),
        "w_out": w_out.astype(BF16), "ffn_g": ffn_norm_g[:, None, :],
        "w_gate_up": w_gate_up.astype(BF16), "w_down": w_down.astype(BF16),
    }
    swap = lambda a: a.transpose(0, 2, 1)
    cmp_w = {"cmp_pe_t": swap(w["cmp_pe"]), "cmp_wk_t": swap(w["cmp_wk"]), "cmp_wv_t": swap(w["cmp_wv"]),
             "kc_g_col": swap(w["kc_g"]), "ones64_kv": consts["ones64_kv"]}

    cache_t = cache_nsa_kv.transpose(0, 1, 3, 4, 5, 2).reshape(depth, cache_nsa_kv.shape[1], 4 * KV_WIDTH, PAGE_SIZE)
    wincache_t = cache_win_kv.transpose(0, 1, 3, 4, 5, 2).reshape(depth, bd, 2 * KV_WIDTH, wb)
    memcache = cache_mem_kv.reshape(depth, bd, cache_mem_kv.shape[2] * 2 * MEM_HEADS, MEM_HEAD_DIM)
    kcvc_s = _compress_pages(cache_t, page_table, cmp_w)

    xp = x_prompt.reshape(b * t, D_MODEL)
    xs = x_sample.reshape(bd, D_MODEL)
    nsa_p, nsa_s, win_p, win_s, pool_p, pool_s, mem_p = [], [], [], [], [], [], []
    for l in range(depth):
        q, nsakv, winkv, kext, kwbf, vt, pool_u, memq, mg, ng = _inproj(xp, l, w, consts, t)
        mkv, mk, mvt = _memkv(mem_prompt, l, w, consts)
        nsakv3 = nsakv.reshape(b, t, 4 * KV_WIDTH)
        kc, vct = _compress_prompt(nsakv3, l, w, consts)
        pool3 = pool_u.reshape(b, t, POOL_WIDTH)
        by_b = lambda a: a.reshape(b, t, a.shape[-1])
        onsa, pooly, memo = _prompt_attn(
            by_b(q), kc, vct, by_b(kext), by_b(kwbf), vt.reshape(b, -1, vt.shape[1], vt.shape[2]), pool3, by_b(memq),
            mk, mvt, by_b(ng), l, w, consts)
        xp = _merge_ffn(xp, onsa.reshape(b * t, NSA_WIDTH), pooly.reshape(b * t, POOL_WIDTH),
                        memo.reshape(b * t, MEM_WIDTH), mg, l, w)
        nsa_p.append(nsakv3.reshape(b, t, 4, KV_HEADS, HEAD_DIM))
        win_rows = winkv.reshape(b, t, 2 * KV_WIDTH)[:, t - min(WINDOW, t):]
        win_p.append(win_rows.reshape(b, win_rows.shape[1], 2, KV_HEADS, HEAD_DIM))
        pool_p.append(pool3[:, t - POOL_BUF:])
        mem_p.append(mkv.reshape(b, mtok, 2, MEM_HEADS, MEM_HEAD_DIM))
        q, nsakv, winkv, _, _, _, pool_u, memq, mg, ng = _inproj(xs, l, w, consts, bd)
        r3 = lambda a: a.reshape(bd, 1, a.shape[-1])
        onsa, pooly, memo, win_t = _sample_attn(l, r3(q), r3(nsakv), r3(winkv), r3(ng), r3(pool_u), r3(memq), kcvc_s,
                                                cache_t, wincache_t, state_pool, memcache, page_table, w, consts)
        xs = _merge_ffn(xs, onsa.reshape(bd, NSA_WIDTH), pooly.reshape(bd, POOL_WIDTH), memo.reshape(bd, MEM_WIDTH), mg, l, w)
        nsa_s.append(nsakv.reshape(bd, 1, 4, KV_HEADS, HEAD_DIM))
        win_s.append(win_t)
        pool_s.append(jnp.concatenate([state_pool[l][:, 1:], pool_u.reshape(bd, 1, POOL_WIDTH)], axis=1))
    win_s_out = jnp.stack(win_s).reshape(depth, bd, 2, KV_HEADS, HEAD_DIM, wb).transpose(0, 1, 5, 2, 3, 4)
    return (xp.reshape(b, t, D_MODEL), xs.reshape(bd, 1, D_MODEL), jnp.stack(nsa_p), jnp.stack(nsa_s),
            jnp.stack(win_p), win_s_out, jnp.stack(pool_p), jnp.stack(pool_s), jnp.stack(mem_p))
```
